```python
import math
import jax, jax.numpy as jnp
from jax import lax
import numpy as np

D_MODEL = 4096
BATCH = 4
SEQ = 2048
DEPTH = 2
DEC_BATCH = 128
DEC_SEQ = 4
PAST_LEN = 16384
PAGE_SIZE = 128

POOL_WIDTH = D_MODEL // 2
POOL_WINDOWS = (2, 4, 8, 16)
N_POOL_GROUPS = len(POOL_WINDOWS)
POOL_GROUP = POOL_WIDTH // N_POOL_GROUPS
POOL_HIST = max(POOL_WINDOWS) - 1
MLSTM_WIDTH = D_MODEL - POOL_WIDTH
N_HEADS = 4
HEAD_V = MLSTM_WIDTH // N_HEADS
HEAD_QK = HEAD_V // 2
QK_WIDTH = N_HEADS * HEAD_QK
CHUNK = 64
SPLIT_POINTS = (POOL_WIDTH,
                POOL_WIDTH + QK_WIDTH,
                POOL_WIDTH + 2 * QK_WIDTH,
                POOL_WIDTH + 2 * QK_WIDTH + MLSTM_WIDTH,
                POOL_WIDTH + 2 * QK_WIDTH + 2 * MLSTM_WIDTH,
                POOL_WIDTH + 2 * QK_WIDTH + 2 * MLSTM_WIDTH + N_HEADS)
N_IN = POOL_WIDTH + 2 * QK_WIDTH + 2 * MLSTM_WIDTH + 2 * N_HEADS
D_FF = ((8 * D_MODEL // 3 + 255) // 256) * 256
FFN_RES = 0.5
N_SUB = 3
N_MOD = 3
EPS = 1e-6

kernel_name = "hybrid_pool_mlstm_macaron_step"


def rms_norm(x, g):
    xf = x.astype(jnp.float32)
    y = xf * lax.rsqrt(jnp.mean(xf * xf, axis=-1, keepdims=True) + EPS)
    return (y * g.astype(jnp.float32)).astype(x.dtype)


def modulated_norm(x, g, shift, scale):
    return rms_norm(x, g) * (1 + scale[:, None, :]) + shift[:, None, :]


def swiglu(h, w1, w3, w2):
    a = jnp.einsum('btd,df->btf', h, w1)
    b = jnp.einsum('btd,df->btf', h, w3)
    return jnp.einsum('btf,fd->btd', jax.nn.silu(a) * b, w2)


def pool_mix(u, hist, start, w_pool, s_pool):
    T = u.shape[1]
    z = jnp.concatenate([hist.astype(u.dtype), u], axis=1)
    zf = z.astype(jnp.float32)
    cs = jnp.concatenate([jnp.zeros_like(zf[:, :1]), jnp.cumsum(zf, axis=1)], axis=1)
    pos = start + jnp.arange(T)
    outs = []
    for g, w in enumerate(POOL_WINDOWS):
        sl = slice(g * POOL_GROUP, (g + 1) * POOL_GROUP)
        hi = cs[:, POOL_HIST + 1:, sl]
        lo = cs[:, POOL_HIST + 1 - w:POOL_HIST + 1 - w + T, sl]
        cnt = jnp.minimum(w, pos + 1).astype(jnp.float32)[None, :, None]
        pooled = (hi - lo) / cnt - zf[:, POOL_HIST:, sl]
        outs.append(jnp.einsum('btc,cd->btd', pooled.astype(u.dtype), w_pool[g]))
    y = jnp.concatenate(outs, axis=-1) * s_pool
    return y, z[:, -POOL_HIST:]


def mlstm(q, k, v, ig, fg, C0, n0, m0):
    f32 = jnp.float32
    B, T = q.shape[:2]
    L = math.gcd(T, CHUNK)
    NC = T // L
    q = q.astype(f32)
    k = k.astype(f32) * (HEAD_QK ** -0.5)
    v = v.astype(f32)
    ig = ig.astype(f32)
    logf = jax.nn.log_sigmoid(fg.astype(f32))

    def to_chunks(a):
        a = a.reshape(B, NC, L, *a.shape[2:])
        return jnp.moveaxis(a, (1, 3), (0, 2))

    causal = jnp.tril(jnp.ones((L, L), dtype=bool))

    def step(carry, xs):
        C, n, m = carry
        qc, kc, vc, ic, lfc = xs
        b = jnp.cumsum(lfc, axis=-1)
        dmat = b[..., :, None] - b[..., None, :] + ic[..., None, :]
        dmat = jnp.where(causal, dmat, -jnp.inf)
        inter = b + m[..., None]
        m_tok = jnp.maximum(inter, jnp.max(dmat, axis=-1))
        w_intra = jnp.exp(dmat - m_tok[..., None])
        w_inter = jnp.exp(inter - m_tok)
        s = jnp.einsum('bhtd,bhsd->bhts', qc, kc) * w_intra
        num = (jnp.einsum('bhts,bhsv->bhtv', s, vc)
               + w_inter[..., None] * jnp.einsum('bhvd,bhtd->bhtv', C, qc))
        den = jnp.sum(s, axis=-1) + w_inter * jnp.einsum('bhd,bhtd->bht', n, qc)
        h = num / jnp.maximum(jnp.abs(den), jnp.exp(-m_tok))[..., None]
        bL = b[..., -1]
        dec = bL[..., None] - b + ic
        m_new = jnp.maximum(bL + m, jnp.max(dec, axis=-1))
        ws = jnp.exp(dec - m_new[..., None])
        wc = jnp.exp(bL + m - m_new)
        C_new = wc[..., None, None] * C + jnp.einsum('bhs,bhsv,bhsd->bhvd', ws, vc, kc)
        n_new = wc[..., None] * n + jnp.einsum('bhs,bhsd->bhd', ws, kc)
        return (C_new, n_new, m_new), h

    xs = (to_chunks(q), to_chunks(k), to_chunks(v), to_chunks(ig), to_chunks(logf))
    carry0 = (C0.astype(f32), n0.astype(f32), m0.astype(f32))
    (C, n, m), h = lax.scan(step, carry0, xs)
    h = jnp.moveaxis(h, (0, 2), (1, 3)).reshape(B, T, N_HEADS, HEAD_V)
    return h, C, n, m


def token_mixer(h, pool_hist, C0, n0, m0, start, w_in, b_in, w_pool, s_pool, g_head, w_out):
    B, T, _ = h.shape
    proj = jnp.einsum('btd,dn->btn', h, w_in) + b_in
    u, q, k, v, o, ig, fg = jnp.split(proj, SPLIT_POINTS, axis=-1)
    y_pool, new_hist = pool_mix(u, pool_hist, start, w_pool, s_pool)
    hm, C, n, m = mlstm(q.reshape(B, T, N_HEADS, HEAD_QK), k.reshape(B, T, N_HEADS, HEAD_QK),
                        v.reshape(B, T, N_HEADS, HEAD_V), ig, fg, C0, n0, m0)
    hm = hm * lax.rsqrt(jnp.mean(hm * hm, axis=-1, keepdims=True) + EPS) * g_head.astype(jnp.float32)
    y_m = (hm.reshape(B, T, MLSTM_WIDTH) * jax.nn.sigmoid(o.astype(jnp.float32))).astype(h.dtype)
    y = jnp.concatenate([y_pool, y_m], axis=-1)
    return jnp.einsum('btd,de->bte', y, w_out), new_hist, C, n, m


def layer(x, c, pool_hist, C0, n0, m0, start, w_ada, b_ada, g_norm, w_in, b_in, w_pool, s_pool,
          g_head, w_out, w1, w3, w2):
    B = x.shape[0]
    mods = (jnp.einsum('bd,de->be', jax.nn.silu(c), w_ada) + b_ada).reshape(B, N_SUB, N_MOD, D_MODEL)
    h = modulated_norm(x, g_norm[0], mods[:, 0, 0], mods[:, 0, 1])
    x = x + FFN_RES * mods[:, 0, 2][:, None, :] * swiglu(h, w1[0], w3[0], w2[0])
    h = modulated_norm(x, g_norm[1], mods[:, 1, 0], mods[:, 1, 1])
    y, new_hist, C, n, m = token_mixer(h, pool_hist, C0, n0, m0, start, w_in, b_in, w_pool, s_pool,
                                       g_head, w_out)
    x = x + mods[:, 1, 2][:, None, :] * y
    h = modulated_norm(x, g_norm[2], mods[:, 2, 0], mods[:, 2, 1])
    x = x + FFN_RES * mods[:, 2, 2][:, None, :] * swiglu(h, w1[1], w3[1], w2[1])
    return x, new_hist, C, n, m


def setup_inputs(seed: int = 0) -> dict:
    key = jax.random.key(seed)
    ks = jax.random.split(key, 24)
    f32 = jnp.float32
    nrm = lambda k, shape, s: jax.random.normal(k, shape, f32) * s
    b_in = nrm(ks[13], (DEPTH, N_IN), 0.02)
    b_in = b_in.at[:, N_IN - N_HEADS:].add(jnp.linspace(3.0, 6.0, N_HEADS, dtype=f32))
    return {
        "x_prompt": nrm(ks[0], (BATCH, SEQ, D_MODEL), 1.0),
        "x_sample": nrm(ks[1], (DEC_BATCH, DEC_SEQ, D_MODEL), 1.0),
        "state_pool": nrm(ks[2], (DEPTH, DEC_BATCH, POOL_HIST, POOL_WIDTH), 1.0),
        "state_C": nrm(ks[3], (DEPTH, DEC_BATCH, N_HEADS, HEAD_V, HEAD_QK), 0.1),
        "state_n": nrm(ks[4], (DEPTH, DEC_BATCH, N_HEADS, HEAD_QK), 0.5),
        "state_m": nrm(ks[5], (DEPTH, DEC_BATCH, N_HEADS), 1.0),
        "c_prompt": nrm(ks[6], (BATCH, D_MODEL), 1.0),
        "c_sample": nrm(ks[7], (DEC_BATCH, D_MODEL), 1.0),
        "w_ada": nrm(ks[8], (DEPTH, D_MODEL, N_SUB * N_MOD * D_MODEL), D_MODEL ** -0.5),
        "b_ada": nrm(ks[9], (DEPTH, N_SUB * N_MOD * D_MODEL), 0.02),
        "g_norm": 1.0 + nrm(ks[10], (DEPTH, N_SUB, D_MODEL), 0.05),
        "w_in": nrm(ks[11], (DEPTH, D_MODEL, N_IN), D_MODEL ** -0.5),
        "b_in": b_in,
        "w_pool": nrm(ks[12], (DEPTH, N_POOL_GROUPS, POOL_GROUP, POOL_GROUP), POOL_GROUP ** -0.5),
        "s_pool": 1.0 + nrm(ks[14], (DEPTH, POOL_WIDTH), 0.05),
        "g_head": 1.0 + nrm(ks[15], (DEPTH, N_HEADS, HEAD_V), 0.05),
        "w_out": nrm(ks[16], (DEPTH, D_MODEL, D_MODEL), D_MODEL ** -0.5),
        "w1": nrm(ks[17], (DEPTH, 2, D_MODEL, D_FF), D_MODEL ** -0.5),
        "w3": nrm(ks[18], (DEPTH, 2, D_MODEL, D_FF), D_MODEL ** -0.5),
        "w2": nrm(ks[19], (DEPTH, 2, D_FF, D_MODEL), D_FF ** -0.5),
        "g_final": 1.0 + nrm(ks[20], (D_MODEL,), 0.05),
    }


def reference(x_prompt, x_sample, state_pool, state_C, state_n, state_m, c_prompt, c_sample,
              w_ada, b_ada, g_norm, w_in, b_in, w_pool, s_pool, g_head, w_out, w1, w3, w2, g_final):
    xp, xs = x_prompt, x_sample
    Bp = xp.shape[0]
    hist0 = jnp.zeros((Bp, POOL_HIST, POOL_WIDTH), xp.dtype)
    C0 = jnp.zeros((Bp, N_HEADS, HEAD_V, HEAD_QK), jnp.float32)
    n0 = jnp.zeros((Bp, N_HEADS, HEAD_QK), jnp.float32)
    m0 = jnp.zeros((Bp, N_HEADS), jnp.float32)
    pool_p, C_p, n_p, m_p = [], [], [], []
    pool_s, C_s, n_s, m_s = [], [], [], []
    for l in range(DEPTH):
        lw = (w_ada[l], b_ada[l], g_norm[l], w_in[l], b_in[l], w_pool[l], s_pool[l], g_head[l],
              w_out[l], w1[l], w3[l], w2[l])
        xp, hp, Cp, npp, mp = layer(xp, c_prompt, hist0, C0, n0, m0, 0, *lw)
        xs, hs, Cs, nss, mss = layer(xs, c_sample, state_pool[l], state_C[l], state_n[l], state_m[l],
                                     PAST_LEN, *lw)
        pool_p.append(hp); C_p.append(Cp); n_p.append(npp); m_p.append(mp)
        pool_s.append(hs); C_s.append(Cs); n_s.append(nss); m_s.append(mss)
    y_prompt = rms_norm(xp, g_final)
    y_sample = rms_norm(xs, g_final)
    return (y_prompt, y_sample,
            jnp.stack(pool_p), jnp.stack(C_p), jnp.stack(n_p), jnp.stack(m_p),
            jnp.stack(pool_s), jnp.stack(C_s), jnp.stack(n_s), jnp.stack(m_s))
```

```python
import functools

import jax
import jax.numpy as jnp
from jax import lax
from jax.experimental import pallas as pl
from jax.experimental.pallas import tpu as pltpu

F32 = jnp.float32
BF16 = jnp.bfloat16

EPS = 1e-6
FFN_RES = 0.5
POOL_WINDOWS = (2, 4, 8, 16)
N_HEADS = 4
N_SUB = 3
N_MOD = 3
PROMPT_CHUNK = 256
SAMPLE_PAIR = 4
NEG_BIG = -1e30

VMEM_LIMIT = 56 * 1024 * 1024

ROW_TILE = 256
MM_ROW_TILE = 1088
ACC_ROW_TILE = 2048
ACC_K_TILE = 1024
ACC_N_TILE = 1024


def _cparams(sem):
    return pltpu.CompilerParams(dimension_semantics=sem, vmem_limit_bytes=VMEM_LIMIT)


def _dot(a, b):
    return lax.dot_general(a, b, (((1,), (0,)), ((), ())), preferred_element_type=F32)


def _dot_nt(a, b):
    return lax.dot_general(a, b, (((1,), (1,)), ((), ())), preferred_element_type=F32)


def _dot_tn(a, b):
    return lax.dot_general(a, b, (((0,), (0,)), ((), ())), preferred_element_type=F32)


def _ada_kernel(c_ref, w_ref, b_ref, o_ref):
    c = c_ref[...]
    sc = (c * jax.nn.sigmoid(c)).astype(BF16)
    o_ref[...] = _dot(sc, w_ref[...]) + b_ref[...]


def _ada(c_all, w_ada, b_ada, tn=512):
    depth, d, n = w_ada.shape
    rows = c_all.shape[0]
    return pl.pallas_call(
        _ada_kernel,
        grid=(depth, n // tn),
        in_specs=[
            pl.BlockSpec((rows, d), lambda l, j: (0, 0)),
            pl.BlockSpec((None, d, tn), lambda l, j: (l, 0, j)),
            pl.BlockSpec((None, 1, tn), lambda l, j: (l, 0, j)),
        ],
        out_specs=pl.BlockSpec((None, rows, tn), lambda l, j: (l, 0, j)),
        out_shape=jax.ShapeDtypeStruct((depth, rows, n), F32),
        compiler_params=_cparams(("arbitrary", "arbitrary")),
        name="ada_mods",
    )(c_all, w_ada, b_ada.reshape(depth, 1, n))


def _rms(x, g):
    return x * lax.rsqrt(jnp.mean(x * x, axis=-1, keepdims=True) + EPS) * g


def _normmod_kernel(n_prompt_tiles, x_ref, g_ref, shp_ref, scp_ref, shs_ref, scs_ref, o_ref):
    i = pl.program_id(0)
    y = _rms(x_ref[...], g_ref[...])

    @pl.when(i < n_prompt_tiles)
    def _():
        o_ref[...] = (y * (1.0 + scp_ref[...]) + shp_ref[...]).astype(o_ref.dtype)

    @pl.when(i >= n_prompt_tiles)
    def _():
        o_ref[...] = (y * (1.0 + scs_ref[...]) + shs_ref[...]).astype(o_ref.dtype)


def _normmod(x, g, shift_p, scale_p, shift_s, scale_s, seq):
    m, d = x.shape
    tr = ROW_TILE
    n_bp = shift_p.shape[0]
    n_pt = n_bp * seq // tr
    tiles_per_seq = seq // tr
    p_map = lambda i: (jnp.minimum(i // tiles_per_seq, n_bp - 1), 0, 0)
    s_map = lambda i: (jnp.maximum(i - n_pt, 0), 0)
    return pl.pallas_call(
        functools.partial(_normmod_kernel, n_pt),
        grid=(m // tr,),
        in_specs=[
            pl.BlockSpec((tr, d), lambda i: (i, 0)),
            pl.BlockSpec((1, d), lambda i: (0, 0)),
            pl.BlockSpec((None, 1, d), p_map),
            pl.BlockSpec((None, 1, d), p_map),
            pl.BlockSpec((tr, d), s_map),
            pl.BlockSpec((tr, d), s_map),
        ],
        out_specs=pl.BlockSpec((tr, d), lambda i: (i, 0)),
        out_shape=jax.ShapeDtypeStruct((m, d), BF16),
        compiler_params=_cparams(("arbitrary",)),
        name="normmod",
    )(x, g.reshape(1, d), shift_p, scale_p, shift_s, scale_s)


def _final_norm_kernel(x_ref, g_ref, o_ref):
    o_ref[...] = _rms(x_ref[...], g_ref[...])


def _final_norm(x, g, row0, rows):
    d = x.shape[1]
    tr = ROW_TILE
    b0 = row0 // tr
    return pl.pallas_call(
        _final_norm_kernel,
        grid=(rows // tr,),
        in_specs=[pl.BlockSpec((tr, d), lambda i: (i + b0, 0)),
                  pl.BlockSpec((1, d), lambda i: (0, 0))],
        out_specs=pl.BlockSpec((tr, d), lambda i: (i, 0)),
        out_shape=jax.ShapeDtypeStruct((rows, d), F32),
        compiler_params=_cparams(("arbitrary",)),
        name="final_norm",
    )(x, g.reshape(1, d))


def _swiglu_up_kernel(x_ref, w1_ref, w3_ref, o_ref):
    x = x_ref[...]
    a = _dot(x, w1_ref[...])
    b = _dot(x, w3_ref[...])
    o_ref[...] = (a * jax.nn.sigmoid(a) * b).astype(o_ref.dtype)


def _swiglu_up(h, w1, w3, layer, sub, tn=256):
    m, d = h.shape
    f = w1.shape[-1]
    tm = MM_ROW_TILE
    w_spec = pl.BlockSpec((None, None, d, tn), lambda j, i: (layer, sub, 0, j))
    return pl.pallas_call(
        _swiglu_up_kernel,
        grid=(f // tn, m // tm),
        in_specs=[pl.BlockSpec((tm, d), lambda j, i: (i, 0)), w_spec, w_spec],
        out_specs=pl.BlockSpec((tm, tn), lambda j, i: (i, j)),
        out_shape=jax.ShapeDtypeStruct((m, f), BF16),
        compiler_params=_cparams(("arbitrary", "arbitrary")),
        name="swiglu_up",
    )(h, w1, w3)


def _proj_kernel(x_ref, w_ref, b_ref, o_ref):
    o_ref[...] = (_dot(x_ref[...], w_ref[...]) + b_ref[...]).astype(o_ref.dtype)


def _in_proj(h, w_in, b_in3, layer, n_main, tn=512):
    m, d = h.shape
    tm = MM_ROW_TILE
    return pl.pallas_call(
        _proj_kernel,
        grid=(n_main // tn, m // tm),
        in_specs=[pl.BlockSpec((tm, d), lambda j, i: (i, 0)),
                  pl.BlockSpec((None, d, tn), lambda j, i: (layer, 0, j)),
                  pl.BlockSpec((None, 1, tn), lambda j, i: (layer, 0, j))],
        out_specs=pl.BlockSpec((tm, tn), lambda j, i: (i, j)),
        out_shape=jax.ShapeDtypeStruct((m, n_main), BF16),
        compiler_params=_cparams(("arbitrary", "arbitrary")),
        name="in_proj",
    )(h, w_in, b_in3)


def _gate_proj(h, w_in, b_in3, layer, n_main, tn=128):
    m, d = h.shape
    tm = MM_ROW_TILE
    jb = n_main // tn
    return pl.pallas_call(
        _proj_kernel,
        grid=(m // tm,),
        in_specs=[pl.BlockSpec((tm, d), lambda i: (i, 0)),
                  pl.BlockSpec((None, d, tn), lambda i: (layer, 0, jb)),
                  pl.BlockSpec((None, 1, tn), lambda i: (layer, 0, jb))],
        out_specs=pl.BlockSpec((tm, tn), lambda i: (i, 0)),
        out_shape=jax.ShapeDtypeStruct((m, tn), F32),
        compiler_params=_cparams(("arbitrary",)),
        name="gate_proj",
    )(h, w_in, b_in3)


def _acc_resid_kernel(n_prompt_tiles, sample_rows, k_last_valid, res_scale,
                      a_ref, w_ref, x_ref, gp_ref, gs_ref, o_ref):
    i = pl.program_id(0)
    k = pl.program_id(2)
    nk = pl.num_programs(2)

    def accumulate(rows, kv):
        for r0 in range(0, rows, sample_rows):
            rs = slice(r0, r0 + sample_rows)
            part = _dot(a_ref[rs, 0:kv], w_ref[0:kv, :])

            @pl.when(k == 0)
            def _():
                o_ref[rs, :] = part

            @pl.when(k > 0)
            def _():
                o_ref[rs, :] += part

    def step(rows):
        if k_last_valid == a_ref.shape[1]:
            accumulate(rows, k_last_valid)
        else:
            @pl.when(k < nk - 1)
            def _():
                accumulate(rows, a_ref.shape[1])

            @pl.when(k == nk - 1)
            def _():
                accumulate(rows, k_last_valid)

    @pl.when(i < n_prompt_tiles)
    def _():
        step(a_ref.shape[0])

        @pl.when(k == nk - 1)
        def _():
            o_ref[...] = x_ref[...] + (res_scale * gp_ref[...]) * o_ref[...]

    @pl.when(i >= n_prompt_tiles)
    def _():
        step(sample_rows)

        @pl.when(k == nk - 1)
        def _():
            o_ref[0:sample_rows, :] = (x_ref[0:sample_rows, :]
                                       + (res_scale * gs_ref[...]) * o_ref[0:sample_rows, :])


def _acc_resid(a, w_full, w_index, x, gate_p, gate_s, res_scale):
    m, kdim = a.shape
    d = x.shape[1]
    tm, tk, tn = ACC_ROW_TILE, ACC_K_TILE, ACC_N_TILE
    n_bp = gate_p.shape[0]
    sample_rows = gate_s.shape[0]
    assert m == n_bp * tm + sample_rows and tm % sample_rows == 0
    nk = pl.cdiv(kdim, tk)
    k_last_valid = kdim - (nk - 1) * tk
    lead = (None,) * len(w_index)
    return pl.pallas_call(
        functools.partial(_acc_resid_kernel, n_bp, sample_rows, k_last_valid, res_scale),
        grid=(n_bp + 1, d // tn, nk),
        in_specs=[
            pl.BlockSpec((tm, tk), lambda i, j, k: (i, k)),
            pl.BlockSpec(lead + (tk, tn), lambda i, j, k: tuple(w_index) + (k, j)),
            pl.BlockSpec((tm, tn), lambda i, j, k: (i, j)),
            pl.BlockSpec((None, 1, tn), lambda i, j, k: (jnp.minimum(i, n_bp - 1), 0, j)),
            pl.BlockSpec((sample_rows, tn), lambda i, j, k: (0, j)),
        ],
        out_specs=pl.BlockSpec((tm, tn), lambda i, j, k: (i, j)),
        out_shape=jax.ShapeDtypeStruct((m, d), F32),
        compiler_params=_cparams(("arbitrary", "arbitrary", "arbitrary")),
        name="acc_resid",
    )(a, w_full, x, gate_p, gate_s)


HALO = 16


def _pool_prompt_kernel(start, u_ref, wp_ref, sp_ref, ymix_in_ref, y_ref, hist_ref, z_ref):
    del ymix_in_ref
    t = pl.program_id(1)
    nt = pl.num_programs(1)
    tt = u_ref.shape[0]
    group = wp_ref.shape[1]

    @pl.when(t == 0)
    def _():
        z_ref[0:HALO, :] = jnp.zeros((HALO, z_ref.shape[1]), F32)

    z_ref[HALO:HALO + tt, :] = u_ref[...].astype(F32)

    pos = start + t * tt + lax.broadcasted_iota(jnp.int32, (tt, 1), 0)
    for g, w in enumerate(POOL_WINDOWS):
        cols = slice(g * group, (g + 1) * group)
        cur = z_ref[HALO:HALO + tt, cols]
        acc = cur
        for j in range(1, w):
            acc = acc + z_ref[HALO - j:HALO - j + tt, cols]
        cnt = jnp.minimum(w, pos + 1).astype(F32)
        pooled = acc / cnt - cur
        y = _dot(pooled.astype(BF16), wp_ref[g]) * sp_ref[:, cols]
        y_ref[:, cols] = y.astype(y_ref.dtype)

    @pl.when(t == nt - 1)
    def _():
        hist_ref[...] = z_ref[HALO + tt - (HALO - 1):HALO + tt, :]

    z_ref[0:HALO, :] = z_ref[tt:tt + HALO, :]


def _pool_prompt(proj, w_pool_l, s_pool_l, ymix, n_b, seq, tt=256):
    pw = s_pool_l.shape[-1]
    ntt = seq // tt
    n_groups, group, _ = w_pool_l.shape
    y, hist = pl.pallas_call(
        functools.partial(_pool_prompt_kernel, 0),
        grid=(n_b, ntt),
        in_specs=[
            pl.BlockSpec((tt, pw), lambda b, t: (b * ntt + t, 0)),
            pl.BlockSpec((n_groups, group, group), lambda b, t: (0, 0, 0)),
            pl.BlockSpec((1, pw), lambda b, t: (0, 0)),
            pl.BlockSpec(memory_space=pl.ANY),
        ],
        out_specs=[
            pl.BlockSpec((tt, pw), lambda b, t: (b * ntt + t, 0)),
            pl.BlockSpec((None, HALO - 1, pw), lambda b, t: (b, 0, 0)),
        ],
        out_shape=[jax.ShapeDtypeStruct(ymix.shape, ymix.dtype),
                   jax.ShapeDtypeStruct((n_b, HALO - 1, pw), F32)],
        scratch_shapes=[pltpu.VMEM((HALO + tt, pw), F32)],
        input_output_aliases={3: 0},
        compiler_params=_cparams(("arbitrary", "arbitrary")),
        name="pool_prompt",
    )(proj, w_pool_l, s_pool_l.reshape(1, pw), ymix)
    return y, hist


def _pool_sample_kernel(z_ref, wp_ref, sp_ref, y_ref):
    n_hist = HALO - 1
    n_t = z_ref.shape[0] - n_hist
    g = pl.program_id(0)
    for t in range(n_t):
        cur = z_ref[n_hist + t]
        run = cur
        sums = []
        for j in range(1, POOL_WINDOWS[-1]):
            run = run + z_ref[n_hist + t - j]
            if j + 1 in POOL_WINDOWS:
                sums.append(run)
        pooled = jnp.zeros_like(cur)
        for gi, w in enumerate(POOL_WINDOWS):
            pooled = jnp.where(g == gi, sums[gi] / float(w) - cur, pooled)
        y = _dot(pooled.astype(BF16), wp_ref[...]) * sp_ref[...]
        y_ref[t] = y.astype(y_ref.dtype)


def _pool_sample(z_tm, w_pool_l, s_pool_l):
    n_rows, n_b, pw = z_tm.shape
    n_groups, group, _ = w_pool_l.shape
    n_t = n_rows - (HALO - 1)
    return pl.pallas_call(
        _pool_sample_kernel,
        grid=(n_groups,),
        in_specs=[
            pl.BlockSpec((n_rows, n_b, group), lambda g: (0, 0, g)),
            pl.BlockSpec((None, group, group), lambda g: (g, 0, 0)),
            pl.BlockSpec((1, group), lambda g: (0, g)),
        ],
        out_specs=pl.BlockSpec((n_t, n_b, group), lambda g: (0, 0, g)),
        out_shape=jax.ShapeDtypeStruct((n_t, n_b, pw), BF16),
        compiler_params=_cparams(("arbitrary",)),
        name="pool_sample",
    )(z_tm, w_pool_l, s_pool_l.reshape(1, pw))


def _mlstm_segment(q, k, v, ig, lf, c_state, n_state, m_state):
    r = q.shape[0]
    row = lax.broadcasted_iota(jnp.int32, (r, r), 0)
    col = lax.broadcasted_iota(jnp.int32, (r, r), 1)
    causal = col <= row
    lf_rows = jnp.sum(jnp.where(row == col, lf, 0.0), axis=0, keepdims=True)
    ig_rows = jnp.sum(jnp.where(row == col, ig, 0.0), axis=0, keepdims=True)
    b_col = jnp.sum(jnp.where(causal, lf_rows, 0.0), axis=1, keepdims=True)
    b_rows = jnp.sum(jnp.where(row <= col, lf, 0.0), axis=0, keepdims=True)
    dmat = jnp.where(causal, b_col - b_rows + ig_rows, -jnp.inf)
    inter = b_col + m_state
    m_tok = jnp.maximum(inter, jnp.max(dmat, axis=-1, keepdims=True))
    w_intra = jnp.exp(dmat - m_tok)
    w_inter = jnp.exp(inter - m_tok)
    s = _dot_nt(q, k) * w_intra
    num = _dot(s.astype(BF16), v) + w_inter * _dot_nt(q, c_state)
    qn = jnp.sum(q.astype(F32) * n_state, axis=-1, keepdims=True)
    den = jnp.sum(s, axis=-1, keepdims=True) + w_inter * qn
    h = num / jnp.maximum(jnp.abs(den), jnp.exp(-m_tok))
    b_last = jnp.sum(lf, axis=0, keepdims=True)
    dec = b_last - b_col + ig
    m_new = jnp.maximum(b_last + m_state, jnp.max(dec, axis=0, keepdims=True))
    ws = jnp.exp(dec - m_new)
    wc = jnp.exp(b_last + m_state - m_new)
    kf = k.astype(F32)
    c_new = wc * c_state + _dot_tn((ws * v.astype(F32)).astype(BF16), k)
    n_new = wc * n_state + jnp.sum(ws * kf, axis=0, keepdims=True)
    return h, c_new, n_new, m_new


def _head_out(h, o, g_head):
    hn = h * lax.rsqrt(jnp.mean(h * h, axis=-1, keepdims=True) + EPS) * g_head
    return hn * jax.nn.sigmoid(o.astype(F32))


def _log_sigmoid(x):
    return jnp.minimum(x, 0.0) - jnp.log1p(jnp.exp(-jnp.abs(x)))


def _mlstm_prompt_kernel(q_ref, k_ref, v_ref, o_ref, gt_ref, gh_ref, ymix_in_ref,
                         y_ref, c_out_ref, n_out_ref, m_out_ref, c_s, n_s, m_s):
    del ymix_in_ref
    c = pl.program_id(1)
    nc = pl.num_programs(1)
    dk = c_s.shape[2]
    dv = c_s.shape[1]

    @pl.when(c == 0)
    def _():
        c_s[...] = jnp.zeros(c_s.shape, F32)
        n_s[...] = jnp.zeros(n_s.shape, F32)
        m_s[...] = jnp.zeros(m_s.shape, F32)

    for h in range(N_HEADS):
        q = q_ref[:, h * dk:(h + 1) * dk]
        k = (k_ref[:, h * dk:(h + 1) * dk].astype(F32) * (dk ** -0.5)).astype(BF16)
        v = v_ref[:, h * dv:(h + 1) * dv]
        ig = gt_ref[:, h:h + 1]
        lf = _log_sigmoid(gt_ref[:, N_HEADS + h:N_HEADS + h + 1])
        hh, c_new, n_new, m_new = _mlstm_segment(
            q, k, v, ig, lf, c_s[h], n_s[h:h + 1, :], m_s[h:h + 1, 0:1])
        c_s[h] = c_new
        n_s[h:h + 1, :] = n_new
        m_s[h:h + 1, :] = jnp.broadcast_to(m_new, (1, m_s.shape[1]))
        y_ref[:, h * dv:(h + 1) * dv] = _head_out(
            hh, o_ref[:, h * dv:(h + 1) * dv], gh_ref[h:h + 1, :]).astype(y_ref.dtype)

    @pl.when(c == nc - 1)
    def _():
        c_out_ref[...] = c_s[...]
        n_out_ref[...] = n_s[0:N_HEADS, :]
        m_out_ref[...] = m_s[...]


def _mlstm_prompt(proj, gates, g_head_l, ymix, n_b, seq, col_q, dk, dv):
    lc = PROMPT_CHUNK
    nch = seq // lc
    qk_w = N_HEADS * dk
    v_w = N_HEADS * dv
    row = lambda b, c: b * nch + c
    assert col_q % qk_w == 0 and (col_q + 2 * qk_w) % v_w == 0
    bq = col_q // qk_w
    bv = (col_q + 2 * qk_w) // v_w
    y, c_out, n_out, m_out = pl.pallas_call(
        _mlstm_prompt_kernel,
        grid=(n_b, nch),
        in_specs=[
            pl.BlockSpec((lc, qk_w), lambda b, c: (row(b, c), bq)),
            pl.BlockSpec((lc, qk_w), lambda b, c: (row(b, c), bq + 1)),
            pl.BlockSpec((lc, v_w), lambda b, c: (row(b, c), bv)),
            pl.BlockSpec((lc, v_w), lambda b, c: (row(b, c), bv + 1)),
            pl.BlockSpec((lc, gates.shape[1]), lambda b, c: (row(b, c), 0)),
            pl.BlockSpec((N_HEADS, dv), lambda b, c: (0, 0)),
            pl.BlockSpec(memory_space=pl.ANY),
        ],
        out_specs=[
            pl.BlockSpec((lc, v_w), lambda b, c: (row(b, c), 1)),
            pl.BlockSpec((None, N_HEADS, dv, dk), lambda b, c: (b, 0, 0, 0)),
            pl.BlockSpec((None, N_HEADS, dk), lambda b, c: (b, 0, 0)),
            pl.BlockSpec((None, 8, 128), lambda b, c: (b, 0, 0)),
        ],
        out_shape=[jax.ShapeDtypeStruct(ymix.shape, ymix.dtype),
                   jax.ShapeDtypeStruct((n_b, N_HEADS, dv, dk), F32),
                   jax.ShapeDtypeStruct((n_b, N_HEADS, dk), F32),
                   jax.ShapeDtypeStruct((n_b, 8, 128), F32)],
        scratch_shapes=[pltpu.VMEM((N_HEADS, dv, dk), F32),
                        pltpu.VMEM((8, dk), F32),
                        pltpu.VMEM((8, 128), F32)],
        input_output_aliases={6: 0},
        compiler_params=_cparams(("arbitrary", "arbitrary")),
        name="mlstm_prompt",
    )(proj, proj, proj, proj, gates, g_head_l, ymix)
    return y, c_out, n_out, m_out[:, :N_HEADS, 0]


def _mlstm_sample_kernel(seq, q_ref, k_ref, v_ref, o_ref, gt_ref, gh_ref, c0_ref, n0_ref, m0_ref,
                         ymix_in_ref, y_ref, c_out_ref, n_out_ref, m_out_ref):
    del ymix_in_ref
    rows = q_ref.shape[0]
    dk = c0_ref.shape[3]
    dv = c0_ref.shape[2]
    seg_of_row = lax.broadcasted_iota(jnp.int32, (rows, 1), 0) // seq
    for h in range(N_HEADS):
        q = q_ref[:, h * dk:(h + 1) * dk]
        k = (k_ref[:, h * dk:(h + 1) * dk].astype(F32) * (dk ** -0.5)).astype(BF16)
        v = v_ref[:, h * dv:(h + 1) * dv]
        ig_all = gt_ref[:, h:h + 1]
        lf_all = _log_sigmoid(gt_ref[:, N_HEADS + h:N_HEADS + h + 1])
        hh = jnp.zeros((rows, dv), F32)
        for sgm in range(rows // seq):
            mine = seg_of_row == sgm
            ig = jnp.where(mine, ig_all, NEG_BIG)
            lf = jnp.where(mine, lf_all, 0.0)
            h_seg, c_new, n_new, m_new = _mlstm_segment(
                q, k, v, ig, lf, c0_ref[sgm, h], n0_ref[sgm, h:h + 1, :], m0_ref[sgm, h:h + 1, :])
            hh = jnp.where(mine, h_seg, hh)
            c_out_ref[sgm, h] = c_new
            n_out_ref[sgm, h:h + 1, :] = n_new
            m_out_ref[sgm, h:h + 1, :] = m_new
        y_ref[:, h * dv:(h + 1) * dv] = _head_out(
            hh, o_ref[:, h * dv:(h + 1) * dv], gh_ref[h:h + 1, :]).astype(y_ref.dtype)


def _mlstm_sample(proj, gates, g_head_l, c0, n0, m0, ymix, row0, seq, col_q, dk, dv):
    n_b = c0.shape[0]
    pair = SAMPLE_PAIR
    rows = pair * seq
    qk_w = N_HEADS * dk
    v_w = N_HEADS * dv
    bq = col_q // qk_w
    bv = (col_q + 2 * qk_w) // v_w
    r0 = row0 // rows
    y, c_out, n_out, m_out = pl.pallas_call(
        functools.partial(_mlstm_sample_kernel, seq),
        grid=(n_b // pair,),
        in_specs=[
            pl.BlockSpec((rows, qk_w), lambda i: (r0 + i, bq)),
            pl.BlockSpec((rows, qk_w), lambda i: (r0 + i, bq + 1)),
            pl.BlockSpec((rows, v_w), lambda i: (r0 + i, bv)),
            pl.BlockSpec((rows, v_w), lambda i: (r0 + i, bv + 1)),
            pl.BlockSpec((rows, gates.shape[1]), lambda i: (r0 + i, 0)),
            pl.BlockSpec((N_HEADS, dv), lambda i: (0, 0)),
            pl.BlockSpec((pair, N_HEADS, dv, dk), lambda i: (i, 0, 0, 0)),
            pl.BlockSpec((pair, N_HEADS, dk), lambda i: (i, 0, 0)),
            pl.BlockSpec((pair, N_HEADS, 1), lambda i: (i, 0, 0)),
            pl.BlockSpec(memory_space=pl.ANY),
        ],
        out_specs=[
            pl.BlockSpec((rows, v_w), lambda i: (r0 + i, 1)),
            pl.BlockSpec((pair, N_HEADS, dv, dk), lambda i: (i, 0, 0, 0)),
            pl.BlockSpec((pair, N_HEADS, dk), lambda i: (i, 0, 0)),
            pl.BlockSpec((pair, N_HEADS, 1), lambda i: (i, 0, 0)),
        ],
        out_shape=[jax.ShapeDtypeStruct(ymix.shape, ymix.dtype),
                   jax.ShapeDtypeStruct(c0.shape, F32),
                   jax.ShapeDtypeStruct(n0.shape, F32),
                   jax.ShapeDtypeStruct((n_b, N_HEADS, 1), F32)],
        input_output_aliases={9: 0},
        compiler_params=_cparams(("arbitrary",)),
        name="mlstm_sample",
    )(proj, proj, proj, proj, gates, g_head_l, c0, n0, m0.reshape(n_b, N_HEADS, 1), ymix)
    return y, c_out, n_out, m_out[:, :, 0]


def kernel(x_prompt, x_sample, state_pool, state_C, state_n, state_m, c_prompt, c_sample,
           w_ada, b_ada, g_norm, w_in, b_in, w_pool, s_pool, g_head, w_out, w1, w3, w2, g_final):
    n_bp, seq_p, d = x_prompt.shape
    n_bs, seq_s, _ = x_sample.shape
    depth = w_ada.shape[0]
    pw = s_pool.shape[-1]
    dv = g_head.shape[-1]
    dk = state_C.shape[-1]
    n_hist = state_pool.shape[2]
    rows_p = n_bp * seq_p
    rows_s = n_bs * seq_s
    n_main = pw + 2 * N_HEADS * dk + 2 * N_HEADS * dv

    x = jnp.concatenate([x_prompt.reshape(rows_p, d), x_sample.reshape(rows_s, d)], axis=0)

    c_all = jnp.concatenate([c_prompt, c_sample], axis=0)
    pad = (-c_all.shape[0]) % 8
    c_all = jnp.pad(c_all, ((0, pad), (0, 0)))
    mods = _ada(c_all, w_ada, b_ada).reshape(depth, c_all.shape[0], N_SUB * N_MOD, d)
    mods_p = jnp.transpose(mods[:, :n_bp], (0, 2, 1, 3))[:, :, :, None, :]
    mods_s = jnp.repeat(jnp.transpose(mods[:, n_bp:n_bp + n_bs], (0, 2, 1, 3)), seq_s, axis=2)

    b_in3 = b_in.reshape(depth, 1, b_in.shape[-1])
    pool_p, c_p, n_p, m_p = [], [], [], []
    pool_s, c_s, n_s, m_s = [], [], [], []

    def mod(l, sub, which):
        return mods_p[l, sub * N_MOD + which], mods_s[l, sub * N_MOD + which]

    def ffn(x, l, sub_layer, ffn_idx):
        shp, shs = mod(l, sub_layer, 0)
        scp, scs = mod(l, sub_layer, 1)
        gp, gs = mod(l, sub_layer, 2)
        h = _normmod(x, g_norm[l, sub_layer], shp, scp, shs, scs, seq_p)
        act = _swiglu_up(h, w1, w3, l, ffn_idx)
        return _acc_resid(act, w2, (l, ffn_idx), x, gp, gs, FFN_RES)

    for l in range(depth):
        x = ffn(x, l, 0, 0)

        shp, shs = mod(l, 1, 0)
        scp, scs = mod(l, 1, 1)
        gp, gs = mod(l, 1, 2)
        h = _normmod(x, g_norm[l, 1], shp, scp, shs, scs, seq_p)
        proj = _in_proj(h, w_in, b_in3, l, n_main)
        gates = _gate_proj(h, w_in, b_in3, l, n_main)

        ymix = jnp.zeros((rows_p + rows_s, d), BF16)
        ymix, hist_p = _pool_prompt(proj, w_pool[l], s_pool[l], ymix, n_bp, seq_p)
        ymix, cp, np_, mp = _mlstm_prompt(proj, gates, g_head[l], ymix, n_bp, seq_p, pw, dk, dv)
        ymix, cs, ns, ms = _mlstm_sample(proj, gates, g_head[l], state_C[l], state_n[l], state_m[l],
                                         ymix, rows_p, seq_s, pw, dk, dv)
        u_s = proj[rows_p:, :pw].astype(F32).reshape(n_bs, seq_s, pw)
        z_s = jnp.concatenate([state_pool[l], u_s], axis=1)
        y_pool_s = _pool_sample(jnp.transpose(z_s, (1, 0, 2)), w_pool[l], s_pool[l])
        ymix = lax.dynamic_update_slice(
            ymix, jnp.transpose(y_pool_s, (1, 0, 2)).reshape(rows_s, pw), (rows_p, 0))
        x = _acc_resid(ymix, w_out, (l,), x, gp, gs, 1.0)

        pool_p.append(hist_p); c_p.append(cp); n_p.append(np_); m_p.append(mp)
        pool_s.append(z_s[:, -n_hist:]); c_s.append(cs); n_s.append(ns); m_s.append(ms)

        x = ffn(x, l, 2, 1)

    y_prompt = _final_norm(x, g_final, 0, rows_p).reshape(n_bp, seq_p, d)
    y_sample = _final_norm(x, g_final, rows_p, rows_s).reshape(n_bs, seq_s, d)
    return (y_prompt, y_sample,
            jnp.stack(pool_p), jnp.stack(c_p), jnp.stack(n_p), jnp.stack(m_p),
            jnp.stack(pool_s), jnp.stack(c_s), jnp.stack(n_s), jnp.stack(m_s))
```

```python
import functools

import jax
import jax.numpy as jnp
from jax import lax
from jax.experimental import pallas as pl
from jax.experimental.pallas import tpu as pltpu

F32 = jnp.float32
BF16 = jnp.bfloat16

EPS = 1e-6
FFN_RES = 0.5
POOL_WINDOWS = (2, 4, 8, 16)
N_HEADS = 4
N_SUB = 3
N_MOD = 3
PROMPT_CHUNK = 256
SAMPLE_GROUP = 4
NEG_BIG = -1e30

VMEM_LIMIT = 56 * 1024 * 1024

ROW_TILE = 256
MM_ROW_TILE = 1088
ACC_ROW_TILE = 2048
ACC_SUB_ROWS = 512
ACC_K_TILE = 1024
ACC_N_TILE = 1024


def _cparams(sem):
    return pltpu.CompilerParams(dimension_semantics=sem, vmem_limit_bytes=VMEM_LIMIT)


def _dot(a, b):
    return lax.dot_general(a, b, (((1,), (0,)), ((), ())), preferred_element_type=F32)


def _dot_nt(a, b):
    return lax.dot_general(a, b, (((1,), (1,)), ((), ())), preferred_element_type=F32)


def _dot_tn(a, b):
    return lax.dot_general(a, b, (((0,), (0,)), ((), ())), preferred_element_type=F32)


def _ada_kernel(c_ref, w_ref, b_ref, o_ref):
    c = c_ref[...]
    sc = (c * jax.nn.sigmoid(c)).astype(BF16)
    o_ref[...] = _dot(sc, w_ref[...]) + b_ref[...]


def _ada(c_all, w_ada, b_ada, tn=512):
    depth, d, n = w_ada.shape
    rows = c_all.shape[0]
    per = d // tn
    return pl.pallas_call(
        _ada_kernel,
        grid=(depth, n // tn),
        in_specs=[
            pl.BlockSpec((rows, d), lambda l, j: (0, 0)),
            pl.BlockSpec((None, d, tn), lambda l, j: (l, 0, j)),
            pl.BlockSpec((None, 1, tn), lambda l, j: (l, 0, j)),
        ],
        out_specs=pl.BlockSpec((None, None, rows, tn), lambda l, j: (l, j // per, 0, j % per)),
        out_shape=jax.ShapeDtypeStruct((depth, n // d, rows, d), F32),
        compiler_params=_cparams(("arbitrary", "arbitrary")),
        name="ada_mods",
    )(c_all, w_ada, b_ada.reshape(depth, 1, n))


def _rms(x, g):
    return x * lax.rsqrt(jnp.mean(x * x, axis=-1, keepdims=True) + EPS) * g


def _normmod_kernel(n_prompt_tiles, emit_x, xp_ref, xs_ref, g_ref, shp_ref, scp_ref, shs_ref, scs_ref,
                    h_ref, *xo_ref):
    i = pl.program_id(0)
    n_bs = shs_ref.shape[0]

    @pl.when(i < n_prompt_tiles)
    def _():
        x = xp_ref[...]
        h_ref[...] = (_rms(x, g_ref[...]) * (1.0 + scp_ref[...]) + shp_ref[...]).astype(h_ref.dtype)
        if emit_x:
            xo_ref[0][...] = x

    @pl.when(i >= n_prompt_tiles)
    def _():
        for r0 in range(0, xs_ref.shape[0], n_bs):
            rs = slice(r0, r0 + n_bs)
            x = xs_ref[rs, :]
            h_ref[rs, :] = (_rms(x, g_ref[...]) * (1.0 + scs_ref[...]) + shs_ref[...]).astype(h_ref.dtype)
            if emit_x:
                xo_ref[0][rs, :] = x


def _normmod(xp, xs, xs_row0, g, mods, mods_p, layer, sub, seq, emit_x):
    d = xp.shape[1]
    tr = ROW_TILE
    n_bp = mods_p.shape[2]
    n_bs = mods.shape[2] - 8
    rows_p = n_bp * seq
    rows_s = xs.shape[0] - xs_row0
    m = rows_p + rows_s
    n_pt = rows_p // tr
    tiles_per_seq = seq // tr
    s0 = xs_row0 // tr
    i_shift, i_scale = sub * N_MOD, sub * N_MOD + 1
    p_map = lambda which: (lambda i: (layer, which, jnp.minimum(i // tiles_per_seq, n_bp - 1), 0, 0))
    s_map = lambda which: (lambda i: (layer, which, 0, 0))
    out_specs = [pl.BlockSpec((tr, d), lambda i: (i, 0))]
    out_shape = [jax.ShapeDtypeStruct((m, d), BF16)]
    if emit_x:
        out_specs.append(pl.BlockSpec((tr, d), lambda i: (i, 0)))
        out_shape.append(jax.ShapeDtypeStruct((m, d), F32))
    return pl.pallas_call(
        functools.partial(_normmod_kernel, n_pt, emit_x),
        grid=(m // tr,),
        in_specs=[
            pl.BlockSpec((tr, d), lambda i: (jnp.minimum(i, n_pt - 1), 0)),
            pl.BlockSpec((tr, d), lambda i: (s0 + jnp.maximum(i - n_pt, 0), 0)),
            pl.BlockSpec((1, d), lambda i: (0, 0)),
            pl.BlockSpec((None, None, None, 1, d), p_map(i_shift)),
            pl.BlockSpec((None, None, None, 1, d), p_map(i_scale)),
            pl.BlockSpec((None, None, n_bs, d), s_map(i_shift)),
            pl.BlockSpec((None, None, n_bs, d), s_map(i_scale)),
        ],
        out_specs=out_specs,
        out_shape=out_shape,
        compiler_params=_cparams(("arbitrary",)),
        name="normmod",
    )(xp, xs, g.reshape(1, d), mods_p, mods_p, mods, mods)


def _final_norm_kernel(x_ref, g_ref, o_ref):
    o_ref[...] = _rms(x_ref[...], g_ref[...])


def _final_norm(x, g, row0, rows):
    d = x.shape[1]
    tr = ROW_TILE
    b0 = row0 // tr
    return pl.pallas_call(
        _final_norm_kernel,
        grid=(rows // tr,),
        in_specs=[pl.BlockSpec((tr, d), lambda i: (i + b0, 0)),
                  pl.BlockSpec((1, d), lambda i: (0, 0))],
        out_specs=pl.BlockSpec((tr, d), lambda i: (i, 0)),
        out_shape=jax.ShapeDtypeStruct((rows, d), F32),
        compiler_params=_cparams(("arbitrary",)),
        name="final_norm",
    )(x, g.reshape(1, d))


def _swiglu_up_kernel(x_ref, w1_ref, w3_ref, o_ref):
    x = x_ref[...]
    a = _dot(x, w1_ref[...])
    b = _dot(x, w3_ref[...])
    o_ref[...] = (a * jax.nn.sigmoid(a) * b).astype(o_ref.dtype)


def _swiglu_up(h, w1, w3, layer, sub, tn=256):
    m, d = h.shape
    f = w1.shape[-1]
    tm = MM_ROW_TILE
    w_spec = pl.BlockSpec((None, None, d, tn), lambda j, i: (layer, sub, 0, j))
    return pl.pallas_call(
        _swiglu_up_kernel,
        grid=(f // tn, m // tm),
        in_specs=[pl.BlockSpec((tm, d), lambda j, i: (i, 0)), w_spec, w_spec],
        out_specs=pl.BlockSpec((tm, tn), lambda j, i: (i, j)),
        out_shape=jax.ShapeDtypeStruct((m, f), BF16),
        compiler_params=_cparams(("arbitrary", "arbitrary")),
        name="swiglu_up",
    )(h, w1, w3)


def _proj_kernel(x_ref, w_ref, b_ref, o_ref):
    o_ref[...] = (_dot(x_ref[...], w_ref[...]) + b_ref[...]).astype(o_ref.dtype)


def _in_proj(h, w_in, b_in3, layer, n_main, tn=512):
    m, d = h.shape
    tm = MM_ROW_TILE
    return pl.pallas_call(
        _proj_kernel,
        grid=(n_main // tn, m // tm),
        in_specs=[pl.BlockSpec((tm, d), lambda j, i: (i, 0)),
                  pl.BlockSpec((None, d, tn), lambda j, i: (layer, 0, j)),
                  pl.BlockSpec((None, 1, tn), lambda j, i: (layer, 0, j))],
        out_specs=pl.BlockSpec((tm, tn), lambda j, i: (i, j)),
        out_shape=jax.ShapeDtypeStruct((m, n_main), BF16),
        compiler_params=_cparams(("arbitrary", "arbitrary")),
        name="in_proj",
    )(h, w_in, b_in3)


def _gate_proj(h, w_in, b_in3, layer, n_main, tn=128):
    m, d = h.shape
    tm = MM_ROW_TILE
    jb = n_main // tn
    return pl.pallas_call(
        _proj_kernel,
        grid=(m // tm,),
        in_specs=[pl.BlockSpec((tm, d), lambda i: (i, 0)),
                  pl.BlockSpec((None, d, tn), lambda i: (layer, 0, jb)),
                  pl.BlockSpec((None, 1, tn), lambda i: (layer, 0, jb))],
        out_specs=pl.BlockSpec((tm, tn), lambda i: (i, 0)),
        out_shape=jax.ShapeDtypeStruct((m, tn), F32),
        compiler_params=_cparams(("arbitrary",)),
        name="gate_proj",
    )(h, w_in, b_in3)


def _acc_resid_kernel(n_prompt_tiles, sample_rows, k_last_valid, res_scale,
                      a_ref, w_ref, x_ref, gp_ref, gs_ref, o_ref):
    i = pl.program_id(0)
    k = pl.program_id(2)
    nk = pl.num_programs(2)
    tk = a_ref.shape[1]
    n_bs = gs_ref.shape[0]

    def sweep(rows, kind, gate_of):
        kv = k_last_valid if kind == "last" else tk
        for r0 in range(0, rows, ACC_SUB_ROWS):
            rs = slice(r0, r0 + ACC_SUB_ROWS)
            part = _dot(a_ref[rs, 0:kv], w_ref[0:kv, :])
            if kind == "first":
                o_ref[rs, :] = part
            elif kind == "mid":
                o_ref[rs, :] += part
            else:
                for q0 in range(r0, r0 + ACC_SUB_ROWS, n_bs):
                    qs = slice(q0, q0 + n_bs)
                    ps = slice(q0 - r0, q0 - r0 + n_bs)
                    o_ref[qs, :] = x_ref[qs, :] + (res_scale * gate_of()) * (o_ref[qs, :] + part[ps, :])

    def steps(rows, gate_of):
        @pl.when(k == 0)
        def _():
            sweep(rows, "first", gate_of)

        @pl.when(jnp.logical_and(k > 0, k < nk - 1))
        def _():
            sweep(rows, "mid", gate_of)

        @pl.when(k == nk - 1)
        def _():
            sweep(rows, "last", gate_of)

    @pl.when(i < n_prompt_tiles)
    def _():
        steps(a_ref.shape[0], lambda: gp_ref[...])

    @pl.when(i >= n_prompt_tiles)
    def _():
        steps(sample_rows, lambda: gs_ref[...])


def _acc_resid(a, w_full, w_index, x, mods, mods_p, layer, which, sample_rows, res_scale):
    m, kdim = a.shape
    d = x.shape[1]
    tm, tk, tn = ACC_ROW_TILE, ACC_K_TILE, ACC_N_TILE
    n_bp = mods_p.shape[2]
    n_bs = mods.shape[2] - 8
    assert m == n_bp * tm + sample_rows and tm % ACC_SUB_ROWS == 0
    assert sample_rows % ACC_SUB_ROWS == 0 and ACC_SUB_ROWS % n_bs == 0
    nk = pl.cdiv(kdim, tk)
    assert nk >= 3
    k_last_valid = kdim - (nk - 1) * tk
    lead = (None,) * len(w_index)
    return pl.pallas_call(
        functools.partial(_acc_resid_kernel, n_bp, sample_rows, k_last_valid, res_scale),
        grid=(n_bp + 1, d // tn, nk),
        in_specs=[
            pl.BlockSpec((tm, tk), lambda i, j, k: (i, k)),
            pl.BlockSpec(lead + (tk, tn), lambda i, j, k: tuple(w_index) + (k, j)),
            pl.BlockSpec((tm, tn), lambda i, j, k: (i, j)),
            pl.BlockSpec((None, None, None, 1, tn),
                         lambda i, j, k: (layer, which, jnp.minimum(i, n_bp - 1), 0, j)),
            pl.BlockSpec((None, None, n_bs, tn), lambda i, j, k: (layer, which, 0, j)),
        ],
        out_specs=pl.BlockSpec((tm, tn), lambda i, j, k: (i, j)),
        out_shape=jax.ShapeDtypeStruct((m, d), F32),
        compiler_params=_cparams(("arbitrary", "arbitrary", "arbitrary")),
        name="acc_resid",
    )(a, w_full, x, mods_p, mods)


HALO = 16


def _pool_prompt_kernel(start, u_ref, wp_ref, sp_ref, y_ref, hist_ref, z_ref):
    t = pl.program_id(1)
    nt = pl.num_programs(1)
    tt = u_ref.shape[0]
    group = wp_ref.shape[1]

    @pl.when(t == 0)
    def _():
        z_ref[0:HALO, :] = jnp.zeros((HALO, z_ref.shape[1]), F32)

    z_ref[HALO:HALO + tt, :] = u_ref[...].astype(F32)

    pos = start + t * tt + lax.broadcasted_iota(jnp.int32, (tt, 1), 0)
    for g, w in enumerate(POOL_WINDOWS):
        cols = slice(g * group, (g + 1) * group)
        cur = z_ref[HALO:HALO + tt, cols]
        acc = cur
        for j in range(1, w):
            acc = acc + z_ref[HALO - j:HALO - j + tt, cols]
        cnt = jnp.minimum(w, pos + 1).astype(F32)
        pooled = acc / cnt - cur
        y = _dot(pooled.astype(BF16), wp_ref[g]) * sp_ref[:, cols]
        y_ref[:, cols] = y.astype(y_ref.dtype)

    @pl.when(t == nt - 1)
    def _():
        hist_ref[...] = z_ref[HALO + tt - (HALO - 1):HALO + tt, :]

    z_ref[0:HALO, :] = z_ref[tt:tt + HALO, :]


def _pool_prompt(proj, w_pool_l, s_pool_l, n_b, seq, d, tt=256):
    pw = s_pool_l.shape[-1]
    ntt = seq // tt
    n_groups, group, _ = w_pool_l.shape
    return pl.pallas_call(
        functools.partial(_pool_prompt_kernel, 0),
        grid=(n_b, ntt),
        in_specs=[
            pl.BlockSpec((tt, pw), lambda b, t: (b * ntt + t, 0)),
            pl.BlockSpec((n_groups, group, group), lambda b, t: (0, 0, 0)),
            pl.BlockSpec((1, pw), lambda b, t: (0, 0)),
        ],
        out_specs=[
            pl.BlockSpec((tt, pw), lambda b, t: (b * ntt + t, 0)),
            pl.BlockSpec((None, HALO - 1, pw), lambda b, t: (b, 0, 0)),
        ],
        out_shape=[jax.ShapeDtypeStruct((proj.shape[0], d), BF16),
                   jax.ShapeDtypeStruct((n_b, HALO - 1, pw), F32)],
        scratch_shapes=[pltpu.VMEM((HALO + tt, pw), F32)],
        compiler_params=_cparams(("arbitrary", "arbitrary")),
        name="pool_prompt",
    )(proj, w_pool_l, s_pool_l.reshape(1, pw))


def _pool_sample_kernel(hist_ref, u_ref, wp_ref, sp_ref, ymix_in_ref, y_ref):
    del ymix_in_ref
    n_hist, n_b, _ = hist_ref.shape
    n_t = u_ref.shape[0] // n_b
    g = pl.program_id(0)

    def z(r):
        if r < n_hist:
            return hist_ref[r]
        return u_ref[(r - n_hist) * n_b:(r - n_hist + 1) * n_b, :].astype(F32)

    for t in range(n_t):
        cur = z(n_hist + t)
        run = cur
        sums = []
        for j in range(1, POOL_WINDOWS[-1]):
            run = run + z(n_hist + t - j)
            if j + 1 in POOL_WINDOWS:
                sums.append(run)
        pooled = jnp.zeros_like(cur)
        for gi, w in enumerate(POOL_WINDOWS):
            pooled = jnp.where(g == gi, sums[gi] / float(w) - cur, pooled)
        y = _dot(pooled.astype(BF16), wp_ref[...]) * sp_ref[...]
        y_ref[t * n_b:(t + 1) * n_b, :] = y.astype(y_ref.dtype)


def _pool_sample(hist_tm, proj, w_pool_l, s_pool_l, ymix, row0):
    n_hist, n_b, pw = hist_tm.shape
    n_groups, group, _ = w_pool_l.shape
    rows = proj.shape[0] - row0
    rb = row0 // rows
    return pl.pallas_call(
        _pool_sample_kernel,
        grid=(n_groups,),
        in_specs=[
            pl.BlockSpec((n_hist, n_b, group), lambda g: (0, 0, g)),
            pl.BlockSpec((rows, group), lambda g: (rb, g)),
            pl.BlockSpec((None, group, group), lambda g: (g, 0, 0)),
            pl.BlockSpec((1, group), lambda g: (0, g)),
            pl.BlockSpec(memory_space=pl.ANY),
        ],
        out_specs=pl.BlockSpec((rows, group), lambda g: (rb, g)),
        out_shape=jax.ShapeDtypeStruct(ymix.shape, ymix.dtype),
        input_output_aliases={4: 0},
        compiler_params=_cparams(("arbitrary",)),
        name="pool_sample",
    )(hist_tm, proj, w_pool_l, s_pool_l.reshape(1, pw), ymix)


def _mlstm_segment(q, k, v, ig, lf, c_state, n_state, m_state):
    r = q.shape[0]
    row = lax.broadcasted_iota(jnp.int32, (r, r), 0)
    col = lax.broadcasted_iota(jnp.int32, (r, r), 1)
    causal = col <= row
    lf_rows = jnp.sum(jnp.where(row == col, lf, 0.0), axis=0, keepdims=True)
    ig_rows = jnp.sum(jnp.where(row == col, ig, 0.0), axis=0, keepdims=True)
    b_col = jnp.sum(jnp.where(causal, lf_rows, 0.0), axis=1, keepdims=True)
    b_rows = jnp.sum(jnp.where(row <= col, lf, 0.0), axis=0, keepdims=True)
    dmat = jnp.where(causal, b_col - b_rows + ig_rows, -jnp.inf)
    inter = b_col + m_state
    m_tok = jnp.maximum(inter, jnp.max(dmat, axis=-1, keepdims=True))
    w_intra = jnp.exp(dmat - m_tok)
    w_inter = jnp.exp(inter - m_tok)
    s = _dot_nt(q, k) * w_intra
    num = _dot(s.astype(BF16), v) + w_inter * _dot_nt(q, c_state)
    qn = jnp.sum(q.astype(F32) * n_state, axis=-1, keepdims=True)
    den = jnp.sum(s, axis=-1, keepdims=True) + w_inter * qn
    h = num / jnp.maximum(jnp.abs(den), jnp.exp(-m_tok))
    b_last = jnp.sum(lf, axis=0, keepdims=True)
    dec = b_last - b_col + ig
    m_new = jnp.maximum(b_last + m_state, jnp.max(dec, axis=0, keepdims=True))
    ws = jnp.exp(dec - m_new)
    wc = jnp.exp(b_last + m_state - m_new)
    kf = k.astype(F32)
    c_new = wc * c_state + _dot_tn((ws * v.astype(F32)).astype(BF16), k)
    n_new = wc * n_state + jnp.sum(ws * kf, axis=0, keepdims=True)
    return h, c_new, n_new, m_new


def _head_out(h, o, g_head):
    hn = h * lax.rsqrt(jnp.mean(h * h, axis=-1, keepdims=True) + EPS) * g_head
    return hn * jax.nn.sigmoid(o.astype(F32))


def _log_sigmoid(x):
    return jnp.minimum(x, 0.0) - jnp.log1p(jnp.exp(-jnp.abs(x)))


def _mlstm_prompt_kernel(q_ref, k_ref, v_ref, o_ref, gt_ref, gh_ref, ymix_in_ref,
                         y_ref, c_out_ref, n_out_ref, m_out_ref, c_s, n_s, m_s):
    del ymix_in_ref
    c = pl.program_id(1)
    nc = pl.num_programs(1)
    dk = c_s.shape[2]
    dv = c_s.shape[1]

    @pl.when(c == 0)
    def _():
        c_s[...] = jnp.zeros(c_s.shape, F32)
        n_s[...] = jnp.zeros(n_s.shape, F32)
        m_s[...] = jnp.zeros(m_s.shape, F32)

    for h in range(N_HEADS):
        q = q_ref[:, h * dk:(h + 1) * dk]
        k = (k_ref[:, h * dk:(h + 1) * dk].astype(F32) * (dk ** -0.5)).astype(BF16)
        v = v_ref[:, h * dv:(h + 1) * dv]
        ig = gt_ref[:, h:h + 1]
        lf = _log_sigmoid(gt_ref[:, N_HEADS + h:N_HEADS + h + 1])
        hh, c_new, n_new, m_new = _mlstm_segment(
            q, k, v, ig, lf, c_s[h], n_s[h:h + 1, :], m_s[h:h + 1, 0:1])
        c_s[h] = c_new
        n_s[h:h + 1, :] = n_new
        m_s[h:h + 1, :] = jnp.broadcast_to(m_new, (1, m_s.shape[1]))
        y_ref[:, h * dv:(h + 1) * dv] = _head_out(
            hh, o_ref[:, h * dv:(h + 1) * dv], gh_ref[h:h + 1, :]).astype(y_ref.dtype)

    @pl.when(c == nc - 1)
    def _():
        c_out_ref[...] = c_s[...]
        n_out_ref[...] = n_s[0:N_HEADS, :]
        m_out_ref[...] = m_s[...]


def _mlstm_prompt(proj, gates, g_head_l, ymix, n_b, seq, col_q, dk, dv):
    lc = PROMPT_CHUNK
    nch = seq // lc
    qk_w = N_HEADS * dk
    v_w = N_HEADS * dv
    row = lambda b, c: b * nch + c
    assert col_q % qk_w == 0 and (col_q + 2 * qk_w) % v_w == 0
    bq = col_q // qk_w
    bv = (col_q + 2 * qk_w) // v_w
    y, c_out, n_out, m_out = pl.pallas_call(
        _mlstm_prompt_kernel,
        grid=(n_b, nch),
        in_specs=[
            pl.BlockSpec((lc, qk_w), lambda b, c: (row(b, c), bq)),
            pl.BlockSpec((lc, qk_w), lambda b, c: (row(b, c), bq + 1)),
            pl.BlockSpec((lc, v_w), lambda b, c: (row(b, c), bv)),
            pl.BlockSpec((lc, v_w), lambda b, c: (row(b, c), bv + 1)),
            pl.BlockSpec((lc, gates.shape[1]), lambda b, c: (row(b, c), 0)),
            pl.BlockSpec((N_HEADS, dv), lambda b, c: (0, 0)),
            pl.BlockSpec(memory_space=pl.ANY),
        ],
        out_specs=[
            pl.BlockSpec((lc, v_w), lambda b, c: (row(b, c), 1)),
            pl.BlockSpec((None, N_HEADS, dv, dk), lambda b, c: (b, 0, 0, 0)),
            pl.BlockSpec((None, N_HEADS, dk), lambda b, c: (b, 0, 0)),
            pl.BlockSpec((None, 8, 128), lambda b, c: (b, 0, 0)),
        ],
        out_shape=[jax.ShapeDtypeStruct(ymix.shape, ymix.dtype),
                   jax.ShapeDtypeStruct((n_b, N_HEADS, dv, dk), F32),
                   jax.ShapeDtypeStruct((n_b, N_HEADS, dk), F32),
                   jax.ShapeDtypeStruct((n_b, 8, 128), F32)],
        scratch_shapes=[pltpu.VMEM((N_HEADS, dv, dk), F32),
                        pltpu.VMEM((8, dk), F32),
                        pltpu.VMEM((8, 128), F32)],
        input_output_aliases={6: 0},
        compiler_params=_cparams(("arbitrary", "arbitrary")),
        name="mlstm_prompt",
    )(proj, proj, proj, proj, gates, g_head_l, ymix)
    return y, c_out, n_out, m_out[:, :N_HEADS, 0]


def _mlstm_sample_kernel(seq, has_prev, q_ref, k_ref, v_ref, o_ref, gt_ref, gh_ref, c0_ref, n0_ref, m0_ref,
                         *rest):
    y_ref, c_out_ref, n_out_ref, m_out_ref = rest[-4:]
    rows = q_ref.shape[0]
    dk = c0_ref.shape[3]
    dv = c0_ref.shape[2]
    seg_of_row = lax.broadcasted_iota(jnp.int32, (rows, 1), 0) // seq
    for h in range(N_HEADS):
        q = q_ref[:, h * dk:(h + 1) * dk]
        k = (k_ref[:, h * dk:(h + 1) * dk].astype(F32) * (dk ** -0.5)).astype(BF16)
        v = v_ref[:, h * dv:(h + 1) * dv]
        ig_all = gt_ref[:, h:h + 1]
        lf_all = _log_sigmoid(gt_ref[:, N_HEADS + h:N_HEADS + h + 1])
        hh = jnp.zeros((rows, dv), F32)
        for sgm in range(rows // seq):
            mine = seg_of_row == sgm
            ig = jnp.where(mine, ig_all, NEG_BIG)
            lf = jnp.where(mine, lf_all, 0.0)
            h_seg, c_new, n_new, m_new = _mlstm_segment(
                q, k, v, ig, lf, c0_ref[sgm, h], n0_ref[sgm, h:h + 1, :], m0_ref[sgm, h:h + 1, :])
            hh = jnp.where(mine, h_seg, hh)
            c_out_ref[sgm, h] = c_new
            n_out_ref[sgm, h:h + 1, :] = n_new
            m_out_ref[sgm, h:h + 1, :] = m_new
        y_ref[:, h * dv:(h + 1) * dv] = _head_out(
            hh, o_ref[:, h * dv:(h + 1) * dv], gh_ref[h:h + 1, :]).astype(y_ref.dtype)


def _mlstm_sample(proj_bm, gates_bm, g_head_l, state_c, state_n, state_m, layer, c_prev, seq, col_q, dk, dv):
    depth, n_b = state_c.shape[:2]
    grp = SAMPLE_GROUP
    rows = grp * seq
    qk_w = N_HEADS * dk
    v_w = N_HEADS * dv
    bq = col_q // qk_w
    bv = (col_q + 2 * qk_w) // v_w
    has_prev = c_prev is not None
    in_specs = [
        pl.BlockSpec((rows, qk_w), lambda i: (i, bq)),
        pl.BlockSpec((rows, qk_w), lambda i: (i, bq + 1)),
        pl.BlockSpec((rows, v_w), lambda i: (i, bv)),
        pl.BlockSpec((rows, v_w), lambda i: (i, bv + 1)),
        pl.BlockSpec((rows, gates_bm.shape[1]), lambda i: (i, 0)),
        pl.BlockSpec((N_HEADS, dv), lambda i: (0, 0)),
        pl.BlockSpec((None, grp, N_HEADS, dv, dk), lambda i: (layer, i, 0, 0, 0)),
        pl.BlockSpec((None, grp, N_HEADS, dk), lambda i: (layer, i, 0, 0)),
        pl.BlockSpec((None, grp, N_HEADS, 1), lambda i: (layer, i, 0, 0)),
    ]
    args = [proj_bm, proj_bm, proj_bm, proj_bm, gates_bm, g_head_l, state_c, state_n,
            state_m.reshape(depth, n_b, N_HEADS, 1)]
    aliases = {}
    if has_prev:
        in_specs.append(pl.BlockSpec(memory_space=pl.ANY))
        args.append(c_prev)
        aliases = {len(args) - 1: 1}
    y, c_out, n_out, m_out = pl.pallas_call(
        functools.partial(_mlstm_sample_kernel, seq, has_prev),
        grid=(n_b // grp,),
        in_specs=in_specs,
        out_specs=[
            pl.BlockSpec((rows, v_w), lambda i: (i, 0)),
            pl.BlockSpec((None, grp, N_HEADS, dv, dk), lambda i: (layer, i, 0, 0, 0)),
            pl.BlockSpec((grp, N_HEADS, dk), lambda i: (i, 0, 0)),
            pl.BlockSpec((grp, N_HEADS, 1), lambda i: (i, 0, 0)),
        ],
        out_shape=[jax.ShapeDtypeStruct((n_b * seq, v_w), BF16),
                   jax.ShapeDtypeStruct(state_c.shape, F32),
                   jax.ShapeDtypeStruct((n_b, N_HEADS, dk), F32),
                   jax.ShapeDtypeStruct((n_b, N_HEADS, 1), F32)],
        input_output_aliases=aliases,
        compiler_params=_cparams(("arbitrary",)),
        name="mlstm_sample",
    )(*args)
    return y, c_out, n_out, m_out[:, :, 0]


def _to_time_major(a, n_b, seq):
    return jnp.transpose(a.reshape(n_b, seq, -1), (1, 0, 2)).reshape(n_b * seq, -1)


def _to_batch_major(a, n_b, seq):
    return jnp.transpose(a.reshape(seq, n_b, -1), (1, 0, 2)).reshape(n_b * seq, -1)


def kernel(x_prompt, x_sample, state_pool, state_C, state_n, state_m, c_prompt, c_sample,
           w_ada, b_ada, g_norm, w_in, b_in, w_pool, s_pool, g_head, w_out, w1, w3, w2, g_final):
    n_bp, seq_p, d = x_prompt.shape
    n_bs, seq_s, _ = x_sample.shape
    depth = w_ada.shape[0]
    pw = s_pool.shape[-1]
    dv = g_head.shape[-1]
    dk = state_C.shape[-1]
    n_hist = state_pool.shape[2]
    rows_p = n_bp * seq_p
    rows_s = n_bs * seq_s
    n_main = pw + 2 * N_HEADS * dk + 2 * N_HEADS * dv

    xp0 = x_prompt.reshape(rows_p, d)
    xs0 = _to_time_major(x_sample, n_bs, seq_s)

    c_all = jnp.concatenate([c_sample, c_prompt, jnp.zeros((8 - n_bp, d), F32)], axis=0)
    mods = _ada(c_all, w_ada, b_ada)
    mods_p = mods[:, :, n_bs:n_bs + n_bp][:, :, :, None, :]

    b_in3 = b_in.reshape(depth, 1, b_in.shape[-1])
    pool_p, c_p, n_p, m_p = [], [], [], []
    pool_s, n_s, m_s = [], [], []
    c_s_all = None

    def ffn(x, h, l, sub_layer, ffn_idx):
        act = _swiglu_up(h, w1, w3, l, ffn_idx)
        return _acc_resid(act, w2, (l, ffn_idx), x, mods, mods_p, l, sub_layer * N_MOD + 2,
                          rows_s, FFN_RES)

    x = None
    for l in range(depth):
        if l == 0:
            h, x = _normmod(xp0, xs0, 0, g_norm[l, 0], mods, mods_p, l, 0, seq_p, True)
        else:
            h, = _normmod(x, x, rows_p, g_norm[l, 0], mods, mods_p, l, 0, seq_p, False)
        x = ffn(x, h, l, 0, 0)

        h, = _normmod(x, x, rows_p, g_norm[l, 1], mods, mods_p, l, 1, seq_p, False)
        proj = _in_proj(h, w_in, b_in3, l, n_main)
        gates = _gate_proj(h, w_in, b_in3, l, n_main)

        ymix, hist_p = _pool_prompt(proj, w_pool[l], s_pool[l], n_bp, seq_p, d)
        ymix, cp, np_, mp = _mlstm_prompt(proj, gates, g_head[l], ymix, n_bp, seq_p, pw, dk, dv)
        hist_tm = jnp.transpose(state_pool[l], (1, 0, 2))
        ymix = _pool_sample(hist_tm, proj, w_pool[l], s_pool[l], ymix, rows_p)
        proj_bm = _to_batch_major(proj[rows_p:], n_bs, seq_s)
        gates_bm = _to_batch_major(gates[rows_p:], n_bs, seq_s)
        y_ms, c_s_all, ns, ms = _mlstm_sample(proj_bm, gates_bm, g_head[l], state_C, state_n, state_m,
                                              l, c_s_all, seq_s, pw, dk, dv)
        ymix = lax.dynamic_update_slice(ymix, _to_time_major(y_ms, n_bs, seq_s), (rows_p, pw))
        x = _acc_resid(ymix, w_out, (l,), x, mods, mods_p, l, 1 * N_MOD + 2, rows_s, 1.0)

        u_s = proj_bm[:, :pw].astype(F32).reshape(n_bs, seq_s, pw)
        pool_p.append(hist_p); c_p.append(cp); n_p.append(np_); m_p.append(mp)
        pool_s.append(jnp.concatenate([state_pool[l, :, seq_s:], u_s], axis=1))
        n_s.append(ns); m_s.append(ms)

        h, = _normmod(x, x, rows_p, g_norm[l, 2], mods, mods_p, l, 2, seq_p, False)
        x = ffn(x, h, l, 2, 1)

    y_prompt = _final_norm(x, g_final, 0, rows_p).reshape(n_bp, seq_p, d)
    y_sample = _to_batch_major(_final_norm(x, g_final, rows_p, rows_s), n_bs, seq_s).reshape(n_bs, seq_s, d)
    return (y_prompt, y_sample,
            jnp.stack(pool_p), jnp.stack(c_p), jnp.stack(n_p), jnp.stack(m_p),
            jnp.stack(pool_s), c_s_all, jnp.stack(n_s), jnp.stack(m_s))
```

```python
import functools

import jax
import jax.numpy as jnp
from jax import lax
from jax.experimental import pallas as pl
from jax.experimental.pallas import tpu as pltpu

F32 = jnp.float32
BF16 = jnp.bfloat16

EPS = 1e-6
FFN_RES = 0.5
POOL_WINDOWS = (2, 4, 8, 16)
N_HEADS = 4
N_SUB = 3
N_MOD = 3
PROMPT_CHUNK = 256
SAMPLE_GROUP = 4
NEG_BIG = -1e30

VMEM_LIMIT = 56 * 1024 * 1024

ROW_TILE = 256
MM_ROW_TILE = 1088
ACC_ROW_TILE = 2048
ACC_SUB_ROWS = 512
ACC_K_TILE = 1024
ACC_N_TILE = 1024


def _cparams(sem):
    return pltpu.CompilerParams(dimension_semantics=sem, vmem_limit_bytes=VMEM_LIMIT)


def _dot(a, b):
    return lax.dot_general(a, b, (((1,), (0,)), ((), ())), preferred_element_type=F32)


def _dot_nt(a, b):
    return lax.dot_general(a, b, (((1,), (1,)), ((), ())), preferred_element_type=F32)


def _dot_tn(a, b):
    return lax.dot_general(a, b, (((0,), (0,)), ((), ())), preferred_element_type=F32)


def _ada_kernel(c_ref, w_ref, b_ref, o_ref):
    c = c_ref[...]
    sc = (c * jax.nn.sigmoid(c)).astype(BF16)
    o_ref[...] = _dot(sc, w_ref[...]) + b_ref[...]


def _ada(c_all, w_ada, b_ada, tn=1024):
    depth, d, n = w_ada.shape
    rows = c_all.shape[0]
    per = d // tn
    return pl.pallas_call(
        _ada_kernel,
        grid=(depth, n // tn),
        in_specs=[
            pl.BlockSpec((rows, d), lambda l, j: (0, 0)),
            pl.BlockSpec((None, d, tn), lambda l, j: (l, 0, j)),
            pl.BlockSpec((None, 1, tn), lambda l, j: (l, 0, j)),
        ],
        out_specs=pl.BlockSpec((None, None, rows, tn), lambda l, j: (l, j // per, 0, j % per)),
        out_shape=jax.ShapeDtypeStruct((depth, n // d, rows, d), F32),
        compiler_params=_cparams(("arbitrary", "arbitrary")),
        name="ada_mods",
    )(c_all, w_ada, b_ada.reshape(depth, 1, n))


def _rms(x, g):
    return x * lax.rsqrt(jnp.mean(x * x, axis=-1, keepdims=True) + EPS) * g


NORM_CHUNK = 8


def _normmod_kernel(n_prompt_tiles, emit_x, xp_ref, xs_ref, g_ref, shp_ref, scp_ref, shs_ref, scs_ref,
                    h_ref, *rest):
    xo_ref = rest[0] if emit_x else None
    gm_ref = rest[-1]
    i = pl.program_id(0)
    n_bs = shs_ref.shape[0]
    n_chunks = h_ref.shape[0] // NORM_CHUNK

    def run(x_ref, mod_rows):
        def chunk(c, carry):
            r = pl.multiple_of(c * NORM_CHUNK, NORM_CHUNK)
            rows = pl.ds(r, NORM_CHUNK)
            x = x_ref[rows, :]
            rstd = lax.rsqrt(jnp.mean(x * x, axis=-1, keepdims=True) + EPS)
            gm, sh = mod_rows(r)
            h_ref[rows, :] = (x * rstd * gm + sh).astype(h_ref.dtype)
            if emit_x:
                xo_ref[rows, :] = x
            return carry

        lax.fori_loop(0, n_chunks, chunk, 0, unroll=4)

    @pl.when(i < n_prompt_tiles)
    def _():
        gm_ref[0:1, :] = g_ref[...] * (1.0 + scp_ref[...])
        run(xp_ref, lambda r: (gm_ref[0:1, :], shp_ref[...]))

    @pl.when(i >= n_prompt_tiles)
    def _():
        gm_ref[...] = g_ref[...] * (1.0 + scs_ref[...])

        def mod_rows(r):
            rb = pl.ds(pl.multiple_of(lax.rem(r, n_bs), NORM_CHUNK), NORM_CHUNK)
            return gm_ref[rb, :], shs_ref[rb, :]

        run(xs_ref, mod_rows)


def _normmod(xp, xs, xs_row0, g, mods, mods_p, layer, sub, seq, emit_x):
    d = xp.shape[1]
    tr = ROW_TILE if emit_x else 2 * ROW_TILE
    n_bp = mods_p.shape[2]
    n_bs = mods.shape[2] - 8
    rows_p = n_bp * seq
    rows_s = xs.shape[0] - xs_row0
    m = rows_p + rows_s
    n_pt = rows_p // tr
    tiles_per_seq = seq // tr
    s0 = xs_row0 // tr
    i_shift, i_scale = sub * N_MOD, sub * N_MOD + 1
    p_map = lambda which: (lambda i: (layer, which, jnp.minimum(i // tiles_per_seq, n_bp - 1), 0, 0))
    s_map = lambda which: (lambda i: (layer, which, 0, 0))
    out_specs = [pl.BlockSpec((tr, d), lambda i: (i, 0))]
    out_shape = [jax.ShapeDtypeStruct((m, d), BF16)]
    if emit_x:
        out_specs.append(pl.BlockSpec((tr, d), lambda i: (i, 0)))
        out_shape.append(jax.ShapeDtypeStruct((m, d), F32))
    return pl.pallas_call(
        functools.partial(_normmod_kernel, n_pt, emit_x),
        grid=(m // tr,),
        in_specs=[
            pl.BlockSpec((tr, d), lambda i: (jnp.minimum(i, n_pt - 1), 0)),
            pl.BlockSpec((tr, d), lambda i: (s0 + jnp.maximum(i - n_pt, 0), 0)),
            pl.BlockSpec((1, d), lambda i: (0, 0)),
            pl.BlockSpec((None, None, None, 1, d), p_map(i_shift)),
            pl.BlockSpec((None, None, None, 1, d), p_map(i_scale)),
            pl.BlockSpec((None, None, n_bs, d), s_map(i_shift)),
            pl.BlockSpec((None, None, n_bs, d), s_map(i_scale)),
        ],
        out_specs=out_specs,
        out_shape=out_shape,
        scratch_shapes=[pltpu.VMEM((n_bs, d), F32)],
        compiler_params=_cparams(("arbitrary",)),
        name="normmod",
    )(xp, xs, g.reshape(1, d), mods_p, mods_p, mods, mods)


def _final_norm_kernel(x_ref, g_ref, o_ref):
    def chunk(c, carry):
        rows = pl.ds(pl.multiple_of(c * NORM_CHUNK, NORM_CHUNK), NORM_CHUNK)
        o_ref[rows, :] = _rms(x_ref[rows, :], g_ref[...])
        return carry

    lax.fori_loop(0, o_ref.shape[0] // NORM_CHUNK, chunk, 0, unroll=4)


def _final_norm(x, g, row0, rows):
    d = x.shape[1]
    tr = ROW_TILE
    b0 = row0 // tr
    return pl.pallas_call(
        _final_norm_kernel,
        grid=(rows // tr,),
        in_specs=[pl.BlockSpec((tr, d), lambda i: (i + b0, 0)),
                  pl.BlockSpec((1, d), lambda i: (0, 0))],
        out_specs=pl.BlockSpec((tr, d), lambda i: (i, 0)),
        out_shape=jax.ShapeDtypeStruct((rows, d), F32),
        compiler_params=_cparams(("arbitrary",)),
        name="final_norm",
    )(x, g.reshape(1, d))


def _swiglu_up_kernel(n_full, tail, x_ref, w1_ref, w3_ref, o_ref):
    j = pl.program_id(1)

    def body(cols):
        x = x_ref[...]
        a = _dot(x, w1_ref[:, 0:cols])
        b = _dot(x, w3_ref[:, 0:cols])
        o_ref[:, 0:cols] = (a * jax.nn.sigmoid(a) * b).astype(o_ref.dtype)

    if tail == 0:
        body(o_ref.shape[1])
    else:
        @pl.when(j < n_full)
        def _():
            body(o_ref.shape[1])

        @pl.when(j == n_full)
        def _():
            body(tail)


def _swiglu_up(h, w1, w3, layer, sub, tn=512):
    m, d = h.shape
    f = w1.shape[-1]
    tm = MM_ROW_TILE
    n_full, tail = divmod(f, tn)
    w_spec = pl.BlockSpec((None, None, d, tn), lambda i, j: (layer, sub, 0, j))
    return pl.pallas_call(
        functools.partial(_swiglu_up_kernel, n_full, tail),
        grid=(m // tm, pl.cdiv(f, tn)),
        in_specs=[pl.BlockSpec((tm, d), lambda i, j: (i, 0), pipeline_mode=pl.Buffered(1)),
                  w_spec, w_spec],
        out_specs=pl.BlockSpec((tm, tn), lambda i, j: (i, j)),
        out_shape=jax.ShapeDtypeStruct((m, f), BF16),
        compiler_params=_cparams(("arbitrary", "arbitrary")),
        name="swiglu_up",
    )(h, w1, w3)


def _proj_kernel(x_ref, w_ref, b_ref, o_ref):
    o_ref[...] = (_dot_nt(x_ref[...], w_ref[...]) + b_ref[...]).astype(o_ref.dtype)


def _in_proj(h, w_in_t, b_in3, layer, n_main, tn=1024):
    m, d = h.shape
    tm = MM_ROW_TILE
    return pl.pallas_call(
        _proj_kernel,
        grid=(m // tm, n_main // tn),
        in_specs=[pl.BlockSpec((tm, d), lambda i, j: (i, 0), pipeline_mode=pl.Buffered(1)),
                  pl.BlockSpec((None, tn, d), lambda i, j: (layer, j, 0)),
                  pl.BlockSpec((None, 1, tn), lambda i, j: (layer, 0, j))],
        out_specs=pl.BlockSpec((tm, tn), lambda i, j: (i, j)),
        out_shape=jax.ShapeDtypeStruct((m, n_main), BF16),
        compiler_params=_cparams(("arbitrary", "arbitrary")),
        name="in_proj",
    )(h, w_in_t, b_in3)


def _gate_proj(h, w_in_t, b_in3, layer, n_main, tn=128):
    m, d = h.shape
    tm = MM_ROW_TILE
    jb = n_main // tn
    return pl.pallas_call(
        _proj_kernel,
        grid=(m // tm,),
        in_specs=[pl.BlockSpec((tm, d), lambda i: (i, 0)),
                  pl.BlockSpec((None, tn, d), lambda i: (layer, jb, 0)),
                  pl.BlockSpec((None, 1, tn), lambda i: (layer, 0, jb))],
        out_specs=pl.BlockSpec((tm, tn), lambda i: (i, 0)),
        out_shape=jax.ShapeDtypeStruct((m, tn), F32),
        compiler_params=_cparams(("arbitrary",)),
        name="gate_proj",
    )(h, w_in_t, b_in3)


def _acc_resid_kernel(n_prompt_tiles, sample_rows, k_last_valid, res_scale,
                      a_ref, w_ref, x_ref, gp_ref, gs_ref, o_ref):
    i = pl.program_id(0)
    k = pl.program_id(2)
    nk = pl.num_programs(2)
    tk = a_ref.shape[1]
    n_bs = gs_ref.shape[0]

    def sweep(rows, kind, gate_of):
        kv = k_last_valid if kind == "last" else tk
        for r0 in range(0, rows, ACC_SUB_ROWS):
            rs = slice(r0, r0 + ACC_SUB_ROWS)
            part = _dot(a_ref[rs, 0:kv], w_ref[0:kv, :])
            if kind == "first":
                o_ref[rs, :] = part
            elif kind == "mid":
                o_ref[rs, :] += part
            else:
                for q0 in range(r0, r0 + ACC_SUB_ROWS, n_bs):
                    qs = slice(q0, q0 + n_bs)
                    ps = slice(q0 - r0, q0 - r0 + n_bs)
                    o_ref[qs, :] = x_ref[qs, :] + (res_scale * gate_of()) * (o_ref[qs, :] + part[ps, :])

    def steps(rows, gate_of):
        @pl.when(k == 0)
        def _():
            sweep(rows, "first", gate_of)

        @pl.when(jnp.logical_and(k > 0, k < nk - 1))
        def _():
            sweep(rows, "mid", gate_of)

        @pl.when(k == nk - 1)
        def _():
            sweep(rows, "last", gate_of)

    @pl.when(i < n_prompt_tiles)
    def _():
        steps(a_ref.shape[0], lambda: gp_ref[...])

    @pl.when(i >= n_prompt_tiles)
    def _():
        steps(sample_rows, lambda: gs_ref[...])


def _acc_resid(a, w_full, w_index, x, mods, mods_p, layer, which, sample_rows, res_scale):
    m, kdim = a.shape
    d = x.shape[1]
    tm, tk, tn = ACC_ROW_TILE, ACC_K_TILE, ACC_N_TILE
    n_bp = mods_p.shape[2]
    n_bs = mods.shape[2] - 8
    assert m == n_bp * tm + sample_rows and tm % ACC_SUB_ROWS == 0
    assert sample_rows % ACC_SUB_ROWS == 0 and ACC_SUB_ROWS % n_bs == 0
    nk = pl.cdiv(kdim, tk)
    assert nk >= 3
    k_last_valid = kdim - (nk - 1) * tk
    lead = (None,) * len(w_index)
    return pl.pallas_call(
        functools.partial(_acc_resid_kernel, n_bp, sample_rows, k_last_valid, res_scale),
        grid=(n_bp + 1, d // tn, nk),
        in_specs=[
            pl.BlockSpec((tm, tk), lambda i, j, k: (i, k)),
            pl.BlockSpec(lead + (tk, tn), lambda i, j, k: tuple(w_index) + (k, j)),
            pl.BlockSpec((tm, tn), lambda i, j, k: (i, j)),
            pl.BlockSpec((None, None, None, 1, tn),
                         lambda i, j, k: (layer, which, jnp.minimum(i, n_bp - 1), 0, j)),
            pl.BlockSpec((None, None, n_bs, tn), lambda i, j, k: (layer, which, 0, j)),
        ],
        out_specs=pl.BlockSpec((tm, tn), lambda i, j, k: (i, j)),
        out_shape=jax.ShapeDtypeStruct((m, d), F32),
        compiler_params=_cparams(("arbitrary", "arbitrary", "arbitrary")),
        name="acc_resid",
    )(a, w_full, x, mods_p, mods)


HALO = 16


def _pool_prompt_kernel(start, u_ref, wp_ref, sp_ref, y_ref, hist_ref, z_ref):
    t = pl.program_id(1)
    nt = pl.num_programs(1)
    tt = u_ref.shape[0]
    group = wp_ref.shape[1]

    @pl.when(t == 0)
    def _():
        z_ref[0:HALO, :] = jnp.zeros((HALO, z_ref.shape[1]), F32)

    z_ref[HALO:HALO + tt, :] = u_ref[...].astype(F32)

    pos = start + t * tt + lax.broadcasted_iota(jnp.int32, (tt, 1), 0)
    for g, w in enumerate(POOL_WINDOWS):
        cols = slice(g * group, (g + 1) * group)
        cur = z_ref[HALO:HALO + tt, cols]
        acc = cur
        for j in range(1, w):
            acc = acc + z_ref[HALO - j:HALO - j + tt, cols]
        cnt = jnp.minimum(w, pos + 1).astype(F32)
        pooled = acc / cnt - cur
        y = _dot(pooled.astype(BF16), wp_ref[g]) * sp_ref[:, cols]
        y_ref[:, cols] = y.astype(y_ref.dtype)

    @pl.when(t == nt - 1)
    def _():
        hist_ref[...] = z_ref[HALO + tt - (HALO - 1):HALO + tt, :]

    z_ref[0:HALO, :] = z_ref[tt:tt + HALO, :]


def _pool_prompt(proj, w_pool_l, s_pool_l, n_b, seq, d, tt=256):
    pw = s_pool_l.shape[-1]
    ntt = seq // tt
    n_groups, group, _ = w_pool_l.shape
    return pl.pallas_call(
        functools.partial(_pool_prompt_kernel, 0),
        grid=(n_b, ntt),
        in_specs=[
            pl.BlockSpec((tt, pw), lambda b, t: (b * ntt + t, 0)),
            pl.BlockSpec((n_groups, group, group), lambda b, t: (0, 0, 0)),
            pl.BlockSpec((1, pw), lambda b, t: (0, 0)),
        ],
        out_specs=[
            pl.BlockSpec((tt, pw), lambda b, t: (b * ntt + t, 0)),
            pl.BlockSpec((None, HALO - 1, pw), lambda b, t: (b, 0, 0)),
        ],
        out_shape=[jax.ShapeDtypeStruct((proj.shape[0], d), BF16),
                   jax.ShapeDtypeStruct((n_b, HALO - 1, pw), F32)],
        scratch_shapes=[pltpu.VMEM((HALO + tt, pw), F32)],
        compiler_params=_cparams(("arbitrary", "arbitrary")),
        name="pool_prompt",
    )(proj, w_pool_l, s_pool_l.reshape(1, pw))


def _pool_sample_kernel(hist_ref, u_ref, wp_ref, sp_ref, ymix_in_ref, y_ref):
    del ymix_in_ref
    n_hist, n_b, _ = hist_ref.shape
    n_t = u_ref.shape[0] // n_b
    g = pl.program_id(0)

    def z(r):
        if r < n_hist:
            return hist_ref[r]
        return u_ref[(r - n_hist) * n_b:(r - n_hist + 1) * n_b, :].astype(F32)

    for t in range(n_t):
        cur = z(n_hist + t)
        run = cur
        sums = []
        for j in range(1, POOL_WINDOWS[-1]):
            run = run + z(n_hist + t - j)
            if j + 1 in POOL_WINDOWS:
                sums.append(run)
        pooled = jnp.zeros_like(cur)
        for gi, w in enumerate(POOL_WINDOWS):
            pooled = jnp.where(g == gi, sums[gi] / float(w) - cur, pooled)
        y = _dot(pooled.astype(BF16), wp_ref[...]) * sp_ref[...]
        y_ref[t * n_b:(t + 1) * n_b, :] = y.astype(y_ref.dtype)


def _pool_sample(hist_tm, proj, w_pool_l, s_pool_l, ymix, row0):
    n_hist, n_b, pw = hist_tm.shape
    n_groups, group, _ = w_pool_l.shape
    rows = proj.shape[0] - row0
    rb = row0 // rows
    return pl.pallas_call(
        _pool_sample_kernel,
        grid=(n_groups,),
        in_specs=[
            pl.BlockSpec((n_hist, n_b, group), lambda g: (0, 0, g)),
            pl.BlockSpec((rows, group), lambda g: (rb, g)),
            pl.BlockSpec((None, group, group), lambda g: (g, 0, 0)),
            pl.BlockSpec((1, group), lambda g: (0, g)),
            pl.BlockSpec(memory_space=pl.ANY),
        ],
        out_specs=pl.BlockSpec((rows, group), lambda g: (rb, g)),
        out_shape=jax.ShapeDtypeStruct(ymix.shape, ymix.dtype),
        input_output_aliases={4: 0},
        compiler_params=_cparams(("arbitrary",)),
        name="pool_sample",
    )(hist_tm, proj, w_pool_l, s_pool_l.reshape(1, pw), ymix)


def _mlstm_segment(q, k, v, ig, lf, c_state, n_state, m_state):
    r = q.shape[0]
    row = lax.broadcasted_iota(jnp.int32, (r, r), 0)
    col = lax.broadcasted_iota(jnp.int32, (r, r), 1)
    causal = col <= row
    lf_rows = jnp.sum(jnp.where(row == col, lf, 0.0), axis=0, keepdims=True)
    ig_rows = jnp.sum(jnp.where(row == col, ig, 0.0), axis=0, keepdims=True)
    b_col = jnp.sum(jnp.where(causal, lf_rows, 0.0), axis=1, keepdims=True)
    b_rows = jnp.sum(jnp.where(row <= col, lf, 0.0), axis=0, keepdims=True)
    dmat = jnp.where(causal, b_col - b_rows + ig_rows, -jnp.inf)
    inter = b_col + m_state
    m_tok = jnp.maximum(inter, jnp.max(dmat, axis=-1, keepdims=True))
    w_intra = jnp.exp(dmat - m_tok)
    w_inter = jnp.exp(inter - m_tok)
    s = _dot_nt(q, k) * w_intra
    num = _dot(s.astype(BF16), v) + w_inter * _dot_nt(q, c_state)
    qn = jnp.sum(q.astype(F32) * n_state, axis=-1, keepdims=True)
    den = jnp.sum(s, axis=-1, keepdims=True) + w_inter * qn
    h = num / jnp.maximum(jnp.abs(den), jnp.exp(-m_tok))
    b_last = jnp.sum(lf, axis=0, keepdims=True)
    dec = b_last - b_col + ig
    m_new = jnp.maximum(b_last + m_state, jnp.max(dec, axis=0, keepdims=True))
    ws = jnp.exp(dec - m_new)
    wc = jnp.exp(b_last + m_state - m_new)
    kf = k.astype(F32)
    c_new = wc * c_state + _dot_tn((ws * v.astype(F32)).astype(BF16), k)
    n_new = wc * n_state + jnp.sum(ws * kf, axis=0, keepdims=True)
    return h, c_new, n_new, m_new


def _head_out(h, o, g_head):
    hn = h * lax.rsqrt(jnp.mean(h * h, axis=-1, keepdims=True) + EPS) * g_head
    return hn * jax.nn.sigmoid(o.astype(F32))


def _log_sigmoid(x):
    return jnp.minimum(x, 0.0) - jnp.log1p(jnp.exp(-jnp.abs(x)))


def _mlstm_prompt_kernel(q_ref, k_ref, v_ref, o_ref, gt_ref, gh_ref, ymix_in_ref,
                         y_ref, c_out_ref, n_out_ref, m_out_ref, c_s, n_s, m_s):
    del ymix_in_ref
    c = pl.program_id(1)
    nc = pl.num_programs(1)
    dk = c_s.shape[2]
    dv = c_s.shape[1]

    @pl.when(c == 0)
    def _():
        c_s[...] = jnp.zeros(c_s.shape, F32)
        n_s[...] = jnp.zeros(n_s.shape, F32)
        m_s[...] = jnp.zeros(m_s.shape, F32)

    for h in range(N_HEADS):
        q = q_ref[:, h * dk:(h + 1) * dk]
        k = (k_ref[:, h * dk:(h + 1) * dk].astype(F32) * (dk ** -0.5)).astype(BF16)
        v = v_ref[:, h * dv:(h + 1) * dv]
        ig = gt_ref[:, h:h + 1]
        lf = _log_sigmoid(gt_ref[:, N_HEADS + h:N_HEADS + h + 1])
        hh, c_new, n_new, m_new = _mlstm_segment(
            q, k, v, ig, lf, c_s[h], n_s[h:h + 1, :], m_s[h:h + 1, 0:1])
        c_s[h] = c_new
        n_s[h:h + 1, :] = n_new
        m_s[h:h + 1, :] = jnp.broadcast_to(m_new, (1, m_s.shape[1]))
        y_ref[:, h * dv:(h + 1) * dv] = _head_out(
            hh, o_ref[:, h * dv:(h + 1) * dv], gh_ref[h:h + 1, :]).astype(y_ref.dtype)

    @pl.when(c == nc - 1)
    def _():
        c_out_ref[...] = c_s[...]
        n_out_ref[...] = n_s[0:N_HEADS, :]
        m_out_ref[...] = m_s[...]


def _mlstm_prompt(proj, gates, g_head_l, ymix, n_b, seq, col_q, dk, dv):
    lc = PROMPT_CHUNK
    nch = seq // lc
    qk_w = N_HEADS * dk
    v_w = N_HEADS * dv
    row = lambda b, c: b * nch + c
    assert col_q % qk_w == 0 and (col_q + 2 * qk_w) % v_w == 0
    bq = col_q // qk_w
    bv = (col_q + 2 * qk_w) // v_w
    y, c_out, n_out, m_out = pl.pallas_call(
        _mlstm_prompt_kernel,
        grid=(n_b, nch),
        in_specs=[
            pl.BlockSpec((lc, qk_w), lambda b, c: (row(b, c), bq)),
            pl.BlockSpec((lc, qk_w), lambda b, c: (row(b, c), bq + 1)),
            pl.BlockSpec((lc, v_w), lambda b, c: (row(b, c), bv)),
            pl.BlockSpec((lc, v_w), lambda b, c: (row(b, c), bv + 1)),
            pl.BlockSpec((lc, gates.shape[1]), lambda b, c: (row(b, c), 0)),
            pl.BlockSpec((N_HEADS, dv), lambda b, c: (0, 0)),
            pl.BlockSpec(memory_space=pl.ANY),
        ],
        out_specs=[
            pl.BlockSpec((lc, v_w), lambda b, c: (row(b, c), 1)),
            pl.BlockSpec((None, N_HEADS, dv, dk), lambda b, c: (b, 0, 0, 0)),
            pl.BlockSpec((None, N_HEADS, dk), lambda b, c: (b, 0, 0)),
            pl.BlockSpec((None, 8, 128), lambda b, c: (b, 0, 0)),
        ],
        out_shape=[jax.ShapeDtypeStruct(ymix.shape, ymix.dtype),
                   jax.ShapeDtypeStruct((n_b, N_HEADS, dv, dk), F32),
                   jax.ShapeDtypeStruct((n_b, N_HEADS, dk), F32),
                   jax.ShapeDtypeStruct((n_b, 8, 128), F32)],
        scratch_shapes=[pltpu.VMEM((N_HEADS, dv, dk), F32),
                        pltpu.VMEM((8, dk), F32),
                        pltpu.VMEM((8, 128), F32)],
        input_output_aliases={6: 0},
        compiler_params=_cparams(("arbitrary", "arbitrary")),
        name="mlstm_prompt",
    )(proj, proj, proj, proj, gates, g_head_l, ymix)
    return y, c_out, n_out, m_out[:, :N_HEADS, 0]


def _mlstm_sample_kernel(seq, has_prev, q_ref, k_ref, v_ref, o_ref, gt_ref, gh_ref, c0_ref, n0_ref, m0_ref,
                         *rest):
    y_ref, c_out_ref, n_out_ref, m_out_ref = rest[-4:]
    rows = q_ref.shape[0]
    dk = c0_ref.shape[3]
    dv = c0_ref.shape[2]
    seg_of_row = lax.broadcasted_iota(jnp.int32, (rows, 1), 0) // seq
    for h in range(N_HEADS):
        q = q_ref[:, h * dk:(h + 1) * dk]
        k = (k_ref[:, h * dk:(h + 1) * dk].astype(F32) * (dk ** -0.5)).astype(BF16)
        v = v_ref[:, h * dv:(h + 1) * dv]
        ig_all = gt_ref[:, h:h + 1]
        lf_all = _log_sigmoid(gt_ref[:, N_HEADS + h:N_HEADS + h + 1])
        hh = jnp.zeros((rows, dv), F32)
        for sgm in range(rows // seq):
            mine = seg_of_row == sgm
            ig = jnp.where(mine, ig_all, NEG_BIG)
            lf = jnp.where(mine, lf_all, 0.0)
            h_seg, c_new, n_new, m_new = _mlstm_segment(
                q, k, v, ig, lf, c0_ref[sgm, h], n0_ref[sgm, h:h + 1, :], m0_ref[sgm, h:h + 1, :])
            hh = jnp.where(mine, h_seg, hh)
            c_out_ref[sgm, h] = c_new
            n_out_ref[sgm, h:h + 1, :] = n_new
            m_out_ref[sgm, h:h + 1, :] = m_new
        y_ref[:, h * dv:(h + 1) * dv] = _head_out(
            hh, o_ref[:, h * dv:(h + 1) * dv], gh_ref[h:h + 1, :]).astype(y_ref.dtype)


def _mlstm_sample(proj_bm, gates_bm, g_head_l, state_c, state_n, state_m, layer, c_prev, seq, col_q, dk, dv):
    depth, n_b = state_c.shape[:2]
    grp = SAMPLE_GROUP
    rows = grp * seq
    qk_w = N_HEADS * dk
    v_w = N_HEADS * dv
    bq = col_q // qk_w
    bv = (col_q + 2 * qk_w) // v_w
    has_prev = c_prev is not None
    in_specs = [
        pl.BlockSpec((rows, qk_w), lambda i: (i, bq)),
        pl.BlockSpec((rows, qk_w), lambda i: (i, bq + 1)),
        pl.BlockSpec((rows, v_w), lambda i: (i, bv)),
        pl.BlockSpec((rows, v_w), lambda i: (i, bv + 1)),
        pl.BlockSpec((rows, gates_bm.shape[1]), lambda i: (i, 0)),
        pl.BlockSpec((N_HEADS, dv), lambda i: (0, 0)),
        pl.BlockSpec((None, grp, N_HEADS, dv, dk), lambda i: (layer, i, 0, 0, 0)),
        pl.BlockSpec((None, grp, N_HEADS, dk), lambda i: (layer, i, 0, 0)),
        pl.BlockSpec((None, grp, N_HEADS, 1), lambda i: (layer, i, 0, 0)),
    ]
    args = [proj_bm, proj_bm, proj_bm, proj_bm, gates_bm, g_head_l, state_c, state_n,
            state_m.reshape(depth, n_b, N_HEADS, 1)]
    aliases = {}
    if has_prev:
        in_specs.append(pl.BlockSpec(memory_space=pl.ANY))
        args.append(c_prev)
        aliases = {len(args) - 1: 1}
    y, c_out, n_out, m_out = pl.pallas_call(
        functools.partial(_mlstm_sample_kernel, seq, has_prev),
        grid=(n_b // grp,),
        in_specs=in_specs,
        out_specs=[
            pl.BlockSpec((rows, v_w), lambda i: (i, 0)),
            pl.BlockSpec((None, grp, N_HEADS, dv, dk), lambda i: (layer, i, 0, 0, 0)),
            pl.BlockSpec((grp, N_HEADS, dk), lambda i: (i, 0, 0)),
            pl.BlockSpec((grp, N_HEADS, 1), lambda i: (i, 0, 0)),
        ],
        out_shape=[jax.ShapeDtypeStruct((n_b * seq, v_w), BF16),
                   jax.ShapeDtypeStruct(state_c.shape, F32),
                   jax.ShapeDtypeStruct((n_b, N_HEADS, dk), F32),
                   jax.ShapeDtypeStruct((n_b, N_HEADS, 1), F32)],
        input_output_aliases=aliases,
        compiler_params=_cparams(("arbitrary",)),
        name="mlstm_sample",
    )(*args)
    return y, c_out, n_out, m_out[:, :, 0]


def _to_time_major(a, n_b, seq):
    return jnp.transpose(a.reshape(n_b, seq, -1), (1, 0, 2)).reshape(n_b * seq, -1)


def _to_batch_major(a, n_b, seq):
    return jnp.transpose(a.reshape(seq, n_b, -1), (1, 0, 2)).reshape(n_b * seq, -1)


def kernel(x_prompt, x_sample, state_pool, state_C, state_n, state_m, c_prompt, c_sample,
           w_ada, b_ada, g_norm, w_in, b_in, w_pool, s_pool, g_head, w_out, w1, w3, w2, g_final):
    n_bp, seq_p, d = x_prompt.shape
    n_bs, seq_s, _ = x_sample.shape
    depth = w_ada.shape[0]
    pw = s_pool.shape[-1]
    dv = g_head.shape[-1]
    dk = state_C.shape[-1]
    n_hist = state_pool.shape[2]
    rows_p = n_bp * seq_p
    rows_s = n_bs * seq_s
    n_main = pw + 2 * N_HEADS * dk + 2 * N_HEADS * dv

    xp0 = x_prompt.reshape(rows_p, d)
    xs0 = _to_time_major(x_sample, n_bs, seq_s)

    c_all = jnp.concatenate([c_sample, c_prompt, jnp.zeros((8 - n_bp, d), F32)], axis=0)
    mods = _ada(c_all, w_ada, b_ada)
    mods_p = mods[:, :, n_bs:n_bs + n_bp][:, :, :, None, :]

    b_in3 = b_in.reshape(depth, 1, b_in.shape[-1])
    w_in_t = jnp.transpose(w_in, (0, 2, 1))
    pool_p, c_p, n_p, m_p = [], [], [], []
    pool_s, n_s, m_s = [], [], []
    c_s_all = None

    def ffn(x, h, l, sub_layer, ffn_idx):
        act = _swiglu_up(h, w1, w3, l, ffn_idx)
        return _acc_resid(act, w2, (l, ffn_idx), x, mods, mods_p, l, sub_layer * N_MOD + 2,
                          rows_s, FFN_RES)

    x = None
    for l in range(depth):
        if l == 0:
            h, x = _normmod(xp0, xs0, 0, g_norm[l, 0], mods, mods_p, l, 0, seq_p, True)
        else:
            h, = _normmod(x, x, rows_p, g_norm[l, 0], mods, mods_p, l, 0, seq_p, False)
        x = ffn(x, h, l, 0, 0)

        h, = _normmod(x, x, rows_p, g_norm[l, 1], mods, mods_p, l, 1, seq_p, False)
        proj = _in_proj(h, w_in_t, b_in3, l, n_main)
        gates = _gate_proj(h, w_in_t, b_in3, l, n_main)

        ymix, hist_p = _pool_prompt(proj, w_pool[l], s_pool[l], n_bp, seq_p, d)
        ymix, cp, np_, mp = _mlstm_prompt(proj, gates, g_head[l], ymix, n_bp, seq_p, pw, dk, dv)
        hist_tm = jnp.transpose(state_pool[l], (1, 0, 2))
        ymix = _pool_sample(hist_tm, proj, w_pool[l], s_pool[l], ymix, rows_p)
        proj_bm = _to_batch_major(proj[rows_p:], n_bs, seq_s)
        gates_bm = _to_batch_major(gates[rows_p:], n_bs, seq_s)
        y_ms, c_s_all, ns, ms = _mlstm_sample(proj_bm, gates_bm, g_head[l], state_C, state_n, state_m,
                                              l, c_s_all, seq_s, pw, dk, dv)
        ymix = lax.dynamic_update_slice(ymix, _to_time_major(y_ms, n_bs, seq_s), (rows_p, pw))
        x = _acc_resid(ymix, w_out, (l,), x, mods, mods_p, l, 1 * N_MOD + 2, rows_s, 1.0)

        u_s = proj_bm[:, :pw].astype(F32).reshape(n_bs, seq_s, pw)
        pool_p.append(hist_p); c_p.append(cp); n_p.append(np_); m_p.append(mp)
        pool_s.append(jnp.concatenate([state_pool[l, :, seq_s:], u_s], axis=1))
        n_s.append(ns); m_s.append(ms)

        h, = _normmod(x, x, rows_p, g_norm[l, 2], mods, mods_p, l, 2, seq_p, False)
        x = ffn(x, h, l, 2, 1)

    y_prompt = _final_norm(x, g_final, 0, rows_p).reshape(n_bp, seq_p, d)
    y_sample = _to_batch_major(_final_norm(x, g_final, rows_p, rows_s), n_bs, seq_s).reshape(n_bs, seq_s, d)
    return (y_prompt, y_sample,
            jnp.stack(pool_p), jnp.stack(c_p), jnp.stack(n_p), jnp.stack(m_p),
            jnp.stack(pool_s), c_s_all, jnp.stack(n_s), jnp.stack(m_s))
```

```python
import functools

import jax
import jax.numpy as jnp
from jax import lax
from jax.experimental import pallas as pl
from jax.experimental.pallas import tpu as pltpu

F32 = jnp.float32
BF16 = jnp.bfloat16

EPS = 1e-6
FFN_RES = 0.5
POOL_WINDOWS = (2, 4, 8, 16)
N_HEADS = 4
N_SUB = 3
N_MOD = 3
PROMPT_CHUNK = 256
SAMPLE_GROUP = 4
NEG_BIG = -1e30

VMEM_LIMIT = 56 * 1024 * 1024

ROW_TILE = 256
MM_ROW_TILE = 2176
ACC_ROW_TILE = 2048
ACC_SUB_ROWS = 512
ACC_K_TILE = 1024
ACC_N_TILE = 1024


def _cparams(sem):
    return pltpu.CompilerParams(dimension_semantics=sem, vmem_limit_bytes=VMEM_LIMIT)


def _dot(a, b):
    return lax.dot_general(a, b, (((1,), (0,)), ((), ())), preferred_element_type=F32)


def _dot_nt(a, b):
    return lax.dot_general(a, b, (((1,), (1,)), ((), ())), preferred_element_type=F32)


def _dot_tn(a, b):
    return lax.dot_general(a, b, (((0,), (0,)), ((), ())), preferred_element_type=F32)


def _ada_kernel(c_ref, w_ref, b_ref, o_ref):
    c = c_ref[...]
    sc = (c * jax.nn.sigmoid(c)).astype(BF16)
    o_ref[...] = _dot(sc, w_ref[...]) + b_ref[...]


def _ada(c_all, w_ada, b_ada, tn=1024):
    depth, d, n = w_ada.shape
    rows = c_all.shape[0]
    per = d // tn
    return pl.pallas_call(
        _ada_kernel,
        grid=(depth, n // tn),
        in_specs=[
            pl.BlockSpec((rows, d), lambda l, j: (0, 0)),
            pl.BlockSpec((None, d, tn), lambda l, j: (l, 0, j)),
            pl.BlockSpec((None, 1, tn), lambda l, j: (l, 0, j)),
        ],
        out_specs=pl.BlockSpec((None, None, rows, tn), lambda l, j: (l, j // per, 0, j % per)),
        out_shape=jax.ShapeDtypeStruct((depth, n // d, rows, d), F32),
        compiler_params=_cparams(("arbitrary", "arbitrary")),
        name="ada_mods",
    )(c_all, w_ada, b_ada.reshape(depth, 1, n))


def _rms(x, g):
    return x * lax.rsqrt(jnp.mean(x * x, axis=-1, keepdims=True) + EPS) * g


NORM_CHUNK = 8


def _normmod_kernel(n_prompt_tiles, emit_x, xp_ref, xs_ref, g_ref, shp_ref, scp_ref, shs_ref, scs_ref,
                    h_ref, *rest):
    xo_ref = rest[0] if emit_x else None
    gm_ref = rest[-1]
    i = pl.program_id(0)
    n_bs = shs_ref.shape[0]
    n_chunks = h_ref.shape[0] // NORM_CHUNK

    def run(x_ref, mod_rows):
        def chunk(c, carry):
            r = pl.multiple_of(c * NORM_CHUNK, NORM_CHUNK)
            rows = pl.ds(r, NORM_CHUNK)
            x = x_ref[rows, :]
            rstd = lax.rsqrt(jnp.mean(x * x, axis=-1, keepdims=True) + EPS)
            gm, sh = mod_rows(r)
            h_ref[rows, :] = (x * rstd * gm + sh).astype(h_ref.dtype)
            if emit_x:
                xo_ref[rows, :] = x
            return carry

        lax.fori_loop(0, n_chunks, chunk, 0, unroll=4)

    @pl.when(i < n_prompt_tiles)
    def _():
        gm_ref[0:1, :] = g_ref[...] * (1.0 + scp_ref[...])
        run(xp_ref, lambda r: (gm_ref[0:1, :], shp_ref[...]))

    @pl.when(i >= n_prompt_tiles)
    def _():
        gm_ref[...] = g_ref[...] * (1.0 + scs_ref[...])

        def mod_rows(r):
            rb = pl.ds(pl.multiple_of(lax.rem(r, n_bs), NORM_CHUNK), NORM_CHUNK)
            return gm_ref[rb, :], shs_ref[rb, :]

        run(xs_ref, mod_rows)


def _normmod(xp, xs, xs_row0, g, mods, mods_p, layer, sub, seq, emit_x):
    d = xp.shape[1]
    tr = ROW_TILE if emit_x else 2 * ROW_TILE
    n_bp = mods_p.shape[2]
    n_bs = mods.shape[2] - 8
    rows_p = n_bp * seq
    rows_s = xs.shape[0] - xs_row0
    m = rows_p + rows_s
    n_pt = rows_p // tr
    tiles_per_seq = seq // tr
    s0 = xs_row0 // tr
    i_shift, i_scale = sub * N_MOD, sub * N_MOD + 1
    p_map = lambda which: (lambda i: (layer, which, jnp.minimum(i // tiles_per_seq, n_bp - 1), 0, 0))
    s_map = lambda which: (lambda i: (layer, which, 0, 0))
    out_specs = [pl.BlockSpec((tr, d), lambda i: (i, 0))]
    out_shape = [jax.ShapeDtypeStruct((m, d), BF16)]
    if emit_x:
        out_specs.append(pl.BlockSpec((tr, d), lambda i: (i, 0)))
        out_shape.append(jax.ShapeDtypeStruct((m, d), F32))
    return pl.pallas_call(
        functools.partial(_normmod_kernel, n_pt, emit_x),
        grid=(m // tr,),
        in_specs=[
            pl.BlockSpec((tr, d), lambda i: (jnp.minimum(i, n_pt - 1), 0)),
            pl.BlockSpec((tr, d), lambda i: (s0 + jnp.maximum(i - n_pt, 0), 0)),
            pl.BlockSpec((1, d), lambda i: (0, 0)),
            pl.BlockSpec((None, None, None, 1, d), p_map(i_shift)),
            pl.BlockSpec((None, None, None, 1, d), p_map(i_scale)),
            pl.BlockSpec((None, None, n_bs, d), s_map(i_shift)),
            pl.BlockSpec((None, None, n_bs, d), s_map(i_scale)),
        ],
        out_specs=out_specs,
        out_shape=out_shape,
        scratch_shapes=[pltpu.VMEM((n_bs, d), F32)],
        compiler_params=_cparams(("arbitrary",)),
        name="normmod",
    )(xp, xs, g.reshape(1, d), mods_p, mods_p, mods, mods)


def _final_norm_kernel(x_ref, g_ref, o_ref):
    def chunk(c, carry):
        rows = pl.ds(pl.multiple_of(c * NORM_CHUNK, NORM_CHUNK), NORM_CHUNK)
        o_ref[rows, :] = _rms(x_ref[rows, :], g_ref[...])
        return carry

    lax.fori_loop(0, o_ref.shape[0] // NORM_CHUNK, chunk, 0, unroll=4)


def _final_norm(x, g, row0, rows):
    d = x.shape[1]
    tr = ROW_TILE
    b0 = row0 // tr
    return pl.pallas_call(
        _final_norm_kernel,
        grid=(rows // tr,),
        in_specs=[pl.BlockSpec((tr, d), lambda i: (i + b0, 0)),
                  pl.BlockSpec((1, d), lambda i: (0, 0))],
        out_specs=pl.BlockSpec((tr, d), lambda i: (i, 0)),
        out_shape=jax.ShapeDtypeStruct((rows, d), F32),
        compiler_params=_cparams(("arbitrary",)),
        name="final_norm",
    )(x, g.reshape(1, d))


def _swiglu_up_kernel(n_full, tail, x_ref, w1_ref, w3_ref, o_ref):
    j = pl.program_id(1)

    def body(cols):
        x = x_ref[...]
        a = _dot(x, w1_ref[:, 0:cols])
        b = _dot(x, w3_ref[:, 0:cols])
        o_ref[:, 0:cols] = (a * jax.nn.sigmoid(a) * b).astype(o_ref.dtype)

    if tail == 0:
        body(o_ref.shape[1])
    else:
        @pl.when(j < n_full)
        def _():
            body(o_ref.shape[1])

        @pl.when(j == n_full)
        def _():
            body(tail)


def _swiglu_up(h, w1, w3, layer, sub, tn=256):
    m, d = h.shape
    f = w1.shape[-1]
    tm = MM_ROW_TILE
    n_full, tail = divmod(f, tn)
    w_spec = pl.BlockSpec((None, None, d, tn), lambda i, j: (layer, sub, 0, j))
    return pl.pallas_call(
        functools.partial(_swiglu_up_kernel, n_full, tail),
        grid=(m // tm, pl.cdiv(f, tn)),
        in_specs=[pl.BlockSpec((tm, d), lambda i, j: (i, 0), pipeline_mode=pl.Buffered(1)),
                  w_spec, w_spec],
        out_specs=pl.BlockSpec((tm, tn), lambda i, j: (i, j)),
        out_shape=jax.ShapeDtypeStruct((m, f), BF16),
        compiler_params=_cparams(("arbitrary", "arbitrary")),
        name="swiglu_up",
    )(h, w1, w3)


def _proj_kernel(x_ref, w_ref, b_ref, o_ref):
    o_ref[...] = (_dot_nt(x_ref[...], w_ref[...]) + b_ref[...]).astype(o_ref.dtype)


def _in_proj(h, w_in_t, b_in3, layer, n_main, tn=512):
    m, d = h.shape
    tm = MM_ROW_TILE
    return pl.pallas_call(
        _proj_kernel,
        grid=(m // tm, n_main // tn),
        in_specs=[pl.BlockSpec((tm, d), lambda i, j: (i, 0), pipeline_mode=pl.Buffered(1)),
                  pl.BlockSpec((None, tn, d), lambda i, j: (layer, j, 0)),
                  pl.BlockSpec((None, 1, tn), lambda i, j: (layer, 0, j))],
        out_specs=pl.BlockSpec((tm, tn), lambda i, j: (i, j)),
        out_shape=jax.ShapeDtypeStruct((m, n_main), BF16),
        compiler_params=_cparams(("arbitrary", "arbitrary")),
        name="in_proj",
    )(h, w_in_t, b_in3)


def _gate_proj(h, w_in_t, b_in3, layer, n_main, tn=128):
    m, d = h.shape
    tm = MM_ROW_TILE
    jb = n_main // tn
    return pl.pallas_call(
        _proj_kernel,
        grid=(m // tm,),
        in_specs=[pl.BlockSpec((tm, d), lambda i: (i, 0)),
                  pl.BlockSpec((None, tn, d), lambda i: (layer, jb, 0)),
                  pl.BlockSpec((None, 1, tn), lambda i: (layer, 0, jb))],
        out_specs=pl.BlockSpec((tm, tn), lambda i: (i, 0)),
        out_shape=jax.ShapeDtypeStruct((m, tn), F32),
        compiler_params=_cparams(("arbitrary",)),
        name="gate_proj",
    )(h, w_in_t, b_in3)


def _acc_resid_kernel(n_prompt_tiles, sample_rows, k_last_valid, res_scale,
                      a_ref, w_ref, x_ref, gp_ref, gs_ref, o_ref):
    i = pl.program_id(0)
    k = pl.program_id(2)
    nk = pl.num_programs(2)
    tk = a_ref.shape[1]
    n_bs = gs_ref.shape[0]

    def sweep(rows, kind, gate_of):
        kv = k_last_valid if kind == "last" else tk
        for r0 in range(0, rows, ACC_SUB_ROWS):
            rs = slice(r0, r0 + ACC_SUB_ROWS)
            part = _dot(a_ref[rs, 0:kv], w_ref[0:kv, :])
            if kind == "first":
                o_ref[rs, :] = part
            elif kind == "mid":
                o_ref[rs, :] += part
            else:
                for q0 in range(r0, r0 + ACC_SUB_ROWS, n_bs):
                    qs = slice(q0, q0 + n_bs)
                    ps = slice(q0 - r0, q0 - r0 + n_bs)
                    o_ref[qs, :] = x_ref[qs, :] + (res_scale * gate_of()) * (o_ref[qs, :] + part[ps, :])

    def steps(rows, gate_of):
        @pl.when(k == 0)
        def _():
            sweep(rows, "first", gate_of)

        @pl.when(jnp.logical_and(k > 0, k < nk - 1))
        def _():
            sweep(rows, "mid", gate_of)

        @pl.when(k == nk - 1)
        def _():
            sweep(rows, "last", gate_of)

    @pl.when(i < n_prompt_tiles)
    def _():
        steps(a_ref.shape[0], lambda: gp_ref[...])

    @pl.when(i >= n_prompt_tiles)
    def _():
        steps(sample_rows, lambda: gs_ref[...])


def _acc_resid(a, w_full, w_index, x, mods, mods_p, layer, which, sample_rows, res_scale):
    m, kdim = a.shape
    d = x.shape[1]
    tm, tk, tn = ACC_ROW_TILE, ACC_K_TILE, ACC_N_TILE
    n_bp = mods_p.shape[2]
    n_bs = mods.shape[2] - 8
    assert m == n_bp * tm + sample_rows and tm % ACC_SUB_ROWS == 0
    assert sample_rows % ACC_SUB_ROWS == 0 and ACC_SUB_ROWS % n_bs == 0
    nk = pl.cdiv(kdim, tk)
    assert nk >= 3
    k_last_valid = kdim - (nk - 1) * tk
    lead = (None,) * len(w_index)
    return pl.pallas_call(
        functools.partial(_acc_resid_kernel, n_bp, sample_rows, k_last_valid, res_scale),
        grid=(n_bp + 1, d // tn, nk),
        in_specs=[
            pl.BlockSpec((tm, tk), lambda i, j, k: (i, k)),
            pl.BlockSpec(lead + (tk, tn), lambda i, j, k: tuple(w_index) + (k, j)),
            pl.BlockSpec((tm, tn), lambda i, j, k: (i, j)),
            pl.BlockSpec((None, None, None, 1, tn),
                         lambda i, j, k: (layer, which, jnp.minimum(i, n_bp - 1), 0, j)),
            pl.BlockSpec((None, None, n_bs, tn), lambda i, j, k: (layer, which, 0, j)),
        ],
        out_specs=pl.BlockSpec((tm, tn), lambda i, j, k: (i, j)),
        out_shape=jax.ShapeDtypeStruct((m, d), F32),
        compiler_params=_cparams(("arbitrary", "arbitrary", "arbitrary")),
        name="acc_resid",
    )(a, w_full, x, mods_p, mods)


HALO = 16


def _pool_prompt_kernel(start, u_ref, wp_ref, sp_ref, y_ref, hist_ref, z_ref):
    t = pl.program_id(1)
    nt = pl.num_programs(1)
    tt = u_ref.shape[0]
    group = wp_ref.shape[1]

    @pl.when(t == 0)
    def _():
        z_ref[0:HALO, :] = jnp.zeros((HALO, z_ref.shape[1]), F32)

    z_ref[HALO:HALO + tt, :] = u_ref[...].astype(F32)

    pos = start + t * tt + lax.broadcasted_iota(jnp.int32, (tt, 1), 0)
    for g, w in enumerate(POOL_WINDOWS):
        cols = slice(g * group, (g + 1) * group)
        cur = z_ref[HALO:HALO + tt, cols]
        acc = cur
        for j in range(1, w):
            acc = acc + z_ref[HALO - j:HALO - j + tt, cols]
        cnt = jnp.minimum(w, pos + 1).astype(F32)
        pooled = acc / cnt - cur
        y = _dot(pooled.astype(BF16), wp_ref[g]) * sp_ref[:, cols]
        y_ref[:, cols] = y.astype(y_ref.dtype)

    @pl.when(t == nt - 1)
    def _():
        hist_ref[...] = z_ref[HALO + tt - (HALO - 1):HALO + tt, :]

    z_ref[0:HALO, :] = z_ref[tt:tt + HALO, :]


def _pool_prompt(proj, w_pool_l, s_pool_l, n_b, seq, d, tt=256):
    pw = s_pool_l.shape[-1]
    ntt = seq // tt
    n_groups, group, _ = w_pool_l.shape
    return pl.pallas_call(
        functools.partial(_pool_prompt_kernel, 0),
        grid=(n_b, ntt),
        in_specs=[
            pl.BlockSpec((tt, pw), lambda b, t: (b * ntt + t, 0)),
            pl.BlockSpec((n_groups, group, group), lambda b, t: (0, 0, 0)),
            pl.BlockSpec((1, pw), lambda b, t: (0, 0)),
        ],
        out_specs=[
            pl.BlockSpec((tt, pw), lambda b, t: (b * ntt + t, 0)),
            pl.BlockSpec((None, HALO - 1, pw), lambda b, t: (b, 0, 0)),
        ],
        out_shape=[jax.ShapeDtypeStruct((proj.shape[0], d), BF16),
                   jax.ShapeDtypeStruct((n_b, HALO - 1, pw), F32)],
        scratch_shapes=[pltpu.VMEM((HALO + tt, pw), F32)],
        compiler_params=_cparams(("arbitrary", "arbitrary")),
        name="pool_prompt",
    )(proj, w_pool_l, s_pool_l.reshape(1, pw))


def _pool_sample_kernel(hist_ref, u_ref, wp_ref, sp_ref, ymix_in_ref, y_ref):
    del ymix_in_ref
    n_hist, n_b, _ = hist_ref.shape
    n_t = u_ref.shape[0] // n_b
    g = pl.program_id(0)

    def z(r):
        if r < n_hist:
            return hist_ref[r]
        return u_ref[(r - n_hist) * n_b:(r - n_hist + 1) * n_b, :].astype(F32)

    for t in range(n_t):
        cur = z(n_hist + t)
        run = cur
        sums = []
        for j in range(1, POOL_WINDOWS[-1]):
            run = run + z(n_hist + t - j)
            if j + 1 in POOL_WINDOWS:
                sums.append(run)
        pooled = jnp.zeros_like(cur)
        for gi, w in enumerate(POOL_WINDOWS):
            pooled = jnp.where(g == gi, sums[gi] / float(w) - cur, pooled)
        y = _dot(pooled.astype(BF16), wp_ref[...]) * sp_ref[...]
        y_ref[t * n_b:(t + 1) * n_b, :] = y.astype(y_ref.dtype)


def _pool_sample(hist_tm, proj, w_pool_l, s_pool_l, ymix, row0):
    n_hist, n_b, pw = hist_tm.shape
    n_groups, group, _ = w_pool_l.shape
    rows = proj.shape[0] - row0
    rb = row0 // rows
    return pl.pallas_call(
        _pool_sample_kernel,
        grid=(n_groups,),
        in_specs=[
            pl.BlockSpec((n_hist, n_b, group), lambda g: (0, 0, g)),
            pl.BlockSpec((rows, group), lambda g: (rb, g)),
            pl.BlockSpec((None, group, group), lambda g: (g, 0, 0)),
            pl.BlockSpec((1, group), lambda g: (0, g)),
            pl.BlockSpec(memory_space=pl.ANY),
        ],
        out_specs=pl.BlockSpec((rows, group), lambda g: (rb, g)),
        out_shape=jax.ShapeDtypeStruct(ymix.shape, ymix.dtype),
        input_output_aliases={4: 0},
        compiler_params=_cparams(("arbitrary",)),
        name="pool_sample",
    )(hist_tm, proj, w_pool_l, s_pool_l.reshape(1, pw), ymix)


def _mlstm_segment(q, k, v, ig, lf, c_state, n_state, m_state):
    r = q.shape[0]
    row = lax.broadcasted_iota(jnp.int32, (r, r), 0)
    col = lax.broadcasted_iota(jnp.int32, (r, r), 1)
    causal = col <= row
    lf_rows = jnp.sum(jnp.where(row == col, lf, 0.0), axis=0, keepdims=True)
    ig_rows = jnp.sum(jnp.where(row == col, ig, 0.0), axis=0, keepdims=True)
    b_col = jnp.sum(jnp.where(causal, lf_rows, 0.0), axis=1, keepdims=True)
    b_rows = jnp.sum(jnp.where(row <= col, lf, 0.0), axis=0, keepdims=True)
    dmat = jnp.where(causal, b_col - b_rows + ig_rows, -jnp.inf)
    inter = b_col + m_state
    m_tok = jnp.maximum(inter, jnp.max(dmat, axis=-1, keepdims=True))
    w_intra = jnp.exp(dmat - m_tok)
    w_inter = jnp.exp(inter - m_tok)
    s = _dot_nt(q, k) * w_intra
    num = _dot(s.astype(BF16), v) + w_inter * _dot_nt(q, c_state)
    qn = jnp.sum(q.astype(F32) * n_state, axis=-1, keepdims=True)
    den = jnp.sum(s, axis=-1, keepdims=True) + w_inter * qn
    h = num / jnp.maximum(jnp.abs(den), jnp.exp(-m_tok))
    b_last = jnp.sum(lf, axis=0, keepdims=True)
    dec = b_last - b_col + ig
    m_new = jnp.maximum(b_last + m_state, jnp.max(dec, axis=0, keepdims=True))
    ws = jnp.exp(dec - m_new)
    wc = jnp.exp(b_last + m_state - m_new)
    kf = k.astype(F32)
    c_new = wc * c_state + _dot_tn((ws * v.astype(F32)).astype(BF16), k)
    n_new = wc * n_state + jnp.sum(ws * kf, axis=0, keepdims=True)
    return h, c_new, n_new, m_new


def _head_out(h, o, g_head):
    hn = h * lax.rsqrt(jnp.mean(h * h, axis=-1, keepdims=True) + EPS) * g_head
    return hn * jax.nn.sigmoid(o.astype(F32))


def _log_sigmoid(x):
    return jnp.minimum(x, 0.0) - jnp.log1p(jnp.exp(-jnp.abs(x)))


def _mlstm_prompt_kernel(q_ref, k_ref, v_ref, o_ref, gt_ref, gh_ref, ymix_in_ref,
                         y_ref, c_out_ref, n_out_ref, m_out_ref, c_s, n_s, m_s):
    del ymix_in_ref
    c = pl.program_id(1)
    nc = pl.num_programs(1)
    dk = c_s.shape[2]
    dv = c_s.shape[1]

    @pl.when(c == 0)
    def _():
        c_s[...] = jnp.zeros(c_s.shape, F32)
        n_s[...] = jnp.zeros(n_s.shape, F32)
        m_s[...] = jnp.zeros(m_s.shape, F32)

    gates = gt_ref[...]
    log_f = _log_sigmoid(gates)
    for h in range(N_HEADS):
        q = q_ref[:, h * dk:(h + 1) * dk]
        k = (k_ref[:, h * dk:(h + 1) * dk].astype(F32) * (dk ** -0.5)).astype(BF16)
        v = v_ref[:, h * dv:(h + 1) * dv]
        ig = gates[:, h:h + 1]
        lf = log_f[:, N_HEADS + h:N_HEADS + h + 1]
        hh, c_new, n_new, m_new = _mlstm_segment(
            q, k, v, ig, lf, c_s[h], n_s[h:h + 1, :], m_s[h:h + 1, 0:1])
        c_s[h] = c_new
        n_s[h:h + 1, :] = n_new
        m_s[h:h + 1, :] = jnp.broadcast_to(m_new, (1, m_s.shape[1]))
        y_ref[:, h * dv:(h + 1) * dv] = _head_out(
            hh, o_ref[:, h * dv:(h + 1) * dv], gh_ref[h:h + 1, :]).astype(y_ref.dtype)

    @pl.when(c == nc - 1)
    def _():
        c_out_ref[...] = c_s[...]
        n_out_ref[...] = n_s[0:N_HEADS, :]
        m_out_ref[...] = m_s[...]


def _mlstm_prompt(proj, gates, g_head_l, ymix, n_b, seq, col_q, dk, dv):
    lc = PROMPT_CHUNK
    nch = seq // lc
    qk_w = N_HEADS * dk
    v_w = N_HEADS * dv
    row = lambda b, c: b * nch + c
    assert col_q % qk_w == 0 and (col_q + 2 * qk_w) % v_w == 0
    bq = col_q // qk_w
    bv = (col_q + 2 * qk_w) // v_w
    y, c_out, n_out, m_out = pl.pallas_call(
        _mlstm_prompt_kernel,
        grid=(n_b, nch),
        in_specs=[
            pl.BlockSpec((lc, qk_w), lambda b, c: (row(b, c), bq)),
            pl.BlockSpec((lc, qk_w), lambda b, c: (row(b, c), bq + 1)),
            pl.BlockSpec((lc, v_w), lambda b, c: (row(b, c), bv)),
            pl.BlockSpec((lc, v_w), lambda b, c: (row(b, c), bv + 1)),
            pl.BlockSpec((lc, gates.shape[1]), lambda b, c: (row(b, c), 0)),
            pl.BlockSpec((N_HEADS, dv), lambda b, c: (0, 0)),
            pl.BlockSpec(memory_space=pl.ANY),
        ],
        out_specs=[
            pl.BlockSpec((lc, v_w), lambda b, c: (row(b, c), 1)),
            pl.BlockSpec((None, N_HEADS, dv, dk), lambda b, c: (b, 0, 0, 0)),
            pl.BlockSpec((None, N_HEADS, dk), lambda b, c: (b, 0, 0)),
            pl.BlockSpec((None, 8, 128), lambda b, c: (b, 0, 0)),
        ],
        out_shape=[jax.ShapeDtypeStruct(ymix.shape, ymix.dtype),
                   jax.ShapeDtypeStruct((n_b, N_HEADS, dv, dk), F32),
                   jax.ShapeDtypeStruct((n_b, N_HEADS, dk), F32),
                   jax.ShapeDtypeStruct((n_b, 8, 128), F32)],
        scratch_shapes=[pltpu.VMEM((N_HEADS, dv, dk), F32),
                        pltpu.VMEM((8, dk), F32),
                        pltpu.VMEM((8, 128), F32)],
        input_output_aliases={6: 0},
        compiler_params=_cparams(("arbitrary", "arbitrary")),
        name="mlstm_prompt",
    )(proj, proj, proj, proj, gates, g_head_l, ymix)
    return y, c_out, n_out, m_out[:, :N_HEADS, 0]


def _mlstm_sample_kernel(seq, has_prev, q_ref, k_ref, v_ref, o_ref, gt_ref, gh_ref, c0_ref, n0_ref, m0_ref,
                         *rest):
    y_ref, c_out_ref, n_out_ref, m_out_ref = rest[-4:]
    rows = q_ref.shape[0]
    dk = c0_ref.shape[3]
    dv = c0_ref.shape[2]
    seg_of_row = lax.broadcasted_iota(jnp.int32, (rows, 1), 0) // seq
    gates = gt_ref[...]
    log_f = _log_sigmoid(gates)
    for h in range(N_HEADS):
        q = q_ref[:, h * dk:(h + 1) * dk]
        k = (k_ref[:, h * dk:(h + 1) * dk].astype(F32) * (dk ** -0.5)).astype(BF16)
        v = v_ref[:, h * dv:(h + 1) * dv]
        ig_all = gates[:, h:h + 1]
        lf_all = log_f[:, N_HEADS + h:N_HEADS + h + 1]
        hh = jnp.zeros((rows, dv), F32)
        for sgm in range(rows // seq):
            mine = seg_of_row == sgm
            ig = jnp.where(mine, ig_all, NEG_BIG)
            lf = jnp.where(mine, lf_all, 0.0)
            h_seg, c_new, n_new, m_new = _mlstm_segment(
                q, k, v, ig, lf, c0_ref[sgm, h], n0_ref[sgm, h:h + 1, :], m0_ref[sgm, h:h + 1, :])
            hh = jnp.where(mine, h_seg, hh)
            c_out_ref[sgm, h] = c_new
            n_out_ref[sgm, h:h + 1, :] = n_new
            m_out_ref[sgm, h:h + 1, :] = m_new
        y_ref[:, h * dv:(h + 1) * dv] = _head_out(
            hh, o_ref[:, h * dv:(h + 1) * dv], gh_ref[h:h + 1, :]).astype(y_ref.dtype)


def _mlstm_sample(proj_bm, gates_bm, g_head_l, state_c, state_n, state_m, layer, c_prev, seq, col_q, dk, dv):
    depth, n_b = state_c.shape[:2]
    grp = SAMPLE_GROUP
    rows = grp * seq
    qk_w = N_HEADS * dk
    v_w = N_HEADS * dv
    bq = col_q // qk_w
    bv = (col_q + 2 * qk_w) // v_w
    has_prev = c_prev is not None
    in_specs = [
        pl.BlockSpec((rows, qk_w), lambda i: (i, bq)),
        pl.BlockSpec((rows, qk_w), lambda i: (i, bq + 1)),
        pl.BlockSpec((rows, v_w), lambda i: (i, bv)),
        pl.BlockSpec((rows, v_w), lambda i: (i, bv + 1)),
        pl.BlockSpec((rows, gates_bm.shape[1]), lambda i: (i, 0)),
        pl.BlockSpec((N_HEADS, dv), lambda i: (0, 0)),
        pl.BlockSpec((None, grp, N_HEADS, dv, dk), lambda i: (layer, i, 0, 0, 0)),
        pl.BlockSpec((None, grp, N_HEADS, dk), lambda i: (layer, i, 0, 0)),
        pl.BlockSpec((None, grp, N_HEADS, 1), lambda i: (layer, i, 0, 0)),
    ]
    args = [proj_bm, proj_bm, proj_bm, proj_bm, gates_bm, g_head_l, state_c, state_n,
            state_m.reshape(depth, n_b, N_HEADS, 1)]
    aliases = {}
    if has_prev:
        in_specs.append(pl.BlockSpec(memory_space=pl.ANY))
        args.append(c_prev)
        aliases = {len(args) - 1: 1}
    y, c_out, n_out, m_out = pl.pallas_call(
        functools.partial(_mlstm_sample_kernel, seq, has_prev),
        grid=(n_b // grp,),
        in_specs=in_specs,
        out_specs=[
            pl.BlockSpec((rows, v_w), lambda i: (i, 0)),
            pl.BlockSpec((None, grp, N_HEADS, dv, dk), lambda i: (layer, i, 0, 0, 0)),
            pl.BlockSpec((grp, N_HEADS, dk), lambda i: (i, 0, 0)),
            pl.BlockSpec((grp, N_HEADS, 1), lambda i: (i, 0, 0)),
        ],
        out_shape=[jax.ShapeDtypeStruct((n_b * seq, v_w), BF16),
                   jax.ShapeDtypeStruct(state_c.shape, F32),
                   jax.ShapeDtypeStruct((n_b, N_HEADS, dk), F32),
                   jax.ShapeDtypeStruct((n_b, N_HEADS, 1), F32)],
        input_output_aliases=aliases,
        compiler_params=_cparams(("arbitrary",)),
        name="mlstm_sample",
    )(*args)
    return y, c_out, n_out, m_out[:, :, 0]


def _to_time_major(a, n_b, seq):
    return jnp.transpose(a.reshape(n_b, seq, -1), (1, 0, 2)).reshape(n_b * seq, -1)


def _to_batch_major(a, n_b, seq):
    return jnp.transpose(a.reshape(seq, n_b, -1), (1, 0, 2)).reshape(n_b * seq, -1)


def kernel(x_prompt, x_sample, state_pool, state_C, state_n, state_m, c_prompt, c_sample,
           w_ada, b_ada, g_norm, w_in, b_in, w_pool, s_pool, g_head, w_out, w1, w3, w2, g_final):
    n_bp, seq_p, d = x_prompt.shape
    n_bs, seq_s, _ = x_sample.shape
    depth = w_ada.shape[0]
    pw = s_pool.shape[-1]
    dv = g_head.shape[-1]
    dk = state_C.shape[-1]
    n_hist = state_pool.shape[2]
    rows_p = n_bp * seq_p
    rows_s = n_bs * seq_s
    n_main = pw + 2 * N_HEADS * dk + 2 * N_HEADS * dv

    xp0 = x_prompt.reshape(rows_p, d)
    xs0 = _to_time_major(x_sample, n_bs, seq_s)

    c_all = jnp.concatenate([c_sample, c_prompt, jnp.zeros((8 - n_bp, d), F32)], axis=0)
    mods = _ada(c_all, w_ada, b_ada)
    mods_p = mods[:, :, n_bs:n_bs + n_bp][:, :, :, None, :]

    b_in3 = b_in.reshape(depth, 1, b_in.shape[-1])
    w_in_t = jnp.transpose(w_in, (0, 2, 1))
    pool_p, c_p, n_p, m_p = [], [], [], []
    pool_s, n_s, m_s = [], [], []
    c_s_all = None

    def ffn(x, h, l, sub_layer, ffn_idx):
        act = _swiglu_up(h, w1, w3, l, ffn_idx)
        return _acc_resid(act, w2, (l, ffn_idx), x, mods, mods_p, l, sub_layer * N_MOD + 2,
                          rows_s, FFN_RES)

    x = None
    for l in range(depth):
        if l == 0:
            h, x = _normmod(xp0, xs0, 0, g_norm[l, 0], mods, mods_p, l, 0, seq_p, True)
        else:
            h, = _normmod(x, x, rows_p, g_norm[l, 0], mods, mods_p, l, 0, seq_p, False)
        x = ffn(x, h, l, 0, 0)

        h, = _normmod(x, x, rows_p, g_norm[l, 1], mods, mods_p, l, 1, seq_p, False)
        proj = _in_proj(h, w_in_t, b_in3, l, n_main)
        gates = _gate_proj(h, w_in_t, b_in3, l, n_main)

        ymix, hist_p = _pool_prompt(proj, w_pool[l], s_pool[l], n_bp, seq_p, d)
        ymix, cp, np_, mp = _mlstm_prompt(proj, gates, g_head[l], ymix, n_bp, seq_p, pw, dk, dv)
        hist_tm = jnp.transpose(state_pool[l], (1, 0, 2))
        ymix = _pool_sample(hist_tm, proj, w_pool[l], s_pool[l], ymix, rows_p)
        proj_bm = _to_batch_major(proj[rows_p:], n_bs, seq_s)
        gates_bm = _to_batch_major(gates[rows_p:], n_bs, seq_s)
        y_ms, c_s_all, ns, ms = _mlstm_sample(proj_bm, gates_bm, g_head[l], state_C, state_n, state_m,
                                              l, c_s_all, seq_s, pw, dk, dv)
        ymix = lax.dynamic_update_slice(ymix, _to_time_major(y_ms, n_bs, seq_s), (rows_p, pw))
        x = _acc_resid(ymix, w_out, (l,), x, mods, mods_p, l, 1 * N_MOD + 2, rows_s, 1.0)

        u_s = proj_bm[:, :pw].astype(F32).reshape(n_bs, seq_s, pw)
        pool_p.append(hist_p); c_p.append(cp); n_p.append(np_); m_p.append(mp)
        pool_s.append(jnp.concatenate([state_pool[l, :, seq_s:], u_s], axis=1))
        n_s.append(ns); m_s.append(ms)

        h, = _normmod(x, x, rows_p, g_norm[l, 2], mods, mods_p, l, 2, seq_p, False)
        x = ffn(x, h, l, 2, 1)

    y_prompt = _final_norm(x, g_final, 0, rows_p).reshape(n_bp, seq_p, d)
    y_sample = _to_batch_major(_final_norm(x, g_final, rows_p, rows_s), n_bs, seq_s).reshape(n_bs, seq_s, d)
    return (y_prompt, y_sample,
            jnp.stack(pool_p), jnp.stack(c_p), jnp.stack(n_p), jnp.stack(m_p),
            jnp.stack(pool_s), c_s_all, jnp.stack(n_s), jnp.stack(m_s))
```

```python
import functools

import jax
import jax.numpy as jnp
from jax import lax
from jax.experimental import pallas as pl
from jax.experimental.pallas import tpu as pltpu

F32 = jnp.float32
BF16 = jnp.bfloat16

EPS = 1e-6
FFN_RES = 0.5
POOL_WINDOWS = (2, 4, 8, 16)
N_HEADS = 4
N_SUB = 3
N_MOD = 3
PROMPT_CHUNK = 256
SAMPLE_GROUP = 4
NEG_BIG = -1e30

VMEM_LIMIT = 56 * 1024 * 1024

ROW_TILE = 256
MM_ROW_TILE = 2176
ACC_ROW_TILE = 2176
ACC_SUB_ROWS = 544
ACC_VMEM_LIMIT = 60 * 1024 * 1024
ACC_K_TILE = 1024
ACC_N_TILE = 1024


def _cparams(sem):
    return pltpu.CompilerParams(dimension_semantics=sem, vmem_limit_bytes=VMEM_LIMIT)


def _dot(a, b):
    return lax.dot_general(a, b, (((1,), (0,)), ((), ())), preferred_element_type=F32)


def _dot_nt(a, b):
    return lax.dot_general(a, b, (((1,), (1,)), ((), ())), preferred_element_type=F32)


def _dot_tn(a, b):
    return lax.dot_general(a, b, (((0,), (0,)), ((), ())), preferred_element_type=F32)


def _ada_kernel(c_ref, w_ref, b_ref, o_ref):
    c = c_ref[...]
    sc = (c * jax.nn.sigmoid(c)).astype(BF16)
    o_ref[...] = _dot(sc, w_ref[...]) + b_ref[...]


def _ada(c_all, w_ada, b_ada, tn=1024):
    depth, d, n = w_ada.shape
    rows = c_all.shape[0]
    per = d // tn
    return pl.pallas_call(
        _ada_kernel,
        grid=(depth, n // tn),
        in_specs=[
            pl.BlockSpec((rows, d), lambda l, j: (0, 0)),
            pl.BlockSpec((None, d, tn), lambda l, j: (l, 0, j)),
            pl.BlockSpec((None, 1, tn), lambda l, j: (l, 0, j)),
        ],
        out_specs=pl.BlockSpec((None, None, rows, tn), lambda l, j: (l, j // per, 0, j % per)),
        out_shape=jax.ShapeDtypeStruct((depth, n // d, rows, d), F32),
        compiler_params=_cparams(("arbitrary", "arbitrary")),
        name="ada_mods",
    )(c_all, w_ada, b_ada.reshape(depth, 1, n))


def _rms(x, g):
    return x * lax.rsqrt(jnp.mean(x * x, axis=-1, keepdims=True) + EPS) * g


NORM_CHUNK = 8


def _normmod_kernel(n_prompt_tiles, emit_x, xp_ref, xs_ref, g_ref, shp_ref, scp_ref, shs_ref, scs_ref,
                    h_ref, *rest):
    xo_ref = rest[0] if emit_x else None
    gm_ref = rest[-1]
    i = pl.program_id(0)
    n_bs = shs_ref.shape[0]
    n_chunks = h_ref.shape[0] // NORM_CHUNK

    def run(x_ref, mod_rows):
        def chunk(c, carry):
            r = pl.multiple_of(c * NORM_CHUNK, NORM_CHUNK)
            rows = pl.ds(r, NORM_CHUNK)
            x = x_ref[rows, :]
            rstd = lax.rsqrt(jnp.mean(x * x, axis=-1, keepdims=True) + EPS)
            gm, sh = mod_rows(r)
            h_ref[rows, :] = (x * rstd * gm + sh).astype(h_ref.dtype)
            if emit_x:
                xo_ref[rows, :] = x
            return carry

        lax.fori_loop(0, n_chunks, chunk, 0, unroll=4)

    @pl.when(i < n_prompt_tiles)
    def _():
        gm_ref[0:1, :] = g_ref[...] * (1.0 + scp_ref[...])
        run(xp_ref, lambda r: (gm_ref[0:1, :], shp_ref[...]))

    @pl.when(i >= n_prompt_tiles)
    def _():
        gm_ref[...] = g_ref[...] * (1.0 + scs_ref[...])

        def mod_rows(r):
            rb = pl.ds(pl.multiple_of(lax.rem(r, n_bs), NORM_CHUNK), NORM_CHUNK)
            return gm_ref[rb, :], shs_ref[rb, :]

        run(xs_ref, mod_rows)


def _normmod(xp, xs, xs_row0, g, mods, mods_p, layer, sub, seq, emit_x):
    d = xp.shape[1]
    tr = ROW_TILE if emit_x else 2 * ROW_TILE
    n_bp = mods_p.shape[2]
    n_bs = mods.shape[2] - 8
    rows_p = n_bp * seq
    rows_s = xs.shape[0] - xs_row0
    m = rows_p + rows_s
    n_pt = rows_p // tr
    tiles_per_seq = seq // tr
    s0 = xs_row0 // tr
    i_shift, i_scale = sub * N_MOD, sub * N_MOD + 1
    p_map = lambda which: (lambda i: (layer, which, jnp.minimum(i // tiles_per_seq, n_bp - 1), 0, 0))
    s_map = lambda which: (lambda i: (layer, which, 0, 0))
    out_specs = [pl.BlockSpec((tr, d), lambda i: (i, 0))]
    out_shape = [jax.ShapeDtypeStruct((m, d), BF16)]
    if emit_x:
        out_specs.append(pl.BlockSpec((tr, d), lambda i: (i, 0)))
        out_shape.append(jax.ShapeDtypeStruct((m, d), F32))
    return pl.pallas_call(
        functools.partial(_normmod_kernel, n_pt, emit_x),
        grid=(m // tr,),
        in_specs=[
            pl.BlockSpec((tr, d), lambda i: (jnp.minimum(i, n_pt - 1), 0)),
            pl.BlockSpec((tr, d), lambda i: (s0 + jnp.maximum(i - n_pt, 0), 0)),
            pl.BlockSpec((1, d), lambda i: (0, 0)),
            pl.BlockSpec((None, None, None, 1, d), p_map(i_shift)),
            pl.BlockSpec((None, None, None, 1, d), p_map(i_scale)),
            pl.BlockSpec((None, None, n_bs, d), s_map(i_shift)),
            pl.BlockSpec((None, None, n_bs, d), s_map(i_scale)),
        ],
        out_specs=out_specs,
        out_shape=out_shape,
        scratch_shapes=[pltpu.VMEM((n_bs, d), F32)],
        compiler_params=_cparams(("arbitrary",)),
        name="normmod",
    )(xp, xs, g.reshape(1, d), mods_p, mods_p, mods, mods)


def _final_norm_kernel(x_ref, g_ref, o_ref):
    def chunk(c, carry):
        rows = pl.ds(pl.multiple_of(c * NORM_CHUNK, NORM_CHUNK), NORM_CHUNK)
        o_ref[rows, :] = _rms(x_ref[rows, :], g_ref[...])
        return carry

    lax.fori_loop(0, o_ref.shape[0] // NORM_CHUNK, chunk, 0, unroll=4)


def _final_norm(x, g, row0, rows):
    d = x.shape[1]
    tr = ROW_TILE
    b0 = row0 // tr
    return pl.pallas_call(
        _final_norm_kernel,
        grid=(rows // tr,),
        in_specs=[pl.BlockSpec((tr, d), lambda i: (i + b0, 0)),
                  pl.BlockSpec((1, d), lambda i: (0, 0))],
        out_specs=pl.BlockSpec((tr, d), lambda i: (i, 0)),
        out_shape=jax.ShapeDtypeStruct((rows, d), F32),
        compiler_params=_cparams(("arbitrary",)),
        name="final_norm",
    )(x, g.reshape(1, d))


def _swiglu_up_kernel(n_full, tail, x_ref, w1_ref, w3_ref, o_ref):
    j = pl.program_id(1)

    def body(cols):
        x = x_ref[...]
        a = _dot(x, w1_ref[:, 0:cols])
        b = _dot(x, w3_ref[:, 0:cols])
        o_ref[:, 0:cols] = (a * jax.nn.sigmoid(a) * b).astype(o_ref.dtype)

    if tail == 0:
        body(o_ref.shape[1])
    else:
        @pl.when(j < n_full)
        def _():
            body(o_ref.shape[1])

        @pl.when(j == n_full)
        def _():
            body(tail)


def _swiglu_up(h, w1, w3, layer, sub, tn=256):
    m, d = h.shape
    f = w1.shape[-1]
    tm = MM_ROW_TILE
    n_full, tail = divmod(f, tn)
    w_spec = pl.BlockSpec((None, None, d, tn), lambda i, j: (layer, sub, 0, j))
    return pl.pallas_call(
        functools.partial(_swiglu_up_kernel, n_full, tail),
        grid=(m // tm, pl.cdiv(f, tn)),
        in_specs=[pl.BlockSpec((tm, d), lambda i, j: (i, 0), pipeline_mode=pl.Buffered(1)),
                  w_spec, w_spec],
        out_specs=pl.BlockSpec((tm, tn), lambda i, j: (i, j)),
        out_shape=jax.ShapeDtypeStruct((m, f), BF16),
        compiler_params=_cparams(("arbitrary", "arbitrary")),
        name="swiglu_up",
    )(h, w1, w3)


def _proj_kernel(x_ref, w_ref, b_ref, o_ref):
    o_ref[...] = (_dot_nt(x_ref[...], w_ref[...]) + b_ref[...]).astype(o_ref.dtype)


def _in_proj(h, w_in_t, b_in3, layer, n_main, tn=512):
    m, d = h.shape
    tm = MM_ROW_TILE
    return pl.pallas_call(
        _proj_kernel,
        grid=(m // tm, n_main // tn),
        in_specs=[pl.BlockSpec((tm, d), lambda i, j: (i, 0), pipeline_mode=pl.Buffered(1)),
                  pl.BlockSpec((None, tn, d), lambda i, j: (layer, j, 0)),
                  pl.BlockSpec((None, 1, tn), lambda i, j: (layer, 0, j))],
        out_specs=pl.BlockSpec((tm, tn), lambda i, j: (i, j)),
        out_shape=jax.ShapeDtypeStruct((m, n_main), BF16),
        compiler_params=_cparams(("arbitrary", "arbitrary")),
        name="in_proj",
    )(h, w_in_t, b_in3)


def _gate_proj(h, w_in_t, b_in3, layer, n_main, tn=128):
    m, d = h.shape
    tm = MM_ROW_TILE
    jb = n_main // tn
    return pl.pallas_call(
        _proj_kernel,
        grid=(m // tm,),
        in_specs=[pl.BlockSpec((tm, d), lambda i: (i, 0)),
                  pl.BlockSpec((None, tn, d), lambda i: (layer, jb, 0)),
                  pl.BlockSpec((None, 1, tn), lambda i: (layer, 0, jb))],
        out_specs=pl.BlockSpec((tm, tn), lambda i: (i, 0)),
        out_shape=jax.ShapeDtypeStruct((m, tn), F32),
        compiler_params=_cparams(("arbitrary",)),
        name="gate_proj",
    )(h, w_in_t, b_in3)


def _gate_segments(tm, n_tiles, n_bp, seq, n_bs):
    rows_p = n_bp * seq
    tiles = []
    for t in range(n_tiles):
        segs, r = [], t * tm
        while r < (t + 1) * tm:
            if r < rows_p:
                b = r // seq
                end = min((b + 1) * seq, (t + 1) * tm)
                segs.append((r - t * tm, end - t * tm, b))
            else:
                end = r + n_bs
                assert (r - rows_p) % n_bs == 0 and end <= (t + 1) * tm
                segs.append((r - t * tm, end - t * tm, None))
            r = end
        tiles.append(segs)
    return tiles


def _acc_resid_kernel(segments, k_last_valid, res_scale, a_ref, w_ref, x_ref, gp_ref, gs_ref, o_ref):
    i = pl.program_id(0)
    k = pl.program_id(2)
    nk = pl.num_programs(2)
    tm, tk = a_ref.shape

    def sweep(kv, first):
        for r0 in range(0, tm, ACC_SUB_ROWS):
            rs = slice(r0, r0 + ACC_SUB_ROWS)
            part = _dot(a_ref[rs, 0:kv], w_ref[0:kv, :])
            if first:
                o_ref[rs, :] = part
            else:
                o_ref[rs, :] += part

    @pl.when(k == 0)
    def _():
        sweep(tk, True)

    @pl.when(jnp.logical_and(k > 0, k < nk - 1))
    def _():
        sweep(tk, False)

    @pl.when(k == nk - 1)
    def _():
        sweep(k_last_valid, False)
        for t, segs in enumerate(segments):
            @pl.when(i == t)
            def _():
                for r0, r1, b in segs:
                    gate = gs_ref[...] if b is None else gp_ref[b]
                    o_ref[r0:r1, :] = x_ref[r0:r1, :] + (res_scale * gate) * o_ref[r0:r1, :]


def _acc_resid(a, w_full, w_index, x, mods, mods_p, layer, which, sample_rows, res_scale):
    m, kdim = a.shape
    d = x.shape[1]
    tm, tk, tn = ACC_ROW_TILE, ACC_K_TILE, ACC_N_TILE
    n_bp = mods_p.shape[2]
    n_bs = mods.shape[2] - 8
    seq = (m - sample_rows) // n_bp
    assert m % tm == 0 and tm % ACC_SUB_ROWS == 0
    nk = pl.cdiv(kdim, tk)
    assert nk >= 3
    k_last_valid = kdim - (nk - 1) * tk
    lead = (None,) * len(w_index)
    segments = _gate_segments(tm, m // tm, n_bp, seq, n_bs)
    return pl.pallas_call(
        functools.partial(_acc_resid_kernel, segments, k_last_valid, res_scale),
        grid=(m // tm, d // tn, nk),
        in_specs=[
            pl.BlockSpec((tm, tk), lambda i, j, k: (i, k)),
            pl.BlockSpec(lead + (tk, tn), lambda i, j, k: tuple(w_index) + (k, j)),
            pl.BlockSpec((tm, tn), lambda i, j, k: (i, j)),
            pl.BlockSpec((None, None, n_bp, 1, tn), lambda i, j, k: (layer, which, 0, 0, j)),
            pl.BlockSpec((None, None, n_bs, tn), lambda i, j, k: (layer, which, 0, j)),
        ],
        out_specs=pl.BlockSpec((tm, tn), lambda i, j, k: (i, j)),
        out_shape=jax.ShapeDtypeStruct((m, d), F32),
        compiler_params=pltpu.CompilerParams(
            dimension_semantics=("arbitrary", "arbitrary", "arbitrary"),
            vmem_limit_bytes=ACC_VMEM_LIMIT),
        name="acc_resid",
    )(a, w_full, x, mods_p, mods)


HALO = 16


def _pool_prompt_kernel(start, u_ref, wp_ref, sp_ref, y_ref, hist_ref, z_ref):
    t = pl.program_id(1)
    nt = pl.num_programs(1)
    tt = u_ref.shape[0]
    group = wp_ref.shape[1]

    @pl.when(t == 0)
    def _():
        z_ref[0:HALO, :] = jnp.zeros((HALO, z_ref.shape[1]), F32)

    z_ref[HALO:HALO + tt, :] = u_ref[...].astype(F32)

    pos = start + t * tt + lax.broadcasted_iota(jnp.int32, (tt, 1), 0)
    for g, w in enumerate(POOL_WINDOWS):
        cols = slice(g * group, (g + 1) * group)
        cur = z_ref[HALO:HALO + tt, cols]
        acc = cur
        for j in range(1, w):
            acc = acc + z_ref[HALO - j:HALO - j + tt, cols]
        cnt = jnp.minimum(w, pos + 1).astype(F32)
        pooled = acc / cnt - cur
        y = _dot(pooled.astype(BF16), wp_ref[g]) * sp_ref[:, cols]
        y_ref[:, cols] = y.astype(y_ref.dtype)

    @pl.when(t == nt - 1)
    def _():
        hist_ref[...] = z_ref[HALO + tt - (HALO - 1):HALO + tt, :]

    z_ref[0:HALO, :] = z_ref[tt:tt + HALO, :]


def _pool_prompt(proj, w_pool_l, s_pool_l, n_b, seq, d, tt=256):
    pw = s_pool_l.shape[-1]
    ntt = seq // tt
    n_groups, group, _ = w_pool_l.shape
    return pl.pallas_call(
        functools.partial(_pool_prompt_kernel, 0),
        grid=(n_b, ntt),
        in_specs=[
            pl.BlockSpec((tt, pw), lambda b, t: (b * ntt + t, 0)),
            pl.BlockSpec((n_groups, group, group), lambda b, t: (0, 0, 0)),
            pl.BlockSpec((1, pw), lambda b, t: (0, 0)),
        ],
        out_specs=[
            pl.BlockSpec((tt, pw), lambda b, t: (b * ntt + t, 0)),
            pl.BlockSpec((None, HALO - 1, pw), lambda b, t: (b, 0, 0)),
        ],
        out_shape=[jax.ShapeDtypeStruct((proj.shape[0], d), BF16),
                   jax.ShapeDtypeStruct((n_b, HALO - 1, pw), F32)],
        scratch_shapes=[pltpu.VMEM((HALO + tt, pw), F32)],
        compiler_params=_cparams(("arbitrary", "arbitrary")),
        name="pool_prompt",
    )(proj, w_pool_l, s_pool_l.reshape(1, pw))


def _pool_sample_kernel(hist_ref, u_ref, wp_ref, sp_ref, *rest):
    y_ref, nh_ref = rest[-2:]
    n_hist, n_b, _ = hist_ref.shape
    n_t = u_ref.shape[0] // n_b
    g = pl.program_id(0)

    def z(r):
        if r < n_hist:
            return hist_ref[r]
        return u_ref[(r - n_hist) * n_b:(r - n_hist + 1) * n_b, :].astype(F32)

    for r in range(n_hist):
        nh_ref[r] = z(r + n_t)

    for t in range(n_t):
        cur = z(n_hist + t)
        run = cur
        sums = []
        for j in range(1, POOL_WINDOWS[-1]):
            run = run + z(n_hist + t - j)
            if j + 1 in POOL_WINDOWS:
                sums.append(run)
        pooled = jnp.zeros_like(cur)
        for gi, w in enumerate(POOL_WINDOWS):
            pooled = jnp.where(g == gi, sums[gi] / float(w) - cur, pooled)
        y = _dot(pooled.astype(BF16), wp_ref[...]) * sp_ref[...]
        y_ref[t * n_b:(t + 1) * n_b, :] = y.astype(y_ref.dtype)


def _pool_sample(state_tm, layer, hist_prev, proj, w_pool_l, s_pool_l, ymix, row0):
    depth, n_hist, n_b, pw = state_tm.shape
    n_groups, group, _ = w_pool_l.shape
    rows = proj.shape[0] - row0
    rb = row0 // rows
    args = [state_tm, proj, w_pool_l, s_pool_l.reshape(1, pw), ymix]
    in_specs = [
        pl.BlockSpec((None, n_hist, n_b, group), lambda g: (layer, 0, 0, g)),
        pl.BlockSpec((rows, group), lambda g: (rb, g)),
        pl.BlockSpec((None, group, group), lambda g: (g, 0, 0)),
        pl.BlockSpec((1, group), lambda g: (0, g)),
        pl.BlockSpec(memory_space=pl.ANY),
    ]
    aliases = {4: 0}
    if hist_prev is not None:
        args.append(hist_prev)
        in_specs.append(pl.BlockSpec(memory_space=pl.ANY))
        aliases[5] = 1
    return pl.pallas_call(
        _pool_sample_kernel,
        grid=(n_groups,),
        in_specs=in_specs,
        out_specs=[pl.BlockSpec((rows, group), lambda g: (rb, g)),
                   pl.BlockSpec((None, n_hist, n_b, group), lambda g: (layer, 0, 0, g))],
        out_shape=[jax.ShapeDtypeStruct(ymix.shape, ymix.dtype),
                   jax.ShapeDtypeStruct(state_tm.shape, F32)],
        input_output_aliases=aliases,
        compiler_params=_cparams(("arbitrary",)),
        name="pool_sample",
    )(*args)


def _mlstm_segment(q, k, v, ig, lf, c_state, n_state, m_state):
    r = q.shape[0]
    row = lax.broadcasted_iota(jnp.int32, (r, r), 0)
    col = lax.broadcasted_iota(jnp.int32, (r, r), 1)
    causal = col <= row
    lf_rows = jnp.sum(jnp.where(row == col, lf, 0.0), axis=0, keepdims=True)
    ig_rows = jnp.sum(jnp.where(row == col, ig, 0.0), axis=0, keepdims=True)
    b_col = jnp.sum(jnp.where(causal, lf_rows, 0.0), axis=1, keepdims=True)
    b_rows = jnp.sum(jnp.where(row <= col, lf, 0.0), axis=0, keepdims=True)
    dmat = jnp.where(causal, b_col - b_rows + ig_rows, -jnp.inf)
    inter = b_col + m_state
    m_tok = jnp.maximum(inter, jnp.max(dmat, axis=-1, keepdims=True))
    w_intra = jnp.exp(dmat - m_tok)
    w_inter = jnp.exp(inter - m_tok)
    s = _dot_nt(q, k) * w_intra
    num = _dot(s.astype(BF16), v) + w_inter * _dot_nt(q, c_state)
    qn = jnp.sum(q.astype(F32) * n_state, axis=-1, keepdims=True)
    den = jnp.sum(s, axis=-1, keepdims=True) + w_inter * qn
    h = num * (1.0 / jnp.maximum(jnp.abs(den), jnp.exp(-m_tok)))
    b_last = jnp.sum(lf, axis=0, keepdims=True)
    dec = b_last - b_col + ig
    m_new = jnp.maximum(b_last + m_state, jnp.max(dec, axis=0, keepdims=True))
    ws = jnp.exp(dec - m_new)
    wc = jnp.exp(b_last + m_state - m_new)
    kf = k.astype(F32)
    c_new = wc * c_state + _dot_tn((ws * v.astype(F32)).astype(BF16), k)
    n_new = wc * n_state + jnp.sum(ws * kf, axis=0, keepdims=True)
    return h, c_new, n_new, m_new


def _head_out(h, o, g_head):
    hn = h * lax.rsqrt(jnp.mean(h * h, axis=-1, keepdims=True) + EPS) * g_head
    return hn * jax.nn.sigmoid(o.astype(F32))


def _log_sigmoid(x):
    return jnp.minimum(x, 0.0) - jnp.log1p(jnp.exp(-jnp.abs(x)))


def _mlstm_prompt_kernel(q_ref, k_ref, v_ref, o_ref, gt_ref, gh_ref, ymix_in_ref,
                         y_ref, c_out_ref, n_out_ref, m_out_ref, c_s, n_s, m_s):
    del ymix_in_ref
    c = pl.program_id(1)
    nc = pl.num_programs(1)
    dk = c_s.shape[2]
    dv = c_s.shape[1]

    @pl.when(c == 0)
    def _():
        c_s[...] = jnp.zeros(c_s.shape, F32)
        n_s[...] = jnp.zeros(n_s.shape, F32)
        m_s[...] = jnp.zeros(m_s.shape, F32)

    gates = gt_ref[...]
    log_f = _log_sigmoid(gates)
    for h in range(N_HEADS):
        q = q_ref[:, h * dk:(h + 1) * dk]
        k = (k_ref[:, h * dk:(h + 1) * dk].astype(F32) * (dk ** -0.5)).astype(BF16)
        v = v_ref[:, h * dv:(h + 1) * dv]
        ig = gates[:, h:h + 1]
        lf = log_f[:, N_HEADS + h:N_HEADS + h + 1]
        hh, c_new, n_new, m_new = _mlstm_segment(
            q, k, v, ig, lf, c_s[h], n_s[h:h + 1, :], m_s[h:h + 1, 0:1])
        c_s[h] = c_new
        n_s[h:h + 1, :] = n_new
        m_s[h:h + 1, :] = jnp.broadcast_to(m_new, (1, m_s.shape[1]))
        y_ref[:, h * dv:(h + 1) * dv] = _head_out(
            hh, o_ref[:, h * dv:(h + 1) * dv], gh_ref[h:h + 1, :]).astype(y_ref.dtype)

    @pl.when(c == nc - 1)
    def _():
        c_out_ref[...] = c_s[...]
        n_out_ref[...] = n_s[0:N_HEADS, :]
        m_out_ref[...] = m_s[...]


def _mlstm_prompt(proj, gates, g_head_l, ymix, n_b, seq, col_q, dk, dv):
    lc = PROMPT_CHUNK
    nch = seq // lc
    qk_w = N_HEADS * dk
    v_w = N_HEADS * dv
    row = lambda b, c: b * nch + c
    assert col_q % qk_w == 0 and (col_q + 2 * qk_w) % v_w == 0
    bq = col_q // qk_w
    bv = (col_q + 2 * qk_w) // v_w
    y, c_out, n_out, m_out = pl.pallas_call(
        _mlstm_prompt_kernel,
        grid=(n_b, nch),
        in_specs=[
            pl.BlockSpec((lc, qk_w), lambda b, c: (row(b, c), bq)),
            pl.BlockSpec((lc, qk_w), lambda b, c: (row(b, c), bq + 1)),
            pl.BlockSpec((lc, v_w), lambda b, c: (row(b, c), bv)),
            pl.BlockSpec((lc, v_w), lambda b, c: (row(b, c), bv + 1)),
            pl.BlockSpec((lc, gates.shape[1]), lambda b, c: (row(b, c), 0)),
            pl.BlockSpec((N_HEADS, dv), lambda b, c: (0, 0)),
            pl.BlockSpec(memory_space=pl.ANY),
        ],
        out_specs=[
            pl.BlockSpec((lc, v_w), lambda b, c: (row(b, c), 1)),
            pl.BlockSpec((None, N_HEADS, dv, dk), lambda b, c: (b, 0, 0, 0)),
            pl.BlockSpec((None, N_HEADS, dk), lambda b, c: (b, 0, 0)),
            pl.BlockSpec((None, 8, 128), lambda b, c: (b, 0, 0)),
        ],
        out_shape=[jax.ShapeDtypeStruct(ymix.shape, ymix.dtype),
                   jax.ShapeDtypeStruct((n_b, N_HEADS, dv, dk), F32),
                   jax.ShapeDtypeStruct((n_b, N_HEADS, dk), F32),
                   jax.ShapeDtypeStruct((n_b, 8, 128), F32)],
        scratch_shapes=[pltpu.VMEM((N_HEADS, dv, dk), F32),
                        pltpu.VMEM((8, dk), F32),
                        pltpu.VMEM((8, 128), F32)],
        input_output_aliases={6: 0},
        compiler_params=_cparams(("arbitrary", "arbitrary")),
        name="mlstm_prompt",
    )(proj, proj, proj, proj, gates, g_head_l, ymix)
    return y, c_out, n_out, m_out[:, :N_HEADS, 0]


def _mlstm_sample_kernel(seq, has_prev, q_ref, k_ref, v_ref, o_ref, gt_ref, gh_ref, c0_ref, n0_ref, m0_ref,
                         *rest):
    y_ref, c_out_ref, n_out_ref, m_out_ref = rest[-4:]
    rows = q_ref.shape[0]
    dk = c0_ref.shape[3]
    dv = c0_ref.shape[2]
    seg_of_row = lax.broadcasted_iota(jnp.int32, (rows, 1), 0) // seq
    gates = gt_ref[...]
    log_f = _log_sigmoid(gates)
    for h in range(N_HEADS):
        q = q_ref[:, h * dk:(h + 1) * dk]
        k = (k_ref[:, h * dk:(h + 1) * dk].astype(F32) * (dk ** -0.5)).astype(BF16)
        v = v_ref[:, h * dv:(h + 1) * dv]
        ig_all = gates[:, h:h + 1]
        lf_all = log_f[:, N_HEADS + h:N_HEADS + h + 1]
        hh = jnp.zeros((rows, dv), F32)
        for sgm in range(rows // seq):
            mine = seg_of_row == sgm
            ig = jnp.where(mine, ig_all, NEG_BIG)
            lf = jnp.where(mine, lf_all, 0.0)
            h_seg, c_new, n_new, m_new = _mlstm_segment(
                q, k, v, ig, lf, c0_ref[sgm, h], n0_ref[sgm, h:h + 1, :], m0_ref[sgm, h:h + 1, :])
            hh = jnp.where(mine, h_seg, hh)
            c_out_ref[sgm, h] = c_new
            n_out_ref[sgm, h:h + 1, :] = n_new
            m_out_ref[sgm, h:h + 1, :] = m_new
        y_ref[:, h * dv:(h + 1) * dv] = _head_out(
            hh, o_ref[:, h * dv:(h + 1) * dv], gh_ref[h:h + 1, :]).astype(y_ref.dtype)


def _mlstm_sample(proj_bm, gates_bm, g_head_l, state_c, state_n, state_m, layer, c_prev, seq, col_q, dk, dv):
    depth, n_b = state_c.shape[:2]
    grp = SAMPLE_GROUP
    rows = grp * seq
    qk_w = N_HEADS * dk
    v_w = N_HEADS * dv
    bq = col_q // qk_w
    bv = (col_q + 2 * qk_w) // v_w
    has_prev = c_prev is not None
    in_specs = [
        pl.BlockSpec((rows, qk_w), lambda i: (i, bq)),
        pl.BlockSpec((rows, qk_w), lambda i: (i, bq + 1)),
        pl.BlockSpec((rows, v_w), lambda i: (i, bv)),
        pl.BlockSpec((rows, v_w), lambda i: (i, bv + 1)),
        pl.BlockSpec((rows, gates_bm.shape[1]), lambda i: (i, 0)),
        pl.BlockSpec((N_HEADS, dv), lambda i: (0, 0)),
        pl.BlockSpec((None, grp, N_HEADS, dv, dk), lambda i: (layer, i, 0, 0, 0)),
        pl.BlockSpec((None, grp, N_HEADS, dk), lambda i: (layer, i, 0, 0)),
        pl.BlockSpec((None, grp, N_HEADS, 1), lambda i: (layer, i, 0, 0)),
    ]
    args = [proj_bm, proj_bm, proj_bm, proj_bm, gates_bm, g_head_l, state_c, state_n,
            state_m.reshape(depth, n_b, N_HEADS, 1)]
    aliases = {}
    if has_prev:
        in_specs.append(pl.BlockSpec(memory_space=pl.ANY))
        args.append(c_prev)
        aliases = {len(args) - 1: 1}
    y, c_out, n_out, m_out = pl.pallas_call(
        functools.partial(_mlstm_sample_kernel, seq, has_prev),
        grid=(n_b // grp,),
        in_specs=in_specs,
        out_specs=[
            pl.BlockSpec((rows, v_w), lambda i: (i, 0)),
            pl.BlockSpec((None, grp, N_HEADS, dv, dk), lambda i: (layer, i, 0, 0, 0)),
            pl.BlockSpec((grp, N_HEADS, dk), lambda i: (i, 0, 0)),
            pl.BlockSpec((grp, N_HEADS, 1), lambda i: (i, 0, 0)),
        ],
        out_shape=[jax.ShapeDtypeStruct((n_b * seq, v_w), BF16),
                   jax.ShapeDtypeStruct(state_c.shape, F32),
                   jax.ShapeDtypeStruct((n_b, N_HEADS, dk), F32),
                   jax.ShapeDtypeStruct((n_b, N_HEADS, 1), F32)],
        input_output_aliases=aliases,
        compiler_params=_cparams(("arbitrary",)),
        name="mlstm_sample",
    )(*args)
    return y, c_out, n_out, m_out[:, :, 0]


def _to_time_major(a, n_b, seq):
    return jnp.transpose(a.reshape(n_b, seq, -1), (1, 0, 2)).reshape(n_b * seq, -1)


def _to_batch_major(a, n_b, seq):
    return jnp.transpose(a.reshape(seq, n_b, -1), (1, 0, 2)).reshape(n_b * seq, -1)


def kernel(x_prompt, x_sample, state_pool, state_C, state_n, state_m, c_prompt, c_sample,
           w_ada, b_ada, g_norm, w_in, b_in, w_pool, s_pool, g_head, w_out, w1, w3, w2, g_final):
    n_bp, seq_p, d = x_prompt.shape
    n_bs, seq_s, _ = x_sample.shape
    depth = w_ada.shape[0]
    pw = s_pool.shape[-1]
    dv = g_head.shape[-1]
    dk = state_C.shape[-1]
    n_hist = state_pool.shape[2]
    rows_p = n_bp * seq_p
    rows_s = n_bs * seq_s
    n_main = pw + 2 * N_HEADS * dk + 2 * N_HEADS * dv

    xp0 = x_prompt.reshape(rows_p, d)
    xs0 = _to_time_major(x_sample, n_bs, seq_s)

    c_all = jnp.concatenate([c_sample, c_prompt, jnp.zeros((8 - n_bp, d), F32)], axis=0)
    mods = _ada(c_all, w_ada, b_ada)
    mods_p = mods[:, :, n_bs:n_bs + n_bp][:, :, :, None, :]

    b_in3 = b_in.reshape(depth, 1, b_in.shape[-1])
    w_in_t = jnp.transpose(w_in, (0, 2, 1))
    pool_p, c_p, n_p, m_p = [], [], [], []
    n_s, m_s = [], []
    c_s_all = None
    hist_s_tm = None
    state_pool_tm = jnp.transpose(state_pool, (0, 2, 1, 3))

    def ffn(x, h, l, sub_layer, ffn_idx):
        act = _swiglu_up(h, w1, w3, l, ffn_idx)
        return _acc_resid(act, w2, (l, ffn_idx), x, mods, mods_p, l, sub_layer * N_MOD + 2,
                          rows_s, FFN_RES)

    x = None
    for l in range(depth):
        if l == 0:
            h, x = _normmod(xp0, xs0, 0, g_norm[l, 0], mods, mods_p, l, 0, seq_p, True)
        else:
            h, = _normmod(x, x, rows_p, g_norm[l, 0], mods, mods_p, l, 0, seq_p, False)
        x = ffn(x, h, l, 0, 0)

        h, = _normmod(x, x, rows_p, g_norm[l, 1], mods, mods_p, l, 1, seq_p, False)
        proj = _in_proj(h, w_in_t, b_in3, l, n_main)
        gates = _gate_proj(h, w_in_t, b_in3, l, n_main)

        ymix, hist_p = _pool_prompt(proj, w_pool[l], s_pool[l], n_bp, seq_p, d)
        ymix, cp, np_, mp = _mlstm_prompt(proj, gates, g_head[l], ymix, n_bp, seq_p, pw, dk, dv)
        ymix, hist_s_tm = _pool_sample(state_pool_tm, l, hist_s_tm, proj, w_pool[l], s_pool[l], ymix, rows_p)
        proj_bm = _to_batch_major(proj[rows_p:], n_bs, seq_s)
        gates_bm = _to_batch_major(gates[rows_p:], n_bs, seq_s)
        y_ms, c_s_all, ns, ms = _mlstm_sample(proj_bm, gates_bm, g_head[l], state_C, state_n, state_m,
                                              l, c_s_all, seq_s, pw, dk, dv)
        ymix = lax.dynamic_update_slice(ymix, _to_time_major(y_ms, n_bs, seq_s), (rows_p, pw))
        x = _acc_resid(ymix, w_out, (l,), x, mods, mods_p, l, 1 * N_MOD + 2, rows_s, 1.0)

        pool_p.append(hist_p); c_p.append(cp); n_p.append(np_); m_p.append(mp)
        n_s.append(ns); m_s.append(ms)

        h, = _normmod(x, x, rows_p, g_norm[l, 2], mods, mods_p, l, 2, seq_p, False)
        x = ffn(x, h, l, 2, 1)

    y_prompt = _final_norm(x, g_final, 0, rows_p).reshape(n_bp, seq_p, d)
    y_sample = _to_batch_major(_final_norm(x, g_final, rows_p, rows_s), n_bs, seq_s).reshape(n_bs, seq_s, d)
    return (y_prompt, y_sample,
            jnp.stack(pool_p), jnp.stack(c_p), jnp.stack(n_p), jnp.stack(m_p),
            jnp.transpose(hist_s_tm, (0, 2, 1, 3)), c_s_all, jnp.stack(n_s), jnp.stack(m_s))
```

```python
import functools

import jax
import jax.numpy as jnp
from jax import lax
from jax.experimental import pallas as pl
from jax.experimental.pallas import tpu as pltpu

F32 = jnp.float32
BF16 = jnp.bfloat16

EPS = 1e-6
FFN_RES = 0.5
POOL_WINDOWS = (2, 4, 8, 16)
N_HEADS = 4
N_SUB = 3
N_MOD = 3
PROMPT_CHUNK = 256
SAMPLE_GROUP = 4
NEG_BIG = -1e30

VMEM_LIMIT = 56 * 1024 * 1024

ROW_TILE = 256
MM_ROW_TILE = 2176
ACC_ROW_TILE = 2176
ACC_SUB_ROWS = 544
ACC_VMEM_LIMIT = 60 * 1024 * 1024
ACC_K_TILE = 1024
ACC_N_TILE = 1024


def _cparams(sem):
    return pltpu.CompilerParams(dimension_semantics=sem, vmem_limit_bytes=VMEM_LIMIT)


def _dot(a, b):
    return lax.dot_general(a, b, (((1,), (0,)), ((), ())), preferred_element_type=F32)


def _dot_nt(a, b):
    return lax.dot_general(a, b, (((1,), (1,)), ((), ())), preferred_element_type=F32)


def _dot_tn(a, b):
    return lax.dot_general(a, b, (((0,), (0,)), ((), ())), preferred_element_type=F32)


def _ada_kernel(c_ref, w_ref, b_ref, o_ref):
    c = c_ref[...]
    sc = (c * jax.nn.sigmoid(c)).astype(BF16)
    o_ref[...] = _dot(sc, w_ref[...]) + b_ref[...]


def _ada_head(c_all, w_ada, b_ada3, n_slabs, tn=1024):
    depth, d, n = w_ada.shape
    rows = c_all.shape[0]
    per = d // tn
    return pl.pallas_call(
        _ada_kernel,
        grid=(n_slabs * per,),
        in_specs=[
            pl.BlockSpec((rows, d), lambda t: (0, 0)),
            pl.BlockSpec((None, d, tn), lambda t: (0, 0, t)),
            pl.BlockSpec((None, 1, tn), lambda t: (0, 0, t)),
        ],
        out_specs=pl.BlockSpec((None, None, rows, tn), lambda t: (0, t // per, 0, t % per)),
        out_shape=jax.ShapeDtypeStruct((depth, n // d, rows, d), F32),
        compiler_params=_cparams(("arbitrary",)),
        name="ada_mods",
    )(c_all, w_ada, b_ada3)


def _rms(x, g):
    return x * lax.rsqrt(jnp.mean(x * x, axis=-1, keepdims=True) + EPS) * g


NORM_CHUNK = 8


def _normmod_kernel(n_prompt_tiles, emit_x, xp_ref, xs_ref, g_ref, shp_ref, scp_ref, shs_ref, scs_ref,
                    h_ref, *rest):
    xo_ref = rest[0] if emit_x else None
    gm_ref = rest[-1]
    i = pl.program_id(0)
    n_bs = shs_ref.shape[0]
    n_chunks = h_ref.shape[0] // NORM_CHUNK

    def run(x_ref, mod_rows):
        def chunk(c, carry):
            r = pl.multiple_of(c * NORM_CHUNK, NORM_CHUNK)
            rows = pl.ds(r, NORM_CHUNK)
            x = x_ref[rows, :]
            rstd = lax.rsqrt(jnp.mean(x * x, axis=-1, keepdims=True) + EPS)
            gm, sh = mod_rows(r)
            h_ref[rows, :] = (x * rstd * gm + sh).astype(h_ref.dtype)
            if emit_x:
                xo_ref[rows, :] = x
            return carry

        lax.fori_loop(0, n_chunks, chunk, 0, unroll=4)

    @pl.when(i < n_prompt_tiles)
    def _():
        gm_ref[0:1, :] = g_ref[...] * (1.0 + scp_ref[...])
        run(xp_ref, lambda r: (gm_ref[0:1, :], shp_ref[...]))

    @pl.when(i >= n_prompt_tiles)
    def _():
        gm_ref[...] = g_ref[...] * (1.0 + scs_ref[...])

        def mod_rows(r):
            rb = pl.ds(pl.multiple_of(lax.rem(r, n_bs), NORM_CHUNK), NORM_CHUNK)
            return gm_ref[rb, :], shs_ref[rb, :]

        run(xs_ref, mod_rows)


def _normmod(xp, xs, xs_row0, g, mods, mods_p, layer, sub, seq, emit_x):
    d = xp.shape[1]
    tr = ROW_TILE if emit_x else 2 * ROW_TILE
    n_bp = mods_p.shape[2]
    n_bs = mods.shape[2] - 8
    rows_p = n_bp * seq
    rows_s = xs.shape[0] - xs_row0
    m = rows_p + rows_s
    n_pt = rows_p // tr
    tiles_per_seq = seq // tr
    s0 = xs_row0 // tr
    i_shift, i_scale = sub * N_MOD, sub * N_MOD + 1
    p_map = lambda which: (lambda i: (layer, which, jnp.minimum(i // tiles_per_seq, n_bp - 1), 0, 0))
    s_map = lambda which: (lambda i: (layer, which, 0, 0))
    out_specs = [pl.BlockSpec((tr, d), lambda i: (i, 0))]
    out_shape = [jax.ShapeDtypeStruct((m, d), BF16)]
    if emit_x:
        out_specs.append(pl.BlockSpec((tr, d), lambda i: (i, 0)))
        out_shape.append(jax.ShapeDtypeStruct((m, d), F32))
    return pl.pallas_call(
        functools.partial(_normmod_kernel, n_pt, emit_x),
        grid=(m // tr,),
        in_specs=[
            pl.BlockSpec((tr, d), lambda i: (jnp.minimum(i, n_pt - 1), 0)),
            pl.BlockSpec((tr, d), lambda i: (s0 + jnp.maximum(i - n_pt, 0), 0)),
            pl.BlockSpec((1, d), lambda i: (0, 0)),
            pl.BlockSpec((None, None, None, 1, d), p_map(i_shift)),
            pl.BlockSpec((None, None, None, 1, d), p_map(i_scale)),
            pl.BlockSpec((None, None, n_bs, d), s_map(i_shift)),
            pl.BlockSpec((None, None, n_bs, d), s_map(i_scale)),
        ],
        out_specs=out_specs,
        out_shape=out_shape,
        scratch_shapes=[pltpu.VMEM((n_bs, d), F32)],
        compiler_params=_cparams(("arbitrary",)),
        name="normmod",
    )(xp, xs, g.reshape(1, d), mods_p, mods_p, mods, mods)


def _final_norm_kernel(x_ref, g_ref, o_ref):
    def chunk(c, carry):
        rows = pl.ds(pl.multiple_of(c * NORM_CHUNK, NORM_CHUNK), NORM_CHUNK)
        o_ref[rows, :] = _rms(x_ref[rows, :], g_ref[...])
        return carry

    lax.fori_loop(0, o_ref.shape[0] // NORM_CHUNK, chunk, 0, unroll=4)


def _final_norm(x, g, row0, rows):
    d = x.shape[1]
    tr = ROW_TILE
    b0 = row0 // tr
    return pl.pallas_call(
        _final_norm_kernel,
        grid=(rows // tr,),
        in_specs=[pl.BlockSpec((tr, d), lambda i: (i + b0, 0)),
                  pl.BlockSpec((1, d), lambda i: (0, 0))],
        out_specs=pl.BlockSpec((tr, d), lambda i: (i, 0)),
        out_shape=jax.ShapeDtypeStruct((rows, d), F32),
        compiler_params=_cparams(("arbitrary",)),
        name="final_norm",
    )(x, g.reshape(1, d))


ADA_TILE = 256


def _swiglu_up_kernel(n_full, tail, with_ada, x_ref, w1_ref, w3_ref, *rest):
    i = pl.program_id(0)
    j = pl.program_id(1)
    if with_ada:
        c_ref, wa_ref, ba_ref, _, o_ref, mods_ref, sc_ref = rest

        @pl.when(jnp.logical_and(i == 0, j == 0))
        def _():
            c = c_ref[...]
            sc_ref[...] = (c * jax.nn.sigmoid(c)).astype(BF16)
    else:
        o_ref, = rest

    def body(cols):
        x = x_ref[...]
        a = _dot(x, w1_ref[:, 0:cols])
        b = _dot(x, w3_ref[:, 0:cols])
        if with_ada:
            mods_ref[...] = _dot(sc_ref[...], wa_ref[...]) + ba_ref[...]
        o_ref[:, 0:cols] = (a * jax.nn.sigmoid(a) * b).astype(o_ref.dtype)

    if tail == 0:
        body(o_ref.shape[1])
    else:
        @pl.when(j < n_full)
        def _():
            body(o_ref.shape[1])

        @pl.when(j == n_full)
        def _():
            body(tail)


def _swiglu_up(h, w1, w3, layer, sub, ada=None, tn=256):
    m, d = h.shape
    f = w1.shape[-1]
    tm = MM_ROW_TILE
    n_full, tail = divmod(f, tn)
    nj = pl.cdiv(f, tn)
    grid = (m // tm, nj)
    w_spec = pl.BlockSpec((None, None, d, tn), lambda i, j: (layer, sub, 0, j))
    in_specs = [pl.BlockSpec((tm, d), lambda i, j: (i, 0), pipeline_mode=pl.Buffered(1)), w_spec, w_spec]
    args = [h, w1, w3]
    out_specs = [pl.BlockSpec((tm, tn), lambda i, j: (i, j))]
    out_shape = [jax.ShapeDtypeStruct((m, f), BF16)]
    scratch, aliases = [], {}
    if ada is not None:
        c_all, w_ada, b_ada3, mods, first_tile, n_tiles = ada
        n_steps = grid[0] * nj
        per_layer = w_ada.shape[-1] // ADA_TILE
        per_slab = d // ADA_TILE
        assert n_tiles <= n_steps

        def tile(i, j):
            return first_tile + ((i * nj + j) * n_tiles) // n_steps

        in_specs += [
            pl.BlockSpec(c_all.shape, lambda i, j: (0, 0), pipeline_mode=pl.Buffered(1)),
            pl.BlockSpec((None, d, ADA_TILE), lambda i, j: (tile(i, j) // per_layer, 0, tile(i, j) % per_layer)),
            pl.BlockSpec((None, 1, ADA_TILE), lambda i, j: (tile(i, j) // per_layer, 0, tile(i, j) % per_layer)),
            pl.BlockSpec(memory_space=pl.ANY),
        ]
        args += [c_all, w_ada, b_ada3, mods]
        out_specs.append(pl.BlockSpec(
            (None, None, c_all.shape[0], ADA_TILE),
            lambda i, j: (tile(i, j) // per_layer, (tile(i, j) % per_layer) // per_slab, 0,
                          tile(i, j) % per_slab)))
        out_shape.append(jax.ShapeDtypeStruct(mods.shape, mods.dtype))
        scratch = [pltpu.VMEM(c_all.shape, BF16)]
        aliases = {len(args) - 1: 1}
    res = pl.pallas_call(
        functools.partial(_swiglu_up_kernel, n_full, tail, ada is not None),
        grid=grid,
        in_specs=in_specs,
        out_specs=out_specs,
        out_shape=out_shape,
        scratch_shapes=scratch,
        input_output_aliases=aliases,
        compiler_params=_cparams(("arbitrary", "arbitrary")),
        name="swiglu_up",
    )(*args)
    return res if ada is not None else res[0]


def _proj_kernel(x_ref, w_ref, b_ref, o_ref):
    o_ref[...] = (_dot_nt(x_ref[...], w_ref[...]) + b_ref[...]).astype(o_ref.dtype)


def _in_proj(h, w_in_t, b_in3, layer, n_main, tn=512):
    m, d = h.shape
    tm = MM_ROW_TILE
    return pl.pallas_call(
        _proj_kernel,
        grid=(m // tm, n_main // tn),
        in_specs=[pl.BlockSpec((tm, d), lambda i, j: (i, 0), pipeline_mode=pl.Buffered(1)),
                  pl.BlockSpec((None, tn, d), lambda i, j: (layer, j, 0)),
                  pl.BlockSpec((None, 1, tn), lambda i, j: (layer, 0, j))],
        out_specs=pl.BlockSpec((tm, tn), lambda i, j: (i, j)),
        out_shape=jax.ShapeDtypeStruct((m, n_main), BF16),
        compiler_params=_cparams(("arbitrary", "arbitrary")),
        name="in_proj",
    )(h, w_in_t, b_in3)


def _gate_proj(h, w_in_t, b_in3, layer, n_main, tn=128):
    m, d = h.shape
    tm = MM_ROW_TILE
    jb = n_main // tn
    return pl.pallas_call(
        _proj_kernel,
        grid=(m // tm,),
        in_specs=[pl.BlockSpec((tm, d), lambda i: (i, 0)),
                  pl.BlockSpec((None, tn, d), lambda i: (layer, jb, 0)),
                  pl.BlockSpec((None, 1, tn), lambda i: (layer, 0, jb))],
        out_specs=pl.BlockSpec((tm, tn), lambda i: (i, 0)),
        out_shape=jax.ShapeDtypeStruct((m, tn), F32),
        compiler_params=_cparams(("arbitrary",)),
        name="gate_proj",
    )(h, w_in_t, b_in3)


def _gate_segments(tm, n_tiles, n_bp, seq, n_bs):
    rows_p = n_bp * seq
    tiles = []
    for t in range(n_tiles):
        segs, r = [], t * tm
        while r < (t + 1) * tm:
            if r < rows_p:
                b = r // seq
                end = min((b + 1) * seq, (t + 1) * tm)
                segs.append((r - t * tm, end - t * tm, b))
            else:
                end = r + n_bs
                assert (r - rows_p) % n_bs == 0 and end <= (t + 1) * tm
                segs.append((r - t * tm, end - t * tm, None))
            r = end
        tiles.append(segs)
    return tiles


def _acc_resid_kernel(segments, k_last_valid, res_scale, a_ref, w_ref, x_ref, gp_ref, gs_ref, o_ref):
    i = pl.program_id(0)
    k = pl.program_id(2)
    nk = pl.num_programs(2)
    tm, tk = a_ref.shape

    def sweep(kv, first):
        for r0 in range(0, tm, ACC_SUB_ROWS):
            rs = slice(r0, r0 + ACC_SUB_ROWS)
            part = _dot(a_ref[rs, 0:kv], w_ref[0:kv, :])
            if first:
                o_ref[rs, :] = part
            else:
                o_ref[rs, :] += part

    @pl.when(k == 0)
    def _():
        sweep(tk, True)

    @pl.when(jnp.logical_and(k > 0, k < nk - 1))
    def _():
        sweep(tk, False)

    @pl.when(k == nk - 1)
    def _():
        sweep(k_last_valid, False)
        for t, segs in enumerate(segments):
            @pl.when(i == t)
            def _():
                for r0, r1, b in segs:
                    gate = gs_ref[...] if b is None else gp_ref[b]
                    o_ref[r0:r1, :] = x_ref[r0:r1, :] + (res_scale * gate) * o_ref[r0:r1, :]


def _acc_resid(a, w_full, w_index, x, mods, mods_p, layer, which, sample_rows, res_scale):
    m, kdim = a.shape
    d = x.shape[1]
    tm, tk, tn = ACC_ROW_TILE, ACC_K_TILE, ACC_N_TILE
    n_bp = mods_p.shape[2]
    n_bs = mods.shape[2] - 8
    seq = (m - sample_rows) // n_bp
    assert m % tm == 0 and tm % ACC_SUB_ROWS == 0
    nk = pl.cdiv(kdim, tk)
    assert nk >= 3
    k_last_valid = kdim - (nk - 1) * tk
    lead = (None,) * len(w_index)
    segments = _gate_segments(tm, m // tm, n_bp, seq, n_bs)
    return pl.pallas_call(
        functools.partial(_acc_resid_kernel, segments, k_last_valid, res_scale),
        grid=(m // tm, d // tn, nk),
        in_specs=[
            pl.BlockSpec((tm, tk), lambda i, j, k: (i, k)),
            pl.BlockSpec(lead + (tk, tn), lambda i, j, k: tuple(w_index) + (k, j)),
            pl.BlockSpec((tm, tn), lambda i, j, k: (i, j)),
            pl.BlockSpec((None, None, n_bp, 1, tn), lambda i, j, k: (layer, which, 0, 0, j)),
            pl.BlockSpec((None, None, n_bs, tn), lambda i, j, k: (layer, which, 0, j)),
        ],
        out_specs=pl.BlockSpec((tm, tn), lambda i, j, k: (i, j)),
        out_shape=jax.ShapeDtypeStruct((m, d), F32),
        compiler_params=pltpu.CompilerParams(
            dimension_semantics=("arbitrary", "arbitrary", "arbitrary"),
            vmem_limit_bytes=ACC_VMEM_LIMIT),
        name="acc_resid",
    )(a, w_full, x, mods_p, mods)


HALO = 16


def _pool_prompt_kernel(start, u_ref, wp_ref, sp_ref, y_ref, hist_ref, z_ref):
    t = pl.program_id(1)
    nt = pl.num_programs(1)
    tt = u_ref.shape[0]
    group = wp_ref.shape[1]

    @pl.when(t == 0)
    def _():
        z_ref[0:HALO, :] = jnp.zeros((HALO, z_ref.shape[1]), F32)

    z_ref[HALO:HALO + tt, :] = u_ref[...].astype(F32)

    pos = start + t * tt + lax.broadcasted_iota(jnp.int32, (tt, 1), 0)
    for g, w in enumerate(POOL_WINDOWS):
        cols = slice(g * group, (g + 1) * group)
        cur = z_ref[HALO:HALO + tt, cols]
        acc = cur
        for j in range(1, w):
            acc = acc + z_ref[HALO - j:HALO - j + tt, cols]
        cnt = jnp.minimum(w, pos + 1).astype(F32)
        pooled = acc / cnt - cur
        y = _dot(pooled.astype(BF16), wp_ref[g]) * sp_ref[:, cols]
        y_ref[:, cols] = y.astype(y_ref.dtype)

    @pl.when(t == nt - 1)
    def _():
        hist_ref[...] = z_ref[HALO + tt - (HALO - 1):HALO + tt, :]

    z_ref[0:HALO, :] = z_ref[tt:tt + HALO, :]


def _pool_prompt(proj, w_pool_l, s_pool_l, n_b, seq, d, tt=256):
    pw = s_pool_l.shape[-1]
    ntt = seq // tt
    n_groups, group, _ = w_pool_l.shape
    return pl.pallas_call(
        functools.partial(_pool_prompt_kernel, 0),
        grid=(n_b, ntt),
        in_specs=[
            pl.BlockSpec((tt, pw), lambda b, t: (b * ntt + t, 0)),
            pl.BlockSpec((n_groups, group, group), lambda b, t: (0, 0, 0)),
            pl.BlockSpec((1, pw), lambda b, t: (0, 0)),
        ],
        out_specs=[
            pl.BlockSpec((tt, pw), lambda b, t: (b * ntt + t, 0)),
            pl.BlockSpec((None, HALO - 1, pw), lambda b, t: (b, 0, 0)),
        ],
        out_shape=[jax.ShapeDtypeStruct((proj.shape[0], d), BF16),
                   jax.ShapeDtypeStruct((n_b, HALO - 1, pw), F32)],
        scratch_shapes=[pltpu.VMEM((HALO + tt, pw), F32)],
        compiler_params=_cparams(("arbitrary", "arbitrary")),
        name="pool_prompt",
    )(proj, w_pool_l, s_pool_l.reshape(1, pw))


def _pool_sample_kernel(hist_ref, u_ref, wp_ref, sp_ref, *rest):
    y_ref, nh_ref = rest[-2:]
    n_hist, n_b, _ = hist_ref.shape
    n_t = u_ref.shape[0] // n_b
    g = pl.program_id(0)

    def z(r):
        if r < n_hist:
            return hist_ref[r]
        return u_ref[(r - n_hist) * n_b:(r - n_hist + 1) * n_b, :].astype(F32)

    for r in range(n_hist):
        nh_ref[r] = z(r + n_t)

    for t in range(n_t):
        cur = z(n_hist + t)
        run = cur
        sums = []
        for j in range(1, POOL_WINDOWS[-1]):
            run = run + z(n_hist + t - j)
            if j + 1 in POOL_WINDOWS:
                sums.append(run)
        pooled = jnp.zeros_like(cur)
        for gi, w in enumerate(POOL_WINDOWS):
            pooled = jnp.where(g == gi, sums[gi] / float(w) - cur, pooled)
        y = _dot(pooled.astype(BF16), wp_ref[...]) * sp_ref[...]
        y_ref[t * n_b:(t + 1) * n_b, :] = y.astype(y_ref.dtype)


def _pool_sample(state_tm, layer, hist_prev, proj, w_pool_l, s_pool_l, ymix, row0):
    depth, n_hist, n_b, pw = state_tm.shape
    n_groups, group, _ = w_pool_l.shape
    rows = proj.shape[0] - row0
    rb = row0 // rows
    args = [state_tm, proj, w_pool_l, s_pool_l.reshape(1, pw), ymix]
    in_specs = [
        pl.BlockSpec((None, n_hist, n_b, group), lambda g: (layer, 0, 0, g)),
        pl.BlockSpec((rows, group), lambda g: (rb, g)),
        pl.BlockSpec((None, group, group), lambda g: (g, 0, 0)),
        pl.BlockSpec((1, group), lambda g: (0, g)),
        pl.BlockSpec(memory_space=pl.ANY),
    ]
    aliases = {4: 0}
    if hist_prev is not None:
        args.append(hist_prev)
        in_specs.append(pl.BlockSpec(memory_space=pl.ANY))
        aliases[5] = 1
    return pl.pallas_call(
        _pool_sample_kernel,
        grid=(n_groups,),
        in_specs=in_specs,
        out_specs=[pl.BlockSpec((rows, group), lambda g: (rb, g)),
                   pl.BlockSpec((None, n_hist, n_b, group), lambda g: (layer, 0, 0, g))],
        out_shape=[jax.ShapeDtypeStruct(ymix.shape, ymix.dtype),
                   jax.ShapeDtypeStruct(state_tm.shape, F32)],
        input_output_aliases=aliases,
        compiler_params=_cparams(("arbitrary",)),
        name="pool_sample",
    )(*args)


def _mlstm_segment(q, k, v, ig, lf, c_state, n_state, m_state):
    r = q.shape[0]
    row = lax.broadcasted_iota(jnp.int32, (r, r), 0)
    col = lax.broadcasted_iota(jnp.int32, (r, r), 1)
    causal = col <= row
    lf_rows = jnp.sum(jnp.where(row == col, lf, 0.0), axis=0, keepdims=True)
    ig_rows = jnp.sum(jnp.where(row == col, ig, 0.0), axis=0, keepdims=True)
    b_col = jnp.sum(jnp.where(causal, lf_rows, 0.0), axis=1, keepdims=True)
    b_rows = jnp.sum(jnp.where(row <= col, lf, 0.0), axis=0, keepdims=True)
    dmat = jnp.where(causal, b_col - b_rows + ig_rows, -jnp.inf)
    inter = b_col + m_state
    m_tok = jnp.maximum(inter, jnp.max(dmat, axis=-1, keepdims=True))
    w_intra = jnp.exp(dmat - m_tok)
    w_inter = jnp.exp(inter - m_tok)
    s = _dot_nt(q, k) * w_intra
    num = _dot(s.astype(BF16), v) + w_inter * _dot_nt(q, c_state)
    qn = jnp.sum(q.astype(F32) * n_state, axis=-1, keepdims=True)
    den = jnp.sum(s, axis=-1, keepdims=True) + w_inter * qn
    h = num * (1.0 / jnp.maximum(jnp.abs(den), jnp.exp(-m_tok)))
    b_last = jnp.sum(lf, axis=0, keepdims=True)
    dec = b_last - b_col + ig
    m_new = jnp.maximum(b_last + m_state, jnp.max(dec, axis=0, keepdims=True))
    ws = jnp.exp(dec - m_new)
    wc = jnp.exp(b_last + m_state - m_new)
    kf = k.astype(F32)
    c_new = wc * c_state + _dot_tn((ws * v.astype(F32)).astype(BF16), k)
    n_new = wc * n_state + jnp.sum(ws * kf, axis=0, keepdims=True)
    return h, c_new, n_new, m_new


def _head_out(h, o, g_head):
    hn = h * lax.rsqrt(jnp.mean(h * h, axis=-1, keepdims=True) + EPS) * g_head
    return hn * jax.nn.sigmoid(o.astype(F32))


def _log_sigmoid(x):
    return jnp.minimum(x, 0.0) - jnp.log1p(jnp.exp(-jnp.abs(x)))


def _mlstm_prompt_kernel(q_ref, k_ref, v_ref, o_ref, gt_ref, gh_ref, ymix_in_ref,
                         y_ref, c_out_ref, n_out_ref, m_out_ref, c_s, n_s, m_s):
    del ymix_in_ref
    c = pl.program_id(1)
    nc = pl.num_programs(1)
    dk = c_s.shape[2]
    dv = c_s.shape[1]

    @pl.when(c == 0)
    def _():
        c_s[...] = jnp.zeros(c_s.shape, F32)
        n_s[...] = jnp.zeros(n_s.shape, F32)
        m_s[...] = jnp.zeros(m_s.shape, F32)

    gates = gt_ref[...]
    log_f = _log_sigmoid(gates)
    for h in range(N_HEADS):
        q = q_ref[:, h * dk:(h + 1) * dk]
        k = (k_ref[:, h * dk:(h + 1) * dk].astype(F32) * (dk ** -0.5)).astype(BF16)
        v = v_ref[:, h * dv:(h + 1) * dv]
        ig = gates[:, h:h + 1]
        lf = log_f[:, N_HEADS + h:N_HEADS + h + 1]
        hh, c_new, n_new, m_new = _mlstm_segment(
            q, k, v, ig, lf, c_s[h], n_s[h:h + 1, :], m_s[h:h + 1, 0:1])
        c_s[h] = c_new
        n_s[h:h + 1, :] = n_new
        m_s[h:h + 1, :] = jnp.broadcast_to(m_new, (1, m_s.shape[1]))
        y_ref[:, h * dv:(h + 1) * dv] = _head_out(
            hh, o_ref[:, h * dv:(h + 1) * dv], gh_ref[h:h + 1, :]).astype(y_ref.dtype)

    @pl.when(c == nc - 1)
    def _():
        c_out_ref[...] = c_s[...]
        n_out_ref[...] = n_s[0:N_HEADS, :]
        m_out_ref[...] = m_s[...]


def _mlstm_prompt(proj, gates, g_head_l, ymix, n_b, seq, col_q, dk, dv):
    lc = PROMPT_CHUNK
    nch = seq // lc
    qk_w = N_HEADS * dk
    v_w = N_HEADS * dv
    row = lambda b, c: b * nch + c
    assert col_q % qk_w == 0 and (col_q + 2 * qk_w) % v_w == 0
    bq = col_q // qk_w
    bv = (col_q + 2 * qk_w) // v_w
    y, c_out, n_out, m_out = pl.pallas_call(
        _mlstm_prompt_kernel,
        grid=(n_b, nch),
        in_specs=[
            pl.BlockSpec((lc, qk_w), lambda b, c: (row(b, c), bq)),
            pl.BlockSpec((lc, qk_w), lambda b, c: (row(b, c), bq + 1)),
            pl.BlockSpec((lc, v_w), lambda b, c: (row(b, c), bv)),
            pl.BlockSpec((lc, v_w), lambda b, c: (row(b, c), bv + 1)),
            pl.BlockSpec((lc, gates.shape[1]), lambda b, c: (row(b, c), 0)),
            pl.BlockSpec((N_HEADS, dv), lambda b, c: (0, 0)),
            pl.BlockSpec(memory_space=pl.ANY),
        ],
        out_specs=[
            pl.BlockSpec((lc, v_w), lambda b, c: (row(b, c), 1)),
            pl.BlockSpec((None, N_HEADS, dv, dk), lambda b, c: (b, 0, 0, 0)),
            pl.BlockSpec((None, N_HEADS, dk), lambda b, c: (b, 0, 0)),
            pl.BlockSpec((None, 8, 128), lambda b, c: (b, 0, 0)),
        ],
        out_shape=[jax.ShapeDtypeStruct(ymix.shape, ymix.dtype),
                   jax.ShapeDtypeStruct((n_b, N_HEADS, dv, dk), F32),
                   jax.ShapeDtypeStruct((n_b, N_HEADS, dk), F32),
                   jax.ShapeDtypeStruct((n_b, 8, 128), F32)],
        scratch_shapes=[pltpu.VMEM((N_HEADS, dv, dk), F32),
                        pltpu.VMEM((8, dk), F32),
                        pltpu.VMEM((8, 128), F32)],
        input_output_aliases={6: 0},
        compiler_params=_cparams(("arbitrary", "arbitrary")),
        name="mlstm_prompt",
    )(proj, proj, proj, proj, gates, g_head_l, ymix)
    return y, c_out, n_out, m_out[:, :N_HEADS, 0]


def _mlstm_sample_kernel(seq, has_prev, q_ref, k_ref, v_ref, o_ref, gt_ref, gh_ref, c0_ref, n0_ref, m0_ref,
                         *rest):
    y_ref, c_out_ref, n_out_ref, m_out_ref = rest[-4:]
    rows = q_ref.shape[0]
    dk = c0_ref.shape[3]
    dv = c0_ref.shape[2]
    seg_of_row = lax.broadcasted_iota(jnp.int32, (rows, 1), 0) // seq
    gates = gt_ref[...]
    log_f = _log_sigmoid(gates)
    for h in range(N_HEADS):
        q = q_ref[:, h * dk:(h + 1) * dk]
        k = (k_ref[:, h * dk:(h + 1) * dk].astype(F32) * (dk ** -0.5)).astype(BF16)
        v = v_ref[:, h * dv:(h + 1) * dv]
        ig_all = gates[:, h:h + 1]
        lf_all = log_f[:, N_HEADS + h:N_HEADS + h + 1]
        hh = jnp.zeros((rows, dv), F32)
        for sgm in range(rows // seq):
            mine = seg_of_row == sgm
            ig = jnp.where(mine, ig_all, NEG_BIG)
            lf = jnp.where(mine, lf_all, 0.0)
            h_seg, c_new, n_new, m_new = _mlstm_segment(
                q, k, v, ig, lf, c0_ref[sgm, h], n0_ref[sgm, h:h + 1, :], m0_ref[sgm, h:h + 1, :])
            hh = jnp.where(mine, h_seg, hh)
            c_out_ref[sgm, h] = c_new
            n_out_ref[sgm, h:h + 1, :] = n_new
            m_out_ref[sgm, h:h + 1, :] = m_new
        y_ref[:, h * dv:(h + 1) * dv] = _head_out(
            hh, o_ref[:, h * dv:(h + 1) * dv], gh_ref[h:h + 1, :]).astype(y_ref.dtype)


def _mlstm_sample(proj_bm, gates_bm, g_head_l, state_c, state_n, state_m, layer, c_prev, seq, col_q, dk, dv):
    depth, n_b = state_c.shape[:2]
    grp = SAMPLE_GROUP
    rows = grp * seq
    qk_w = N_HEADS * dk
    v_w = N_HEADS * dv
    bq = col_q // qk_w
    bv = (col_q + 2 * qk_w) // v_w
    has_prev = c_prev is not None
    in_specs = [
        pl.BlockSpec((rows, qk_w), lambda i: (i, bq)),
        pl.BlockSpec((rows, qk_w), lambda i: (i, bq + 1)),
        pl.BlockSpec((rows, v_w), lambda i: (i, bv)),
        pl.BlockSpec((rows, v_w), lambda i: (i, bv + 1)),
        pl.BlockSpec((rows, gates_bm.shape[1]), lambda i: (i, 0)),
        pl.BlockSpec((N_HEADS, dv), lambda i: (0, 0)),
        pl.BlockSpec((None, grp, N_HEADS, dv, dk), lambda i: (layer, i, 0, 0, 0)),
        pl.BlockSpec((None, grp, N_HEADS, dk), lambda i: (layer, i, 0, 0)),
        pl.BlockSpec((None, grp, N_HEADS, 1), lambda i: (layer, i, 0, 0)),
    ]
    args = [proj_bm, proj_bm, proj_bm, proj_bm, gates_bm, g_head_l, state_c, state_n,
            state_m.reshape(depth, n_b, N_HEADS, 1)]
    aliases = {}
    if has_prev:
        in_specs.append(pl.BlockSpec(memory_space=pl.ANY))
        args.append(c_prev)
        aliases = {len(args) - 1: 1}
    y, c_out, n_out, m_out = pl.pallas_call(
        functools.partial(_mlstm_sample_kernel, seq, has_prev),
        grid=(n_b // grp,),
        in_specs=in_specs,
        out_specs=[
            pl.BlockSpec((rows, v_w), lambda i: (i, 0)),
            pl.BlockSpec((None, grp, N_HEADS, dv, dk), lambda i: (layer, i, 0, 0, 0)),
            pl.BlockSpec((grp, N_HEADS, dk), lambda i: (i, 0, 0)),
            pl.BlockSpec((grp, N_HEADS, 1), lambda i: (i, 0, 0)),
        ],
        out_shape=[jax.ShapeDtypeStruct((n_b * seq, v_w), BF16),
                   jax.ShapeDtypeStruct(state_c.shape, F32),
                   jax.ShapeDtypeStruct((n_b, N_HEADS, dk), F32),
                   jax.ShapeDtypeStruct((n_b, N_HEADS, 1), F32)],
        input_output_aliases=aliases,
        compiler_params=_cparams(("arbitrary",)),
        name="mlstm_sample",
    )(*args)
    return y, c_out, n_out, m_out[:, :, 0]


def _to_time_major(a, n_b, seq):
    return jnp.transpose(a.reshape(n_b, seq, -1), (1, 0, 2)).reshape(n_b * seq, -1)


def _to_batch_major(a, n_b, seq):
    return jnp.transpose(a.reshape(seq, n_b, -1), (1, 0, 2)).reshape(n_b * seq, -1)


def kernel(x_prompt, x_sample, state_pool, state_C, state_n, state_m, c_prompt, c_sample,
           w_ada, b_ada, g_norm, w_in, b_in, w_pool, s_pool, g_head, w_out, w1, w3, w2, g_final):
    n_bp, seq_p, d = x_prompt.shape
    n_bs, seq_s, _ = x_sample.shape
    depth = w_ada.shape[0]
    pw = s_pool.shape[-1]
    dv = g_head.shape[-1]
    dk = state_C.shape[-1]
    n_hist = state_pool.shape[2]
    rows_p = n_bp * seq_p
    rows_s = n_bs * seq_s
    n_main = pw + 2 * N_HEADS * dk + 2 * N_HEADS * dv

    xp0 = x_prompt.reshape(rows_p, d)
    xs0 = _to_time_major(x_sample, n_bs, seq_s)

    c_all = jnp.concatenate([c_sample, c_prompt, jnp.zeros((8 - n_bp, d), F32)], axis=0)
    b_ada3 = b_ada.reshape(depth, 1, b_ada.shape[-1])
    tiles_per_slab = d // ADA_TILE
    tiles_per_layer = N_SUB * N_MOD * tiles_per_slab
    assert depth == 2
    mods = _ada_head(c_all, w_ada, b_ada3, N_MOD)
    ada_jobs = {(0, 0): (N_MOD * tiles_per_slab, tiles_per_layer - N_MOD * tiles_per_slab),
                (0, 1): (tiles_per_layer, tiles_per_layer)}

    def prompt_rows(mods):
        return mods[:, :, n_bs:n_bs + n_bp][:, :, :, None, :]

    b_in3 = b_in.reshape(depth, 1, b_in.shape[-1])
    w_in_t = jnp.transpose(w_in, (0, 2, 1))
    pool_p, c_p, n_p, m_p = [], [], [], []
    n_s, m_s = [], []
    c_s_all = None
    hist_s_tm = None
    state_pool_tm = jnp.transpose(state_pool, (0, 2, 1, 3))

    def ffn(x, h, mods, l, sub_layer, ffn_idx):
        job = ada_jobs.get((l, ffn_idx))
        if job is None:
            act = _swiglu_up(h, w1, w3, l, ffn_idx)
        else:
            act, mods = _swiglu_up(h, w1, w3, l, ffn_idx, (c_all, w_ada, b_ada3, mods) + job)
        x = _acc_resid(act, w2, (l, ffn_idx), x, mods, prompt_rows(mods), l, sub_layer * N_MOD + 2,
                       rows_s, FFN_RES)
        return x, mods

    x = None
    for l in range(depth):
        mods_p = prompt_rows(mods)
        if l == 0:
            h, x = _normmod(xp0, xs0, 0, g_norm[l, 0], mods, mods_p, l, 0, seq_p, True)
        else:
            h, = _normmod(x, x, rows_p, g_norm[l, 0], mods, mods_p, l, 0, seq_p, False)
        x, mods = ffn(x, h, mods, l, 0, 0)
        mods_p = prompt_rows(mods)

        h, = _normmod(x, x, rows_p, g_norm[l, 1], mods, mods_p, l, 1, seq_p, False)
        proj = _in_proj(h, w_in_t, b_in3, l, n_main)
        gates = _gate_proj(h, w_in_t, b_in3, l, n_main)

        ymix, hist_p = _pool_prompt(proj, w_pool[l], s_pool[l], n_bp, seq_p, d)
        ymix, cp, np_, mp = _mlstm_prompt(proj, gates, g_head[l], ymix, n_bp, seq_p, pw, dk, dv)
        ymix, hist_s_tm = _pool_sample(state_pool_tm, l, hist_s_tm, proj, w_pool[l], s_pool[l], ymix, rows_p)
        proj_bm = _to_batch_major(proj[rows_p:], n_bs, seq_s)
        gates_bm = _to_batch_major(gates[rows_p:], n_bs, seq_s)
        y_ms, c_s_all, ns, ms = _mlstm_sample(proj_bm, gates_bm, g_head[l], state_C, state_n, state_m,
                                              l, c_s_all, seq_s, pw, dk, dv)
        ymix = lax.dynamic_update_slice(ymix, _to_time_major(y_ms, n_bs, seq_s), (rows_p, pw))
        x = _acc_resid(ymix, w_out, (l,), x, mods, mods_p, l, 1 * N_MOD + 2, rows_s, 1.0)

        pool_p.append(hist_p); c_p.append(cp); n_p.append(np_); m_p.append(mp)
        n_s.append(ns); m_s.append(ms)

        h, = _normmod(x, x, rows_p, g_norm[l, 2], mods, mods_p, l, 2, seq_p, False)
        x, mods = ffn(x, h, mods, l, 2, 1)

    y_prompt = _final_norm(x, g_final, 0, rows_p).reshape(n_bp, seq_p, d)
    y_sample = _to_batch_major(_final_norm(x, g_final, rows_p, rows_s), n_bs, seq_s).reshape(n_bs, seq_s, d)
    return (y_prompt, y_sample,
            jnp.stack(pool_p), jnp.stack(c_p), jnp.stack(n_p), jnp.stack(m_p),
            jnp.transpose(hist_s_tm, (0, 2, 1, 3)), c_s_all, jnp.stack(n_s), jnp.stack(m_s))
```

```python
import functools

import jax
import jax.numpy as jnp
from jax import lax
from jax.experimental import pallas as pl
from jax.experimental.pallas import tpu as pltpu

F32 = jnp.float32
BF16 = jnp.bfloat16

EPS = 1e-6
FFN_RES = 0.5
POOL_WINDOWS = (2, 4, 8, 16)
N_HEADS = 4
N_SUB = 3
N_MOD = 3
PROMPT_CHUNK = 256
SAMPLE_GROUP = 4
NEG_BIG = -1e30

VMEM_LIMIT = 56 * 1024 * 1024

ROW_TILE = 256
MM_ROW_TILE = 2176
ACC_ROW_TILE = 2176
ACC_SUB_ROWS = 544
ACC_VMEM_LIMIT = 60 * 1024 * 1024
ACC_K_TILE = 1024
ACC_N_TILE = 1024


def _cparams(sem):
    return pltpu.CompilerParams(dimension_semantics=sem, vmem_limit_bytes=VMEM_LIMIT)


def _dot(a, b):
    return lax.dot_general(a, b, (((1,), (0,)), ((), ())), preferred_element_type=F32)


def _dot_nt(a, b):
    return lax.dot_general(a, b, (((1,), (1,)), ((), ())), preferred_element_type=F32)


def _dot_tn(a, b):
    return lax.dot_general(a, b, (((0,), (0,)), ((), ())), preferred_element_type=F32)


def _ada_kernel(c_ref, w_ref, b_ref, o_ref):
    c = c_ref[...]
    sc = (c * jax.nn.sigmoid(c)).astype(BF16)
    o_ref[...] = _dot(sc, w_ref[...]) + b_ref[...]


def _ada_head(c_all, w_ada, b_ada3, n_slabs, tn=1024):
    depth, d, n = w_ada.shape
    rows = c_all.shape[0]
    per = d // tn
    return pl.pallas_call(
        _ada_kernel,
        grid=(n_slabs * per,),
        in_specs=[
            pl.BlockSpec((rows, d), lambda t: (0, 0)),
            pl.BlockSpec((None, d, tn), lambda t: (0, 0, t)),
            pl.BlockSpec((None, 1, tn), lambda t: (0, 0, t)),
        ],
        out_specs=pl.BlockSpec((None, None, rows, tn), lambda t: (0, t // per, 0, t % per)),
        out_shape=jax.ShapeDtypeStruct((depth, n // d, rows, d), F32),
        compiler_params=_cparams(("arbitrary",)),
        name="ada_mods",
    )(c_all, w_ada, b_ada3)


def _rms(x, g):
    return x * lax.rsqrt(jnp.mean(x * x, axis=-1, keepdims=True) + EPS) * g


NORM_CHUNK = 8


def _normmod_kernel(n_prompt_tiles, emit_x, xp_ref, xs_ref, g_ref, shp_ref, scp_ref, shs_ref, scs_ref,
                    h_ref, *rest):
    xo_ref = rest[0] if emit_x else None
    gm_ref = rest[-1]
    i = pl.program_id(0)
    n_bs = shs_ref.shape[0]
    n_chunks = h_ref.shape[0] // NORM_CHUNK

    def run(x_ref, mod_rows):
        def chunk(c, carry):
            r = pl.multiple_of(c * NORM_CHUNK, NORM_CHUNK)
            rows = pl.ds(r, NORM_CHUNK)
            x = x_ref[rows, :]
            rstd = lax.rsqrt(jnp.mean(x * x, axis=-1, keepdims=True) + EPS)
            gm, sh = mod_rows(r)
            h_ref[rows, :] = (x * rstd * gm + sh).astype(h_ref.dtype)
            if emit_x:
                xo_ref[rows, :] = x
            return carry

        lax.fori_loop(0, n_chunks, chunk, 0, unroll=4)

    @pl.when(i < n_prompt_tiles)
    def _():
        gm_ref[0:1, :] = g_ref[...] * (1.0 + scp_ref[...])
        run(xp_ref, lambda r: (gm_ref[0:1, :], shp_ref[...]))

    @pl.when(i >= n_prompt_tiles)
    def _():
        gm_ref[...] = g_ref[...] * (1.0 + scs_ref[...])

        def mod_rows(r):
            rb = pl.ds(pl.multiple_of(lax.rem(r, n_bs), NORM_CHUNK), NORM_CHUNK)
            return gm_ref[rb, :], shs_ref[rb, :]

        run(xs_ref, mod_rows)


def _normmod(xp, xs, xs_row0, g, mods, mods_p, layer, sub, seq, emit_x):
    d = xp.shape[1]
    tr = ROW_TILE if emit_x else 2 * ROW_TILE
    n_bp = mods_p.shape[2]
    n_bs = mods.shape[2] - 8
    rows_p = n_bp * seq
    rows_s = xs.shape[0] - xs_row0
    m = rows_p + rows_s
    n_pt = rows_p // tr
    tiles_per_seq = seq // tr
    s0 = xs_row0 // tr
    i_shift, i_scale = sub * N_MOD, sub * N_MOD + 1
    p_map = lambda which: (lambda i: (layer, which, jnp.minimum(i // tiles_per_seq, n_bp - 1), 0, 0))
    s_map = lambda which: (lambda i: (layer, which, 0, 0))
    out_specs = [pl.BlockSpec((tr, d), lambda i: (i, 0))]
    out_shape = [jax.ShapeDtypeStruct((m, d), BF16)]
    if emit_x:
        out_specs.append(pl.BlockSpec((tr, d), lambda i: (i, 0)))
        out_shape.append(jax.ShapeDtypeStruct((m, d), F32))
    return pl.pallas_call(
        functools.partial(_normmod_kernel, n_pt, emit_x),
        grid=(m // tr,),
        in_specs=[
            pl.BlockSpec((tr, d), lambda i: (jnp.minimum(i, n_pt - 1), 0)),
            pl.BlockSpec((tr, d), lambda i: (s0 + jnp.maximum(i - n_pt, 0), 0)),
            pl.BlockSpec((1, d), lambda i: (0, 0)),
            pl.BlockSpec((None, None, None, 1, d), p_map(i_shift)),
            pl.BlockSpec((None, None, None, 1, d), p_map(i_scale)),
            pl.BlockSpec((None, None, n_bs, d), s_map(i_shift)),
            pl.BlockSpec((None, None, n_bs, d), s_map(i_scale)),
        ],
        out_specs=out_specs,
        out_shape=out_shape,
        scratch_shapes=[pltpu.VMEM((n_bs, d), F32)],
        compiler_params=_cparams(("arbitrary",)),
        name="normmod",
    )(xp, xs, g.reshape(1, d), mods_p, mods_p, mods, mods)


def _final_norm_kernel(x_ref, g_ref, o_ref):
    def chunk(c, carry):
        rows = pl.ds(pl.multiple_of(c * NORM_CHUNK, NORM_CHUNK), NORM_CHUNK)
        o_ref[rows, :] = _rms(x_ref[rows, :], g_ref[...])
        return carry

    lax.fori_loop(0, o_ref.shape[0] // NORM_CHUNK, chunk, 0, unroll=4)


def _final_norm(x, g, row0, rows):
    d = x.shape[1]
    tr = ROW_TILE
    b0 = row0 // tr
    return pl.pallas_call(
        _final_norm_kernel,
        grid=(rows // tr,),
        in_specs=[pl.BlockSpec((tr, d), lambda i: (i + b0, 0)),
                  pl.BlockSpec((1, d), lambda i: (0, 0))],
        out_specs=pl.BlockSpec((tr, d), lambda i: (i, 0)),
        out_shape=jax.ShapeDtypeStruct((rows, d), F32),
        compiler_params=_cparams(("arbitrary",)),
        name="final_norm",
    )(x, g.reshape(1, d))


ADA_TILE = 256


def _swiglu_up_kernel(n_full, tail, ada_tiles, x_ref, w1_ref, w3_ref, *rest):
    i = pl.program_id(0)
    j = pl.program_id(1)
    if ada_tiles:
        c_ref, wa_ref, ba_ref, _, o_ref, mods_ref, sc_ref = rest

        @pl.when(jnp.logical_and(i == 0, j == 0))
        def _():
            c = c_ref[...]
            sc_ref[...] = (c * jax.nn.sigmoid(c)).astype(BF16)
    else:
        o_ref, = rest

    def body(cols, with_ada):
        x = x_ref[...]
        a = _dot(x, w1_ref[:, 0:cols])
        b = _dot(x, w3_ref[:, 0:cols])
        if with_ada:
            mods_ref[...] = _dot(sc_ref[...], wa_ref[...]) + ba_ref[...]
        o_ref[:, 0:cols] = (a * jax.nn.sigmoid(a) * b).astype(o_ref.dtype)

    def step(with_ada):
        if tail == 0:
            body(o_ref.shape[1], with_ada)
        else:
            @pl.when(j < n_full)
            def _():
                body(o_ref.shape[1], with_ada)

            @pl.when(j == n_full)
            def _():
                body(tail, with_ada)

    if ada_tiles:
        n_steps = pl.num_programs(0) * pl.num_programs(1)
        s = i * pl.num_programs(1) + j
        fresh = jnp.logical_or(
            s == 0, (s * ada_tiles) // n_steps != ((jnp.maximum(s, 1) - 1) * ada_tiles) // n_steps)

        @pl.when(fresh)
        def _():
            step(True)

        @pl.when(jnp.logical_not(fresh))
        def _():
            step(False)
    else:
        step(False)


def _swiglu_up(h, w1, w3, layer, sub, ada=None, tn=256):
    m, d = h.shape
    f = w1.shape[-1]
    tm = MM_ROW_TILE
    n_full, tail = divmod(f, tn)
    nj = pl.cdiv(f, tn)
    grid = (m // tm, nj)
    w_spec = pl.BlockSpec((None, None, d, tn), lambda i, j: (layer, sub, 0, j))
    in_specs = [pl.BlockSpec((tm, d), lambda i, j: (i, 0), pipeline_mode=pl.Buffered(1)), w_spec, w_spec]
    args = [h, w1, w3]
    out_specs = [pl.BlockSpec((tm, tn), lambda i, j: (i, j))]
    out_shape = [jax.ShapeDtypeStruct((m, f), BF16)]
    scratch, aliases = [], {}
    if ada is not None:
        c_all, w_ada, b_ada3, mods, first_tile, n_tiles = ada
        n_steps = grid[0] * nj
        per_layer = w_ada.shape[-1] // ADA_TILE
        per_slab = d // ADA_TILE
        assert n_tiles <= n_steps

        def tile(i, j):
            return first_tile + ((i * nj + j) * n_tiles) // n_steps

        in_specs += [
            pl.BlockSpec(c_all.shape, lambda i, j: (0, 0), pipeline_mode=pl.Buffered(1)),
            pl.BlockSpec((None, d, ADA_TILE), lambda i, j: (tile(i, j) // per_layer, 0, tile(i, j) % per_layer)),
            pl.BlockSpec((None, 1, ADA_TILE), lambda i, j: (tile(i, j) // per_layer, 0, tile(i, j) % per_layer)),
            pl.BlockSpec(memory_space=pl.ANY),
        ]
        args += [c_all, w_ada, b_ada3, mods]
        out_specs.append(pl.BlockSpec(
            (None, None, c_all.shape[0], ADA_TILE),
            lambda i, j: (tile(i, j) // per_layer, (tile(i, j) % per_layer) // per_slab, 0,
                          tile(i, j) % per_slab)))
        out_shape.append(jax.ShapeDtypeStruct(mods.shape, mods.dtype))
        scratch = [pltpu.VMEM(c_all.shape, BF16)]
        aliases = {len(args) - 1: 1}
    res = pl.pallas_call(
        functools.partial(_swiglu_up_kernel, n_full, tail, 0 if ada is None else ada[5]),
        grid=grid,
        in_specs=in_specs,
        out_specs=out_specs,
        out_shape=out_shape,
        scratch_shapes=scratch,
        input_output_aliases=aliases,
        compiler_params=_cparams(("arbitrary", "arbitrary")),
        name="swiglu_up",
    )(*args)
    return res if ada is not None else res[0]


def _proj_kernel(x_ref, w_ref, b_ref, o_ref):
    o_ref[...] = (_dot_nt(x_ref[...], w_ref[...]) + b_ref[...]).astype(o_ref.dtype)


def _in_proj(h, w_in_t, b_in3, layer, n_main, tn=512):
    m, d = h.shape
    tm = MM_ROW_TILE
    return pl.pallas_call(
        _proj_kernel,
        grid=(m // tm, n_main // tn),
        in_specs=[pl.BlockSpec((tm, d), lambda i, j: (i, 0), pipeline_mode=pl.Buffered(1)),
                  pl.BlockSpec((None, tn, d), lambda i, j: (layer, j, 0)),
                  pl.BlockSpec((None, 1, tn), lambda i, j: (layer, 0, j))],
        out_specs=pl.BlockSpec((tm, tn), lambda i, j: (i, j)),
        out_shape=jax.ShapeDtypeStruct((m, n_main), BF16),
        compiler_params=_cparams(("arbitrary", "arbitrary")),
        name="in_proj",
    )(h, w_in_t, b_in3)


def _gate_proj(h, w_in_t, b_in3, layer, n_main, tn=128):
    m, d = h.shape
    tm = MM_ROW_TILE
    jb = n_main // tn
    return pl.pallas_call(
        _proj_kernel,
        grid=(m // tm,),
        in_specs=[pl.BlockSpec((tm, d), lambda i: (i, 0)),
                  pl.BlockSpec((None, tn, d), lambda i: (layer, jb, 0)),
                  pl.BlockSpec((None, 1, tn), lambda i: (layer, 0, jb))],
        out_specs=pl.BlockSpec((tm, tn), lambda i: (i, 0)),
        out_shape=jax.ShapeDtypeStruct((m, tn), F32),
        compiler_params=_cparams(("arbitrary",)),
        name="gate_proj",
    )(h, w_in_t, b_in3)


def _gate_segments(tm, n_tiles, n_bp, seq, n_bs):
    rows_p = n_bp * seq
    tiles = []
    for t in range(n_tiles):
        segs, r = [], t * tm
        while r < (t + 1) * tm:
            if r < rows_p:
                b = r // seq
                end = min((b + 1) * seq, (t + 1) * tm)
                segs.append((r - t * tm, end - t * tm, b))
            else:
                end = r + n_bs
                assert (r - rows_p) % n_bs == 0 and end <= (t + 1) * tm
                segs.append((r - t * tm, end - t * tm, None))
            r = end
        tiles.append(segs)
    return tiles


def _acc_resid_kernel(segments, k_last_valid, res_scale, a_ref, w_ref, x_ref, gp_ref, gs_ref, o_ref):
    i = pl.program_id(0)
    k = pl.program_id(2)
    nk = pl.num_programs(2)
    tm, tk = a_ref.shape

    def sweep(kv, first):
        for r0 in range(0, tm, ACC_SUB_ROWS):
            rs = slice(r0, r0 + ACC_SUB_ROWS)
            part = _dot(a_ref[rs, 0:kv], w_ref[0:kv, :])
            if first:
                o_ref[rs, :] = part
            else:
                o_ref[rs, :] += part

    @pl.when(k == 0)
    def _():
        sweep(tk, True)

    @pl.when(jnp.logical_and(k > 0, k < nk - 1))
    def _():
        sweep(tk, False)

    for t, segs in enumerate(segments):
        @pl.when(jnp.logical_and(k == nk - 1, i == t))
        def _():
            for r0 in range(0, tm, ACC_SUB_ROWS):
                r1 = r0 + ACC_SUB_ROWS
                part = _dot(a_ref[r0:r1, 0:k_last_valid], w_ref[0:k_last_valid, :])
                for s0, s1, b in segs:
                    q0, q1 = max(s0, r0), min(s1, r1)
                    if q0 >= q1:
                        continue
                    gate = gs_ref[q0 - s0:q1 - s0, :] if b is None else gp_ref[b]
                    o_ref[q0:q1, :] = (x_ref[q0:q1, :]
                                       + (res_scale * gate) * (o_ref[q0:q1, :] + part[q0 - r0:q1 - r0, :]))


def _acc_resid(a, w_full, w_index, x, mods, mods_p, layer, which, sample_rows, res_scale):
    m, kdim = a.shape
    d = x.shape[1]
    tm, tk, tn = ACC_ROW_TILE, ACC_K_TILE, ACC_N_TILE
    n_bp = mods_p.shape[2]
    n_bs = mods.shape[2] - 8
    seq = (m - sample_rows) // n_bp
    assert m % tm == 0 and tm % ACC_SUB_ROWS == 0
    nk = pl.cdiv(kdim, tk)
    assert nk >= 3
    k_last_valid = kdim - (nk - 1) * tk
    lead = (None,) * len(w_index)
    segments = _gate_segments(tm, m // tm, n_bp, seq, n_bs)
    return pl.pallas_call(
        functools.partial(_acc_resid_kernel, segments, k_last_valid, res_scale),
        grid=(m // tm, d // tn, nk),
        in_specs=[
            pl.BlockSpec((tm, tk), lambda i, j, k: (i, k)),
            pl.BlockSpec(lead + (tk, tn), lambda i, j, k: tuple(w_index) + (k, j)),
            pl.BlockSpec((tm, tn), lambda i, j, k: (i, j)),
            pl.BlockSpec((None, None, n_bp, 1, tn), lambda i, j, k: (layer, which, 0, 0, j)),
            pl.BlockSpec((None, None, n_bs, tn), lambda i, j, k: (layer, which, 0, j)),
        ],
        out_specs=pl.BlockSpec((tm, tn), lambda i, j, k: (i, j)),
        out_shape=jax.ShapeDtypeStruct((m, d), F32),
        compiler_params=pltpu.CompilerParams(
            dimension_semantics=("arbitrary", "arbitrary", "arbitrary"),
            vmem_limit_bytes=ACC_VMEM_LIMIT),
        name="acc_resid",
    )(a, w_full, x, mods_p, mods)


HALO = 16


def _pool_prompt_kernel(start, u_ref, wp_ref, sp_ref, y_ref, hist_ref, z_ref):
    t = pl.program_id(1)
    nt = pl.num_programs(1)
    tt = u_ref.shape[0]
    group = wp_ref.shape[1]

    @pl.when(t == 0)
    def _():
        z_ref[0:HALO, :] = jnp.zeros((HALO, z_ref.shape[1]), F32)

    z_ref[HALO:HALO + tt, :] = u_ref[...].astype(F32)

    pos = start + t * tt + lax.broadcasted_iota(jnp.int32, (tt, 1), 0)
    for g, w in enumerate(POOL_WINDOWS):
        cols = slice(g * group, (g + 1) * group)
        cur = z_ref[HALO:HALO + tt, cols]
        acc = cur
        for j in range(1, w):
            acc = acc + z_ref[HALO - j:HALO - j + tt, cols]
        cnt = jnp.minimum(w, pos + 1).astype(F32)
        pooled = acc / cnt - cur
        y = _dot(pooled.astype(BF16), wp_ref[g]) * sp_ref[:, cols]
        y_ref[:, cols] = y.astype(y_ref.dtype)

    @pl.when(t == nt - 1)
    def _():
        hist_ref[...] = z_ref[HALO + tt - (HALO - 1):HALO + tt, :]

    z_ref[0:HALO, :] = z_ref[tt:tt + HALO, :]


def _pool_prompt(proj, w_pool_l, s_pool_l, n_b, seq, d, tt=256):
    pw = s_pool_l.shape[-1]
    ntt = seq // tt
    n_groups, group, _ = w_pool_l.shape
    return pl.pallas_call(
        functools.partial(_pool_prompt_kernel, 0),
        grid=(n_b, ntt),
        in_specs=[
            pl.BlockSpec((tt, pw), lambda b, t: (b * ntt + t, 0)),
            pl.BlockSpec((n_groups, group, group), lambda b, t: (0, 0, 0)),
            pl.BlockSpec((1, pw), lambda b, t: (0, 0)),
        ],
        out_specs=[
            pl.BlockSpec((tt, pw), lambda b, t: (b * ntt + t, 0)),
            pl.BlockSpec((None, HALO - 1, pw), lambda b, t: (b, 0, 0)),
        ],
        out_shape=[jax.ShapeDtypeStruct((proj.shape[0], d), BF16),
                   jax.ShapeDtypeStruct((n_b, HALO - 1, pw), F32)],
        scratch_shapes=[pltpu.VMEM((HALO + tt, pw), F32)],
        compiler_params=_cparams(("arbitrary", "arbitrary")),
        name="pool_prompt",
    )(proj, w_pool_l, s_pool_l.reshape(1, pw))


def _pool_sample_kernel(hist_ref, u_ref, wp_ref, sp_ref, *rest):
    y_ref, nh_ref = rest[-2:]
    n_hist, n_b, _ = hist_ref.shape
    n_t = u_ref.shape[0] // n_b
    g = pl.program_id(0)

    def z(r):
        if r < n_hist:
            return hist_ref[r]
        return u_ref[(r - n_hist) * n_b:(r - n_hist + 1) * n_b, :].astype(F32)

    for r in range(n_hist):
        nh_ref[r] = z(r + n_t)

    for t in range(n_t):
        cur = z(n_hist + t)
        run = cur
        sums = []
        for j in range(1, POOL_WINDOWS[-1]):
            run = run + z(n_hist + t - j)
            if j + 1 in POOL_WINDOWS:
                sums.append(run)
        pooled = jnp.zeros_like(cur)
        for gi, w in enumerate(POOL_WINDOWS):
            pooled = jnp.where(g == gi, sums[gi] / float(w) - cur, pooled)
        y = _dot(pooled.astype(BF16), wp_ref[...]) * sp_ref[...]
        y_ref[t * n_b:(t + 1) * n_b, :] = y.astype(y_ref.dtype)


def _pool_sample(state_tm, layer, hist_prev, proj, w_pool_l, s_pool_l, ymix, row0):
    depth, n_hist, n_b, pw = state_tm.shape
    n_groups, group, _ = w_pool_l.shape
    rows = proj.shape[0] - row0
    rb = row0 // rows
    args = [state_tm, proj, w_pool_l, s_pool_l.reshape(1, pw), ymix]
    in_specs = [
        pl.BlockSpec((None, n_hist, n_b, group), lambda g: (layer, 0, 0, g)),
        pl.BlockSpec((rows, group), lambda g: (rb, g)),
        pl.BlockSpec((None, group, group), lambda g: (g, 0, 0)),
        pl.BlockSpec((1, group), lambda g: (0, g)),
        pl.BlockSpec(memory_space=pl.ANY),
    ]
    aliases = {4: 0}
    if hist_prev is not None:
        args.append(hist_prev)
        in_specs.append(pl.BlockSpec(memory_space=pl.ANY))
        aliases[5] = 1
    return pl.pallas_call(
        _pool_sample_kernel,
        grid=(n_groups,),
        in_specs=in_specs,
        out_specs=[pl.BlockSpec((rows, group), lambda g: (rb, g)),
                   pl.BlockSpec((None, n_hist, n_b, group), lambda g: (layer, 0, 0, g))],
        out_shape=[jax.ShapeDtypeStruct(ymix.shape, ymix.dtype),
                   jax.ShapeDtypeStruct(state_tm.shape, F32)],
        input_output_aliases=aliases,
        compiler_params=_cparams(("arbitrary",)),
        name="pool_sample",
    )(*args)


def _mlstm_segment(q, k, v, ig, lf, c_state, n_state, m_state):
    r = q.shape[0]
    row = lax.broadcasted_iota(jnp.int32, (r, r), 0)
    col = lax.broadcasted_iota(jnp.int32, (r, r), 1)
    causal = col <= row
    lf_rows = jnp.sum(jnp.where(row == col, lf, 0.0), axis=0, keepdims=True)
    ig_rows = jnp.sum(jnp.where(row == col, ig, 0.0), axis=0, keepdims=True)
    b_col = jnp.sum(jnp.where(causal, lf_rows, 0.0), axis=1, keepdims=True)
    b_rows = jnp.sum(jnp.where(row <= col, lf, 0.0), axis=0, keepdims=True)
    dmat = jnp.where(causal, b_col - b_rows + ig_rows, -jnp.inf)
    inter = b_col + m_state
    m_tok = jnp.maximum(inter, jnp.max(dmat, axis=-1, keepdims=True))
    w_intra = jnp.exp(dmat - m_tok)
    w_inter = jnp.exp(inter - m_tok)
    s = _dot_nt(q, k) * w_intra
    num = _dot(s.astype(BF16), v) + w_inter * _dot_nt(q, c_state)
    qn = jnp.sum(q.astype(F32) * n_state, axis=-1, keepdims=True)
    den = jnp.sum(s, axis=-1, keepdims=True) + w_inter * qn
    h = num * (1.0 / jnp.maximum(jnp.abs(den), jnp.exp(-m_tok)))
    b_last = jnp.sum(lf, axis=0, keepdims=True)
    dec = b_last - b_col + ig
    m_new = jnp.maximum(b_last + m_state, jnp.max(dec, axis=0, keepdims=True))
    ws = jnp.exp(dec - m_new)
    wc = jnp.exp(b_last + m_state - m_new)
    wk = ws * k.astype(F32)
    c_new = wc * c_state + _dot_tn(v, wk.astype(BF16))
    n_new = wc * n_state + jnp.sum(wk, axis=0, keepdims=True)
    return h, c_new, n_new, m_new


def _head_out(h, o, g_head):
    hn = h * lax.rsqrt(jnp.mean(h * h, axis=-1, keepdims=True) + EPS) * g_head
    return hn * jax.nn.sigmoid(o.astype(F32))


def _log_sigmoid(x):
    return jnp.minimum(x, 0.0) - jnp.log1p(jnp.exp(-jnp.abs(x)))


def _mlstm_prompt_kernel(q_ref, k_ref, v_ref, o_ref, gt_ref, gh_ref, ymix_in_ref,
                         y_ref, c_out_ref, n_out_ref, m_out_ref, c_s, n_s, m_s):
    del ymix_in_ref
    c = pl.program_id(1)
    nc = pl.num_programs(1)
    dk = c_s.shape[2]
    dv = c_s.shape[1]

    @pl.when(c == 0)
    def _():
        c_s[...] = jnp.zeros(c_s.shape, F32)
        n_s[...] = jnp.zeros(n_s.shape, F32)
        m_s[...] = jnp.zeros(m_s.shape, F32)

    gates = gt_ref[...]
    log_f = _log_sigmoid(gates)
    for h in range(N_HEADS):
        q = q_ref[:, h * dk:(h + 1) * dk]
        k = (k_ref[:, h * dk:(h + 1) * dk].astype(F32) * (dk ** -0.5)).astype(BF16)
        v = v_ref[:, h * dv:(h + 1) * dv]
        ig = gates[:, h:h + 1]
        lf = log_f[:, N_HEADS + h:N_HEADS + h + 1]
        hh, c_new, n_new, m_new = _mlstm_segment(
            q, k, v, ig, lf, c_s[h], n_s[h:h + 1, :], m_s[h:h + 1, 0:1])
        c_s[h] = c_new
        n_s[h:h + 1, :] = n_new
        m_s[h:h + 1, :] = jnp.broadcast_to(m_new, (1, m_s.shape[1]))
        y_ref[:, h * dv:(h + 1) * dv] = _head_out(
            hh, o_ref[:, h * dv:(h + 1) * dv], gh_ref[h:h + 1, :]).astype(y_ref.dtype)

    @pl.when(c == nc - 1)
    def _():
        c_out_ref[...] = c_s[...]
        n_out_ref[...] = n_s[0:N_HEADS, :]
        m_out_ref[...] = m_s[...]


def _mlstm_prompt(proj, gates, g_head_l, ymix, n_b, seq, col_q, dk, dv):
    lc = PROMPT_CHUNK
    nch = seq // lc
    qk_w = N_HEADS * dk
    v_w = N_HEADS * dv
    row = lambda b, c: b * nch + c
    assert col_q % qk_w == 0 and (col_q + 2 * qk_w) % v_w == 0
    bq = col_q // qk_w
    bv = (col_q + 2 * qk_w) // v_w
    y, c_out, n_out, m_out = pl.pallas_call(
        _mlstm_prompt_kernel,
        grid=(n_b, nch),
        in_specs=[
            pl.BlockSpec((lc, qk_w), lambda b, c: (row(b, c), bq)),
            pl.BlockSpec((lc, qk_w), lambda b, c: (row(b, c), bq + 1)),
            pl.BlockSpec((lc, v_w), lambda b, c: (row(b, c), bv)),
            pl.BlockSpec((lc, v_w), lambda b, c: (row(b, c), bv + 1)),
            pl.BlockSpec((lc, gates.shape[1]), lambda b, c: (row(b, c), 0)),
            pl.BlockSpec((N_HEADS, dv), lambda b, c: (0, 0)),
            pl.BlockSpec(memory_space=pl.ANY),
        ],
        out_specs=[
            pl.BlockSpec((lc, v_w), lambda b, c: (row(b, c), 1)),
            pl.BlockSpec((None, N_HEADS, dv, dk), lambda b, c: (b, 0, 0, 0)),
            pl.BlockSpec((None, N_HEADS, dk), lambda b, c: (b, 0, 0)),
            pl.BlockSpec((None, 8, 128), lambda b, c: (b, 0, 0)),
        ],
        out_shape=[jax.ShapeDtypeStruct(ymix.shape, ymix.dtype),
                   jax.ShapeDtypeStruct((n_b, N_HEADS, dv, dk), F32),
                   jax.ShapeDtypeStruct((n_b, N_HEADS, dk), F32),
                   jax.ShapeDtypeStruct((n_b, 8, 128), F32)],
        scratch_shapes=[pltpu.VMEM((N_HEADS, dv, dk), F32),
                        pltpu.VMEM((8, dk), F32),
                        pltpu.VMEM((8, 128), F32)],
        input_output_aliases={6: 0},
        compiler_params=_cparams(("arbitrary", "arbitrary")),
        name="mlstm_prompt",
    )(proj, proj, proj, proj, gates, g_head_l, ymix)
    return y, c_out, n_out, m_out[:, :N_HEADS, 0]


def _mlstm_sample_kernel(seq, has_prev, q_ref, k_ref, v_ref, o_ref, gt_ref, gh_ref, c0_ref, n0_ref, m0_ref,
                         *rest):
    y_ref, c_out_ref, n_out_ref, m_out_ref = rest[-4:]
    rows = q_ref.shape[0]
    dk = c0_ref.shape[3]
    dv = c0_ref.shape[2]
    seg_of_row = lax.broadcasted_iota(jnp.int32, (rows, 1), 0) // seq
    gates = gt_ref[...]
    log_f = _log_sigmoid(gates)
    for h in range(N_HEADS):
        q = q_ref[:, h * dk:(h + 1) * dk]
        k = (k_ref[:, h * dk:(h + 1) * dk].astype(F32) * (dk ** -0.5)).astype(BF16)
        v = v_ref[:, h * dv:(h + 1) * dv]
        ig_all = gates[:, h:h + 1]
        lf_all = log_f[:, N_HEADS + h:N_HEADS + h + 1]
        hh = jnp.zeros((rows, dv), F32)
        for sgm in range(rows // seq):
            mine = seg_of_row == sgm
            ig = jnp.where(mine, ig_all, NEG_BIG)
            lf = jnp.where(mine, lf_all, 0.0)
            h_seg, c_new, n_new, m_new = _mlstm_segment(
                q, k, v, ig, lf, c0_ref[sgm, h], n0_ref[sgm, h:h + 1, :], m0_ref[sgm, h:h + 1, :])
            hh = jnp.where(mine, h_seg, hh)
            c_out_ref[sgm, h] = c_new
            n_out_ref[sgm, h:h + 1, :] = n_new
            m_out_ref[sgm, h:h + 1, :] = m_new
        y_ref[:, h * dv:(h + 1) * dv] = _head_out(
            hh, o_ref[:, h * dv:(h + 1) * dv], gh_ref[h:h + 1, :]).astype(y_ref.dtype)


def _mlstm_sample(proj_bm, gates_bm, g_head_l, state_c, state_n, state_m, layer, c_prev, seq, col_q, dk, dv):
    depth, n_b = state_c.shape[:2]
    grp = SAMPLE_GROUP
    rows = grp * seq
    qk_w = N_HEADS * dk
    v_w = N_HEADS * dv
    bq = col_q // qk_w
    bv = (col_q + 2 * qk_w) // v_w
    has_prev = c_prev is not None
    in_specs = [
        pl.BlockSpec((rows, qk_w), lambda i: (i, bq)),
        pl.BlockSpec((rows, qk_w), lambda i: (i, bq + 1)),
        pl.BlockSpec((rows, v_w), lambda i: (i, bv)),
        pl.BlockSpec((rows, v_w), lambda i: (i, bv + 1)),
        pl.BlockSpec((rows, gates_bm.shape[1]), lambda i: (i, 0)),
        pl.BlockSpec((N_HEADS, dv), lambda i: (0, 0)),
        pl.BlockSpec((None, grp, N_HEADS, dv, dk), lambda i: (layer, i, 0, 0, 0)),
        pl.BlockSpec((None, grp, N_HEADS, dk), lambda i: (layer, i, 0, 0)),
        pl.BlockSpec((None, grp, N_HEADS, 1), lambda i: (layer, i, 0, 0)),
    ]
    args = [proj_bm, proj_bm, proj_bm, proj_bm, gates_bm, g_head_l, state_c, state_n,
            state_m.reshape(depth, n_b, N_HEADS, 1)]
    aliases = {}
    if has_prev:
        in_specs.append(pl.BlockSpec(memory_space=pl.ANY))
        args.append(c_prev)
        aliases = {len(args) - 1: 1}
    y, c_out, n_out, m_out = pl.pallas_call(
        functools.partial(_mlstm_sample_kernel, seq, has_prev),
        grid=(n_b // grp,),
        in_specs=in_specs,
        out_specs=[
            pl.BlockSpec((rows, v_w), lambda i: (i, 0)),
            pl.BlockSpec((None, grp, N_HEADS, dv, dk), lambda i: (layer, i, 0, 0, 0)),
            pl.BlockSpec((grp, N_HEADS, dk), lambda i: (i, 0, 0)),
            pl.BlockSpec((grp, N_HEADS, 1), lambda i: (i, 0, 0)),
        ],
        out_shape=[jax.ShapeDtypeStruct((n_b * seq, v_w), BF16),
                   jax.ShapeDtypeStruct(state_c.shape, F32),
                   jax.ShapeDtypeStruct((n_b, N_HEADS, dk), F32),
                   jax.ShapeDtypeStruct((n_b, N_HEADS, 1), F32)],
        input_output_aliases=aliases,
        compiler_params=_cparams(("arbitrary",)),
        name="mlstm_sample",
    )(*args)
    return y, c_out, n_out, m_out[:, :, 0]


def _to_time_major(a, n_b, seq):
    return jnp.transpose(a.reshape(n_b, seq, -1), (1, 0, 2)).reshape(n_b * seq, -1)


def _to_batch_major(a, n_b, seq):
    return jnp.transpose(a.reshape(seq, n_b, -1), (1, 0, 2)).reshape(n_b * seq, -1)


def kernel(x_prompt, x_sample, state_pool, state_C, state_n, state_m, c_prompt, c_sample,
           w_ada, b_ada, g_norm, w_in, b_in, w_pool, s_pool, g_head, w_out, w1, w3, w2, g_final):
    n_bp, seq_p, d = x_prompt.shape
    n_bs, seq_s, _ = x_sample.shape
    depth = w_ada.shape[0]
    pw = s_pool.shape[-1]
    dv = g_head.shape[-1]
    dk = state_C.shape[-1]
    n_hist = state_pool.shape[2]
    rows_p = n_bp * seq_p
    rows_s = n_bs * seq_s
    n_main = pw + 2 * N_HEADS * dk + 2 * N_HEADS * dv

    xp0 = x_prompt.reshape(rows_p, d)
    xs0 = _to_time_major(x_sample, n_bs, seq_s)

    c_all = jnp.concatenate([c_sample, c_prompt, jnp.zeros((8 - n_bp, d), F32)], axis=0)
    b_ada3 = b_ada.reshape(depth, 1, b_ada.shape[-1])
    tiles_per_slab = d // ADA_TILE
    tiles_per_layer = N_SUB * N_MOD * tiles_per_slab
    assert depth == 2
    mods = _ada_head(c_all, w_ada, b_ada3, N_MOD)
    ada_jobs = {(0, 0): (N_MOD * tiles_per_slab, tiles_per_layer - N_MOD * tiles_per_slab),
                (0, 1): (tiles_per_layer, tiles_per_layer)}

    def prompt_rows(mods):
        return mods[:, :, n_bs:n_bs + n_bp][:, :, :, None, :]

    b_in3 = b_in.reshape(depth, 1, b_in.shape[-1])
    w_in_t = jnp.transpose(w_in, (0, 2, 1))
    pool_p, c_p, n_p, m_p = [], [], [], []
    n_s, m_s = [], []
    c_s_all = None
    hist_s_tm = None
    state_pool_tm = jnp.transpose(state_pool, (0, 2, 1, 3))

    def ffn(x, h, mods, l, sub_layer, ffn_idx):
        job = ada_jobs.get((l, ffn_idx))
        if job is None:
            act = _swiglu_up(h, w1, w3, l, ffn_idx)
        else:
            act, mods = _swiglu_up(h, w1, w3, l, ffn_idx, (c_all, w_ada, b_ada3, mods) + job)
        x = _acc_resid(act, w2, (l, ffn_idx), x, mods, prompt_rows(mods), l, sub_layer * N_MOD + 2,
                       rows_s, FFN_RES)
        return x, mods

    x = None
    for l in range(depth):
        mods_p = prompt_rows(mods)
        if l == 0:
            h, x = _normmod(xp0, xs0, 0, g_norm[l, 0], mods, mods_p, l, 0, seq_p, True)
        else:
            h, = _normmod(x, x, rows_p, g_norm[l, 0], mods, mods_p, l, 0, seq_p, False)
        x, mods = ffn(x, h, mods, l, 0, 0)
        mods_p = prompt_rows(mods)

        h, = _normmod(x, x, rows_p, g_norm[l, 1], mods, mods_p, l, 1, seq_p, False)
        proj = _in_proj(h, w_in_t, b_in3, l, n_main)
        gates = _gate_proj(h, w_in_t, b_in3, l, n_main)

        ymix, hist_p = _pool_prompt(proj, w_pool[l], s_pool[l], n_bp, seq_p, d)
        ymix, cp, np_, mp = _mlstm_prompt(proj, gates, g_head[l], ymix, n_bp, seq_p, pw, dk, dv)
        ymix, hist_s_tm = _pool_sample(state_pool_tm, l, hist_s_tm, proj, w_pool[l], s_pool[l], ymix, rows_p)
        proj_bm = _to_batch_major(proj[rows_p:], n_bs, seq_s)
        gates_bm = _to_batch_major(gates[rows_p:], n_bs, seq_s)
        y_ms, c_s_all, ns, ms = _mlstm_sample(proj_bm, gates_bm, g_head[l], state_C, state_n, state_m,
                                              l, c_s_all, seq_s, pw, dk, dv)
        ymix = lax.dynamic_update_slice(ymix, _to_time_major(y_ms, n_bs, seq_s), (rows_p, pw))
        x = _acc_resid(ymix, w_out, (l,), x, mods, mods_p, l, 1 * N_MOD + 2, rows_s, 1.0)

        pool_p.append(hist_p); c_p.append(cp); n_p.append(np_); m_p.append(mp)
        n_s.append(ns); m_s.append(ms)

        h, = _normmod(x, x, rows_p, g_norm[l, 2], mods, mods_p, l, 2, seq_p, False)
        x, mods = ffn(x, h, mods, l, 2, 1)

    y_prompt = _final_norm(x, g_final, 0, rows_p).reshape(n_bp, seq_p, d)
    y_sample = _to_batch_major(_final_norm(x, g_final, rows_p, rows_s), n_bs, seq_s).reshape(n_bs, seq_s, d)
    return (y_prompt, y_sample,
            jnp.stack(pool_p), jnp.stack(c_p), jnp.stack(n_p), jnp.stack(m_p),
            jnp.transpose(hist_s_tm, (0, 2, 1, 3)), c_s_all, jnp.stack(n_s), jnp.stack(m_s))
```

```python
import functools

import jax
import jax.numpy as jnp
from jax import lax
from jax.experimental import pallas as pl
from jax.experimental.pallas import tpu as pltpu

F32 = jnp.float32
BF16 = jnp.bfloat16

EPS = 1e-6
FFN_RES = 0.5
POOL_WINDOWS = (2, 4, 8, 16)
N_HEADS = 4
N_SUB = 3
N_MOD = 3
PROMPT_CHUNK = 256
SAMPLE_GROUP = 4
NEG_BIG = -1e30

VMEM_LIMIT = 56 * 1024 * 1024

ROW_TILE = 256
MM_ROW_TILE = 2176
ACC_ROW_TILE = 2176
ACC_SUB_ROWS = 544
ACC_VMEM_LIMIT = 60 * 1024 * 1024
ACC_K_TILE = 1024
ACC_N_TILE = 1024


def _cparams(sem):
    return pltpu.CompilerParams(dimension_semantics=sem, vmem_limit_bytes=VMEM_LIMIT)


def _dot(a, b):
    return lax.dot_general(a, b, (((1,), (0,)), ((), ())), preferred_element_type=F32)


def _dot_nt(a, b):
    return lax.dot_general(a, b, (((1,), (1,)), ((), ())), preferred_element_type=F32)


def _dot_tn(a, b):
    return lax.dot_general(a, b, (((0,), (0,)), ((), ())), preferred_element_type=F32)


def _ada_kernel(c_ref, w_ref, b_ref, o_ref):
    c = c_ref[...]
    sc = (c * jax.nn.sigmoid(c)).astype(BF16)
    o_ref[...] = _dot(sc, w_ref[...]) + b_ref[...]


def _ada_head(c_all, w_ada, b_ada3, n_slabs, tn=1024):
    depth, d, n = w_ada.shape
    rows = c_all.shape[0]
    per = d // tn
    return pl.pallas_call(
        _ada_kernel,
        grid=(n_slabs * per,),
        in_specs=[
            pl.BlockSpec((rows, d), lambda t: (0, 0)),
            pl.BlockSpec((None, d, tn), lambda t: (0, 0, t)),
            pl.BlockSpec((None, 1, tn), lambda t: (0, 0, t)),
        ],
        out_specs=pl.BlockSpec((None, None, rows, tn), lambda t: (0, t // per, 0, t % per)),
        out_shape=jax.ShapeDtypeStruct((depth, n // d, rows, d), F32),
        compiler_params=_cparams(("arbitrary",)),
        name="ada_mods",
    )(c_all, w_ada, b_ada3)


def _rms(x, g):
    return x * lax.rsqrt(jnp.mean(x * x, axis=-1, keepdims=True) + EPS) * g


NORM_CHUNK = 8


def _normmod_kernel(n_prompt_tiles, emit_x, xp_ref, xs_ref, g_ref, shp_ref, scp_ref, shs_ref, scs_ref,
                    h_ref, *rest):
    xo_ref = rest[0] if emit_x else None
    gm_ref = rest[-1]
    i = pl.program_id(0)
    n_bs = shs_ref.shape[0]
    n_chunks = h_ref.shape[0] // NORM_CHUNK

    def run(x_ref, mod_rows):
        def chunk(c, carry):
            r = pl.multiple_of(c * NORM_CHUNK, NORM_CHUNK)
            rows = pl.ds(r, NORM_CHUNK)
            x = x_ref[rows, :]
            rstd = lax.rsqrt(jnp.mean(x * x, axis=-1, keepdims=True) + EPS)
            gm, sh = mod_rows(r)
            h_ref[rows, :] = (x * rstd * gm + sh).astype(h_ref.dtype)
            if emit_x:
                xo_ref[rows, :] = x
            return carry

        lax.fori_loop(0, n_chunks, chunk, 0, unroll=4)

    @pl.when(i < n_prompt_tiles)
    def _():
        gm_ref[0:1, :] = g_ref[...] * (1.0 + scp_ref[...])
        run(xp_ref, lambda r: (gm_ref[0:1, :], shp_ref[...]))

    @pl.when(i >= n_prompt_tiles)
    def _():
        gm_ref[...] = g_ref[...] * (1.0 + scs_ref[...])

        def mod_rows(r):
            rb = pl.ds(pl.multiple_of(lax.rem(r, n_bs), NORM_CHUNK), NORM_CHUNK)
            return gm_ref[rb, :], shs_ref[rb, :]

        run(xs_ref, mod_rows)


def _normmod(xp, xs, xs_row0, g, mods, mods_p, layer, sub, seq, emit_x):
    d = xp.shape[1]
    tr = ROW_TILE if emit_x else 2 * ROW_TILE
    n_bp = mods_p.shape[2]
    n_bs = mods.shape[2] - 8
    rows_p = n_bp * seq
    rows_s = xs.shape[0] - xs_row0
    m = rows_p + rows_s
    n_pt = rows_p // tr
    tiles_per_seq = seq // tr
    s0 = xs_row0 // tr
    i_shift, i_scale = sub * N_MOD, sub * N_MOD + 1
    p_map = lambda which: (lambda i: (layer, which, jnp.minimum(i // tiles_per_seq, n_bp - 1), 0, 0))
    s_map = lambda which: (lambda i: (layer, which, 0, 0))
    out_specs = [pl.BlockSpec((tr, d), lambda i: (i, 0))]
    out_shape = [jax.ShapeDtypeStruct((m, d), BF16)]
    if emit_x:
        out_specs.append(pl.BlockSpec((tr, d), lambda i: (i, 0)))
        out_shape.append(jax.ShapeDtypeStruct((m, d), F32))
    return pl.pallas_call(
        functools.partial(_normmod_kernel, n_pt, emit_x),
        grid=(m // tr,),
        in_specs=[
            pl.BlockSpec((tr, d), lambda i: (jnp.minimum(i, n_pt - 1), 0)),
            pl.BlockSpec((tr, d), lambda i: (s0 + jnp.maximum(i - n_pt, 0), 0)),
            pl.BlockSpec((1, d), lambda i: (0, 0)),
            pl.BlockSpec((None, None, None, 1, d), p_map(i_shift)),
            pl.BlockSpec((None, None, None, 1, d), p_map(i_scale)),
            pl.BlockSpec((None, None, n_bs, d), s_map(i_shift)),
            pl.BlockSpec((None, None, n_bs, d), s_map(i_scale)),
        ],
        out_specs=out_specs,
        out_shape=out_shape,
        scratch_shapes=[pltpu.VMEM((n_bs, d), F32)],
        compiler_params=_cparams(("arbitrary",)),
        name="normmod",
    )(xp, xs, g.reshape(1, d), mods_p, mods_p, mods, mods)


def _final_norm_kernel(x_ref, g_ref, o_ref):
    def chunk(c, carry):
        rows = pl.ds(pl.multiple_of(c * NORM_CHUNK, NORM_CHUNK), NORM_CHUNK)
        o_ref[rows, :] = _rms(x_ref[rows, :], g_ref[...])
        return carry

    lax.fori_loop(0, o_ref.shape[0] // NORM_CHUNK, chunk, 0, unroll=4)


def _final_norm(x, g, row0, rows):
    d = x.shape[1]
    tr = ROW_TILE
    b0 = row0 // tr
    return pl.pallas_call(
        _final_norm_kernel,
        grid=(rows // tr,),
        in_specs=[pl.BlockSpec((tr, d), lambda i: (i + b0, 0)),
                  pl.BlockSpec((1, d), lambda i: (0, 0))],
        out_specs=pl.BlockSpec((tr, d), lambda i: (i, 0)),
        out_shape=jax.ShapeDtypeStruct((rows, d), F32),
        compiler_params=_cparams(("arbitrary",)),
        name="final_norm",
    )(x, g.reshape(1, d))


ADA_TILE = 256


def _swiglu_up_kernel(n_full, tail, ada_tiles, x_ref, w1_ref, w3_ref, *rest):
    i = pl.program_id(0)
    j = pl.program_id(1)
    if ada_tiles:
        c_ref, wa_ref, ba_ref, _, o_ref, mods_ref, sc_ref = rest

        @pl.when(jnp.logical_and(i == 0, j == 0))
        def _():
            c = c_ref[...]
            sc_ref[...] = (c * jax.nn.sigmoid(c)).astype(BF16)
    else:
        o_ref, = rest

    def body(cols, with_ada):
        x = x_ref[...]
        a = _dot(x, w1_ref[:, 0:cols])
        b = _dot(x, w3_ref[:, 0:cols])
        if with_ada:
            mods_ref[...] = _dot(sc_ref[...], wa_ref[...]) + ba_ref[...]
        o_ref[:, 0:cols] = (a * jax.nn.sigmoid(a) * b).astype(o_ref.dtype)

    def step(with_ada):
        if tail == 0:
            body(o_ref.shape[1], with_ada)
        else:
            @pl.when(j < n_full)
            def _():
                body(o_ref.shape[1], with_ada)

            @pl.when(j == n_full)
            def _():
                body(tail, with_ada)

    if ada_tiles:
        n_steps = pl.num_programs(0) * pl.num_programs(1)
        s = i * pl.num_programs(1) + j
        fresh = jnp.logical_or(
            s == 0, (s * ada_tiles) // n_steps != ((jnp.maximum(s, 1) - 1) * ada_tiles) // n_steps)

        @pl.when(fresh)
        def _():
            step(True)

        @pl.when(jnp.logical_not(fresh))
        def _():
            step(False)
    else:
        step(False)


def _swiglu_up(h, w1, w3, layer, sub, ada=None, tn=256):
    m, d = h.shape
    f = w1.shape[-1]
    tm = MM_ROW_TILE
    n_full, tail = divmod(f, tn)
    nj = pl.cdiv(f, tn)
    grid = (m // tm, nj)
    w_spec = pl.BlockSpec((None, None, d, tn), lambda i, j: (layer, sub, 0, j))
    in_specs = [pl.BlockSpec((tm, d), lambda i, j: (i, 0), pipeline_mode=pl.Buffered(1)), w_spec, w_spec]
    args = [h, w1, w3]
    out_specs = [pl.BlockSpec((tm, tn), lambda i, j: (i, j))]
    out_shape = [jax.ShapeDtypeStruct((m, f), BF16)]
    scratch, aliases = [], {}
    if ada is not None:
        c_all, w_ada, b_ada3, mods, first_tile, n_tiles = ada
        n_steps = grid[0] * nj
        per_layer = w_ada.shape[-1] // ADA_TILE
        per_slab = d // ADA_TILE
        assert n_tiles <= n_steps

        def tile(i, j):
            return first_tile + ((i * nj + j) * n_tiles) // n_steps

        in_specs += [
            pl.BlockSpec(c_all.shape, lambda i, j: (0, 0), pipeline_mode=pl.Buffered(1)),
            pl.BlockSpec((None, d, ADA_TILE), lambda i, j: (tile(i, j) // per_layer, 0, tile(i, j) % per_layer)),
            pl.BlockSpec((None, 1, ADA_TILE), lambda i, j: (tile(i, j) // per_layer, 0, tile(i, j) % per_layer)),
            pl.BlockSpec(memory_space=pl.ANY),
        ]
        args += [c_all, w_ada, b_ada3, mods]
        out_specs.append(pl.BlockSpec(
            (None, None, c_all.shape[0], ADA_TILE),
            lambda i, j: (tile(i, j) // per_layer, (tile(i, j) % per_layer) // per_slab, 0,
                          tile(i, j) % per_slab)))
        out_shape.append(jax.ShapeDtypeStruct(mods.shape, mods.dtype))
        scratch = [pltpu.VMEM(c_all.shape, BF16)]
        aliases = {len(args) - 1: 1}
    res = pl.pallas_call(
        functools.partial(_swiglu_up_kernel, n_full, tail, 0 if ada is None else ada[5]),
        grid=grid,
        in_specs=in_specs,
        out_specs=out_specs,
        out_shape=out_shape,
        scratch_shapes=scratch,
        input_output_aliases=aliases,
        compiler_params=_cparams(("arbitrary", "arbitrary")),
        name="swiglu_up",
    )(*args)
    return res if ada is not None else res[0]


def _proj_kernel(x_ref, w_ref, b_ref, o_ref):
    o_ref[...] = (_dot_nt(x_ref[...], w_ref[...]) + b_ref[...]).astype(o_ref.dtype)


def _in_proj(h, w_in_t, b_in3, layer, n_main, tn=512):
    m, d = h.shape
    tm = MM_ROW_TILE
    return pl.pallas_call(
        _proj_kernel,
        grid=(m // tm, n_main // tn),
        in_specs=[pl.BlockSpec((tm, d), lambda i, j: (i, 0), pipeline_mode=pl.Buffered(1)),
                  pl.BlockSpec((None, tn, d), lambda i, j: (layer, j, 0)),
                  pl.BlockSpec((None, 1, tn), lambda i, j: (layer, 0, j))],
        out_specs=pl.BlockSpec((tm, tn), lambda i, j: (i, j)),
        out_shape=jax.ShapeDtypeStruct((m, n_main), BF16),
        compiler_params=_cparams(("arbitrary", "arbitrary")),
        name="in_proj",
    )(h, w_in_t, b_in3)


def _gate_proj(h, w_in_t, b_in3, layer, n_main, tn=128):
    m, d = h.shape
    tm = MM_ROW_TILE
    jb = n_main // tn
    return pl.pallas_call(
        _proj_kernel,
        grid=(m // tm,),
        in_specs=[pl.BlockSpec((tm, d), lambda i: (i, 0)),
                  pl.BlockSpec((None, tn, d), lambda i: (layer, jb, 0)),
                  pl.BlockSpec((None, 1, tn), lambda i: (layer, 0, jb))],
        out_specs=pl.BlockSpec((tm, tn), lambda i: (i, 0)),
        out_shape=jax.ShapeDtypeStruct((m, tn), F32),
        compiler_params=_cparams(("arbitrary",)),
        name="gate_proj",
    )(h, w_in_t, b_in3)


def _gate_segments(tm, n_tiles, n_bp, seq, n_bs):
    rows_p = n_bp * seq
    tiles = []
    for t in range(n_tiles):
        segs, r = [], t * tm
        while r < (t + 1) * tm:
            if r < rows_p:
                b = r // seq
                end = min((b + 1) * seq, (t + 1) * tm)
                segs.append((r - t * tm, end - t * tm, b))
            else:
                end = r + n_bs
                assert (r - rows_p) % n_bs == 0 and end <= (t + 1) * tm
                segs.append((r - t * tm, end - t * tm, None))
            r = end
        tiles.append(segs)
    return tiles


def _acc_resid_kernel(segments, k_last_valid, res_scale, a_ref, w_ref, x_ref, gp_ref, gs_ref, o_ref):
    i = pl.program_id(0)
    k = pl.program_id(2)
    nk = pl.num_programs(2)
    tm, tk = a_ref.shape

    def sweep(kv, first):
        for r0 in range(0, tm, ACC_SUB_ROWS):
            rs = slice(r0, r0 + ACC_SUB_ROWS)
            part = _dot(a_ref[rs, 0:kv], w_ref[0:kv, :])
            if first:
                o_ref[rs, :] = part
            else:
                o_ref[rs, :] += part

    @pl.when(k == 0)
    def _():
        sweep(tk, True)

    @pl.when(jnp.logical_and(k > 0, k < nk - 1))
    def _():
        sweep(tk, False)

    for t, segs in enumerate(segments):
        @pl.when(jnp.logical_and(k == nk - 1, i == t))
        def _():
            for r0 in range(0, tm, ACC_SUB_ROWS):
                r1 = r0 + ACC_SUB_ROWS
                part = _dot(a_ref[r0:r1, 0:k_last_valid], w_ref[0:k_last_valid, :])
                for s0, s1, b in segs:
                    q0, q1 = max(s0, r0), min(s1, r1)
                    if q0 >= q1:
                        continue
                    gate = gs_ref[q0 - s0:q1 - s0, :] if b is None else gp_ref[b]
                    o_ref[q0:q1, :] = (x_ref[q0:q1, :]
                                       + (res_scale * gate) * (o_ref[q0:q1, :] + part[q0 - r0:q1 - r0, :]))


def _proj_resid_kernel(segments, res_scale, a_ref, w_ref, x_ref, gp_ref, gs_ref, o_ref):
    i = pl.program_id(0)
    for t, segs in enumerate(segments):
        @pl.when(i == t)
        def _():
            y = _dot(a_ref[...], w_ref[...])
            for s0, s1, b in segs:
                gate = gs_ref[...] if b is None else gp_ref[b]
                o_ref[s0:s1, :] = x_ref[s0:s1, :] + (res_scale * gate) * y[s0:s1, :]


def _proj_resid(a, w_full, layer, x, mods, mods_p, which, sample_rows, res_scale, tn=512):
    m, kdim = a.shape
    d = x.shape[1]
    tm = MM_ROW_TILE
    n_bp = mods_p.shape[2]
    n_bs = mods.shape[2] - 8
    seq = (m - sample_rows) // n_bp
    segments = _gate_segments(tm, m // tm, n_bp, seq, n_bs)
    return pl.pallas_call(
        functools.partial(_proj_resid_kernel, segments, res_scale),
        grid=(m // tm, d // tn),
        in_specs=[
            pl.BlockSpec((tm, kdim), lambda i, j: (i, 0), pipeline_mode=pl.Buffered(1)),
            pl.BlockSpec((None, kdim, tn), lambda i, j: (layer, 0, j)),
            pl.BlockSpec((tm, tn), lambda i, j: (i, j)),
            pl.BlockSpec((None, None, n_bp, 1, tn), lambda i, j: (layer, which, 0, 0, j)),
            pl.BlockSpec((None, None, n_bs, tn), lambda i, j: (layer, which, 0, j)),
        ],
        out_specs=pl.BlockSpec((tm, tn), lambda i, j: (i, j)),
        out_shape=jax.ShapeDtypeStruct((m, d), F32),
        compiler_params=pltpu.CompilerParams(
            dimension_semantics=("arbitrary", "arbitrary"), vmem_limit_bytes=ACC_VMEM_LIMIT),
        name="proj_resid",
    )(a, w_full, x, mods_p, mods)


def _acc_resid(a, w_full, w_index, x, mods, mods_p, layer, which, sample_rows, res_scale):
    m, kdim = a.shape
    d = x.shape[1]
    tm, tk, tn = ACC_ROW_TILE, ACC_K_TILE, ACC_N_TILE
    n_bp = mods_p.shape[2]
    n_bs = mods.shape[2] - 8
    seq = (m - sample_rows) // n_bp
    assert m % tm == 0 and tm % ACC_SUB_ROWS == 0
    nk = pl.cdiv(kdim, tk)
    assert nk >= 3
    k_last_valid = kdim - (nk - 1) * tk
    lead = (None,) * len(w_index)
    segments = _gate_segments(tm, m // tm, n_bp, seq, n_bs)
    return pl.pallas_call(
        functools.partial(_acc_resid_kernel, segments, k_last_valid, res_scale),
        grid=(m // tm, d // tn, nk),
        in_specs=[
            pl.BlockSpec((tm, tk), lambda i, j, k: (i, k)),
            pl.BlockSpec(lead + (tk, tn), lambda i, j, k: tuple(w_index) + (k, j)),
            pl.BlockSpec((tm, tn), lambda i, j, k: (i, j)),
            pl.BlockSpec((None, None, n_bp, 1, tn), lambda i, j, k: (layer, which, 0, 0, j)),
            pl.BlockSpec((None, None, n_bs, tn), lambda i, j, k: (layer, which, 0, j)),
        ],
        out_specs=pl.BlockSpec((tm, tn), lambda i, j, k: (i, j)),
        out_shape=jax.ShapeDtypeStruct((m, d), F32),
        compiler_params=pltpu.CompilerParams(
            dimension_semantics=("arbitrary", "arbitrary", "arbitrary"),
            vmem_limit_bytes=ACC_VMEM_LIMIT),
        name="acc_resid",
    )(a, w_full, x, mods_p, mods)


N_HIST = max(POOL_WINDOWS) - 1
HALO = 32
LEVEL_ROW0 = 8


def _pool_prompt_kernel(start, u_ref, wp_ref, sp_ref, y_ref, hist_ref, z_ref, sa_ref, sb_ref):
    t = pl.program_id(1)
    nt = pl.num_programs(1)
    tt = u_ref.shape[0]
    group = wp_ref.shape[1]
    n = HALO + tt
    n_hist = hist_ref.shape[0]

    @pl.when(t == 0)
    def _():
        z_ref[0:HALO, :] = jnp.zeros((HALO, z_ref.shape[1]), F32)
        sa_ref[0:LEVEL_ROW0, :] = jnp.zeros((LEVEL_ROW0, group), F32)
        sb_ref[0:LEVEL_ROW0, :] = jnp.zeros((LEVEL_ROW0, group), F32)

    z_ref[HALO:n, :] = u_ref[...].astype(F32)

    def window_sum(cols, w):
        src, src_cols, sh, level = z_ref, cols, 1, 0
        while 2 * sh < w:
            dst = (sa_ref, sb_ref)[level % 2]
            dst[LEVEL_ROW0:n, :] = src[LEVEL_ROW0:n, src_cols] + src[LEVEL_ROW0 - sh:n - sh, src_cols]
            src, src_cols, sh, level = dst, slice(None), 2 * sh, level + 1
        return src[HALO:n, src_cols] + src[HALO - sh:n - sh, src_cols]

    pos = start + t * tt + lax.broadcasted_iota(jnp.int32, (tt, 1), 0)
    for g, w in enumerate(POOL_WINDOWS):
        assert w & (w - 1) == 0 and LEVEL_ROW0 + 2 * (w // 2 - 1) <= HALO
        cols = slice(g * group, (g + 1) * group)
        cur = z_ref[HALO:n, cols]
        cnt = jnp.minimum(w, pos + 1).astype(F32)
        pooled = window_sum(cols, w) / cnt - cur
        y = _dot(pooled.astype(BF16), wp_ref[g]) * sp_ref[:, cols]
        y_ref[:, cols] = y.astype(y_ref.dtype)

    @pl.when(t == nt - 1)
    def _():
        hist_ref[...] = z_ref[n - n_hist:n, :]

    z_ref[0:HALO, :] = z_ref[tt:n, :]


def _pool_prompt(proj, w_pool_l, s_pool_l, n_b, seq, d, tt=256):
    pw = s_pool_l.shape[-1]
    ntt = seq // tt
    n_groups, group, _ = w_pool_l.shape
    return pl.pallas_call(
        functools.partial(_pool_prompt_kernel, 0),
        grid=(n_b, ntt),
        in_specs=[
            pl.BlockSpec((tt, pw), lambda b, t: (b * ntt + t, 0)),
            pl.BlockSpec((n_groups, group, group), lambda b, t: (0, 0, 0)),
            pl.BlockSpec((1, pw), lambda b, t: (0, 0)),
        ],
        out_specs=[
            pl.BlockSpec((tt, pw), lambda b, t: (b * ntt + t, 0)),
            pl.BlockSpec((None, N_HIST, pw), lambda b, t: (b, 0, 0)),
        ],
        out_shape=[jax.ShapeDtypeStruct((proj.shape[0], d), BF16),
                   jax.ShapeDtypeStruct((n_b, N_HIST, pw), F32)],
        scratch_shapes=[pltpu.VMEM((HALO + tt, pw), F32),
                        pltpu.VMEM((HALO + tt, group), F32),
                        pltpu.VMEM((HALO + tt, group), F32)],
        compiler_params=_cparams(("arbitrary", "arbitrary")),
        name="pool_prompt",
    )(proj, w_pool_l, s_pool_l.reshape(1, pw))


def _pool_sample_kernel(hist_ref, u_ref, wp_ref, sp_ref, *rest):
    y_ref, nh_ref = rest[-2:]
    n_hist, n_b, _ = hist_ref.shape
    n_t = u_ref.shape[0] // n_b
    g = pl.program_id(0)

    def z(r):
        if r < n_hist:
            return hist_ref[r]
        return u_ref[(r - n_hist) * n_b:(r - n_hist + 1) * n_b, :].astype(F32)

    for r in range(n_hist):
        nh_ref[r] = z(r + n_t)

    for t in range(n_t):
        cur = z(n_hist + t)
        run = cur
        sums = []
        for j in range(1, POOL_WINDOWS[-1]):
            run = run + z(n_hist + t - j)
            if j + 1 in POOL_WINDOWS:
                sums.append(run)
        pooled = jnp.zeros_like(cur)
        for gi, w in enumerate(POOL_WINDOWS):
            pooled = jnp.where(g == gi, sums[gi] / float(w) - cur, pooled)
        y = _dot(pooled.astype(BF16), wp_ref[...]) * sp_ref[...]
        y_ref[t * n_b:(t + 1) * n_b, :] = y.astype(y_ref.dtype)


def _pool_sample(state_tm, layer, hist_prev, proj, w_pool_l, s_pool_l, ymix, row0):
    depth, n_hist, n_b, pw = state_tm.shape
    n_groups, group, _ = w_pool_l.shape
    rows = proj.shape[0] - row0
    rb = row0 // rows
    args = [state_tm, proj, w_pool_l, s_pool_l.reshape(1, pw), ymix]
    in_specs = [
        pl.BlockSpec((None, n_hist, n_b, group), lambda g: (layer, 0, 0, g)),
        pl.BlockSpec((rows, group), lambda g: (rb, g)),
        pl.BlockSpec((None, group, group), lambda g: (g, 0, 0)),
        pl.BlockSpec((1, group), lambda g: (0, g)),
        pl.BlockSpec(memory_space=pl.ANY),
    ]
    aliases = {4: 0}
    if hist_prev is not None:
        args.append(hist_prev)
        in_specs.append(pl.BlockSpec(memory_space=pl.ANY))
        aliases[5] = 1
    return pl.pallas_call(
        _pool_sample_kernel,
        grid=(n_groups,),
        in_specs=in_specs,
        out_specs=[pl.BlockSpec((rows, group), lambda g: (rb, g)),
                   pl.BlockSpec((None, n_hist, n_b, group), lambda g: (layer, 0, 0, g))],
        out_shape=[jax.ShapeDtypeStruct(ymix.shape, ymix.dtype),
                   jax.ShapeDtypeStruct(state_tm.shape, F32)],
        input_output_aliases=aliases,
        compiler_params=_cparams(("arbitrary",)),
        name="pool_sample",
    )(*args)


def _mlstm_segment(q, k, v, ig, lf, c_state, n_state, m_state):
    r = q.shape[0]
    row = lax.broadcasted_iota(jnp.int32, (r, r), 0)
    col = lax.broadcasted_iota(jnp.int32, (r, r), 1)
    causal = col <= row
    lf_rows = jnp.sum(jnp.where(row == col, lf, 0.0), axis=0, keepdims=True)
    ig_rows = jnp.sum(jnp.where(row == col, ig, 0.0), axis=0, keepdims=True)
    b_col = jnp.sum(jnp.where(causal, lf_rows, 0.0), axis=1, keepdims=True)
    b_rows = jnp.sum(jnp.where(row <= col, lf, 0.0), axis=0, keepdims=True)
    dmat = jnp.where(causal, b_col - b_rows + ig_rows, -jnp.inf)
    inter = b_col + m_state
    m_tok = jnp.maximum(inter, jnp.max(dmat, axis=-1, keepdims=True))
    w_intra = jnp.exp(dmat - m_tok)
    w_inter = jnp.exp(inter - m_tok)
    s = _dot_nt(q, k) * w_intra
    num = _dot(s.astype(BF16), v) + w_inter * _dot_nt(q, c_state)
    qn = jnp.sum(q.astype(F32) * n_state, axis=-1, keepdims=True)
    den = jnp.sum(s, axis=-1, keepdims=True) + w_inter * qn
    h = num * (1.0 / jnp.maximum(jnp.abs(den), jnp.exp(-m_tok)))
    b_last = jnp.sum(lf, axis=0, keepdims=True)
    dec = b_last - b_col + ig
    m_new = jnp.maximum(b_last + m_state, jnp.max(dec, axis=0, keepdims=True))
    ws = jnp.exp(dec - m_new)
    wc = jnp.exp(b_last + m_state - m_new)
    wk = ws * k.astype(F32)
    c_new = wc * c_state + _dot_tn(v, wk.astype(BF16))
    n_new = wc * n_state + jnp.sum(wk, axis=0, keepdims=True)
    return h, c_new, n_new, m_new


def _head_out(h, o, g_head):
    hn = h * lax.rsqrt(jnp.mean(h * h, axis=-1, keepdims=True) + EPS) * g_head
    return hn * jax.nn.sigmoid(o.astype(F32))


def _log_sigmoid(x):
    return jnp.minimum(x, 0.0) - jnp.log1p(jnp.exp(-jnp.abs(x)))


def _mlstm_prompt_kernel(q_ref, k_ref, v_ref, o_ref, gt_ref, gh_ref, ymix_in_ref,
                         y_ref, c_out_ref, n_out_ref, m_out_ref, c_s, n_s, m_s):
    del ymix_in_ref
    c = pl.program_id(1)
    nc = pl.num_programs(1)
    dk = c_s.shape[2]
    dv = c_s.shape[1]

    @pl.when(c == 0)
    def _():
        c_s[...] = jnp.zeros(c_s.shape, F32)
        n_s[...] = jnp.zeros(n_s.shape, F32)
        m_s[...] = jnp.zeros(m_s.shape, F32)

    gates = gt_ref[...]
    log_f = _log_sigmoid(gates)
    for h in range(N_HEADS):
        q = q_ref[:, h * dk:(h + 1) * dk]
        k = (k_ref[:, h * dk:(h + 1) * dk].astype(F32) * (dk ** -0.5)).astype(BF16)
        v = v_ref[:, h * dv:(h + 1) * dv]
        ig = gates[:, h:h + 1]
        lf = log_f[:, N_HEADS + h:N_HEADS + h + 1]
        hh, c_new, n_new, m_new = _mlstm_segment(
            q, k, v, ig, lf, c_s[h], n_s[h:h + 1, :], m_s[h:h + 1, 0:1])
        c_s[h] = c_new
        n_s[h:h + 1, :] = n_new
        m_s[h:h + 1, :] = jnp.broadcast_to(m_new, (1, m_s.shape[1]))
        y_ref[:, h * dv:(h + 1) * dv] = _head_out(
            hh, o_ref[:, h * dv:(h + 1) * dv], gh_ref[h:h + 1, :]).astype(y_ref.dtype)

    @pl.when(c == nc - 1)
    def _():
        c_out_ref[...] = c_s[...]
        n_out_ref[...] = n_s[0:N_HEADS, :]
        m_out_ref[...] = m_s[...]


def _mlstm_prompt(proj, gates, g_head_l, ymix, n_b, seq, col_q, dk, dv):
    lc = PROMPT_CHUNK
    nch = seq // lc
    qk_w = N_HEADS * dk
    v_w = N_HEADS * dv
    row = lambda b, c: b * nch + c
    assert col_q % qk_w == 0 and (col_q + 2 * qk_w) % v_w == 0
    bq = col_q // qk_w
    bv = (col_q + 2 * qk_w) // v_w
    y, c_out, n_out, m_out = pl.pallas_call(
        _mlstm_prompt_kernel,
        grid=(n_b, nch),
        in_specs=[
            pl.BlockSpec((lc, qk_w), lambda b, c: (row(b, c), bq)),
            pl.BlockSpec((lc, qk_w), lambda b, c: (row(b, c), bq + 1)),
            pl.BlockSpec((lc, v_w), lambda b, c: (row(b, c), bv)),
            pl.BlockSpec((lc, v_w), lambda b, c: (row(b, c), bv + 1)),
            pl.BlockSpec((lc, gates.shape[1]), lambda b, c: (row(b, c), 0)),
            pl.BlockSpec((N_HEADS, dv), lambda b, c: (0, 0)),
            pl.BlockSpec(memory_space=pl.ANY),
        ],
        out_specs=[
            pl.BlockSpec((lc, v_w), lambda b, c: (row(b, c), 1)),
            pl.BlockSpec((None, N_HEADS, dv, dk), lambda b, c: (b, 0, 0, 0)),
            pl.BlockSpec((None, N_HEADS, dk), lambda b, c: (b, 0, 0)),
            pl.BlockSpec((None, 8, 128), lambda b, c: (b, 0, 0)),
        ],
        out_shape=[jax.ShapeDtypeStruct(ymix.shape, ymix.dtype),
                   jax.ShapeDtypeStruct((n_b, N_HEADS, dv, dk), F32),
                   jax.ShapeDtypeStruct((n_b, N_HEADS, dk), F32),
                   jax.ShapeDtypeStruct((n_b, 8, 128), F32)],
        scratch_shapes=[pltpu.VMEM((N_HEADS, dv, dk), F32),
                        pltpu.VMEM((8, dk), F32),
                        pltpu.VMEM((8, 128), F32)],
        input_output_aliases={6: 0},
        compiler_params=_cparams(("arbitrary", "arbitrary")),
        name="mlstm_prompt",
    )(proj, proj, proj, proj, gates, g_head_l, ymix)
    return y, c_out, n_out, m_out[:, :N_HEADS, 0]


def _mlstm_sample_kernel(seq, has_prev, q_ref, k_ref, v_ref, o_ref, gt_ref, gh_ref, c0_ref, n0_ref, m0_ref,
                         *rest):
    y_ref, c_out_ref, n_out_ref, m_out_ref = rest[-4:]
    rows = q_ref.shape[0]
    dk = c0_ref.shape[3]
    dv = c0_ref.shape[2]
    seg_of_row = lax.broadcasted_iota(jnp.int32, (rows, 1), 0) // seq
    gates = gt_ref[...]
    log_f = _log_sigmoid(gates)
    for h in range(N_HEADS):
        q = q_ref[:, h * dk:(h + 1) * dk]
        k = (k_ref[:, h * dk:(h + 1) * dk].astype(F32) * (dk ** -0.5)).astype(BF16)
        v = v_ref[:, h * dv:(h + 1) * dv]
        ig_all = gates[:, h:h + 1]
        lf_all = log_f[:, N_HEADS + h:N_HEADS + h + 1]
        hh = jnp.zeros((rows, dv), F32)
        for sgm in range(rows // seq):
            mine = seg_of_row == sgm
            ig = jnp.where(mine, ig_all, NEG_BIG)
            lf = jnp.where(mine, lf_all, 0.0)
            h_seg, c_new, n_new, m_new = _mlstm_segment(
                q, k, v, ig, lf, c0_ref[sgm, h], n0_ref[sgm, h:h + 1, :], m0_ref[sgm, h:h + 1, :])
            hh = jnp.where(mine, h_seg, hh)
            c_out_ref[sgm, h] = c_new
            n_out_ref[sgm, h:h + 1, :] = n_new
            m_out_ref[sgm, h:h + 1, :] = m_new
        y_ref[:, h * dv:(h + 1) * dv] = _head_out(
            hh, o_ref[:, h * dv:(h + 1) * dv], gh_ref[h:h + 1, :]).astype(y_ref.dtype)


def _mlstm_sample(proj_bm, gates_bm, g_head_l, state_c, state_n, state_m, layer, c_prev, seq, col_q, dk, dv):
    depth, n_b = state_c.shape[:2]
    grp = SAMPLE_GROUP
    rows = grp * seq
    qk_w = N_HEADS * dk
    v_w = N_HEADS * dv
    bq = col_q // qk_w
    bv = (col_q + 2 * qk_w) // v_w
    has_prev = c_prev is not None
    in_specs = [
        pl.BlockSpec((rows, qk_w), lambda i: (i, bq)),
        pl.BlockSpec((rows, qk_w), lambda i: (i, bq + 1)),
        pl.BlockSpec((rows, v_w), lambda i: (i, bv)),
        pl.BlockSpec((rows, v_w), lambda i: (i, bv + 1)),
        pl.BlockSpec((rows, gates_bm.shape[1]), lambda i: (i, 0)),
        pl.BlockSpec((N_HEADS, dv), lambda i: (0, 0)),
        pl.BlockSpec((None, grp, N_HEADS, dv, dk), lambda i: (layer, i, 0, 0, 0)),
        pl.BlockSpec((None, grp, N_HEADS, dk), lambda i: (layer, i, 0, 0)),
        pl.BlockSpec((None, grp, N_HEADS, 1), lambda i: (layer, i, 0, 0)),
    ]
    args = [proj_bm, proj_bm, proj_bm, proj_bm, gates_bm, g_head_l, state_c, state_n,
            state_m.reshape(depth, n_b, N_HEADS, 1)]
    aliases = {}
    if has_prev:
        in_specs.append(pl.BlockSpec(memory_space=pl.ANY))
        args.append(c_prev)
        aliases = {len(args) - 1: 1}
    y, c_out, n_out, m_out = pl.pallas_call(
        functools.partial(_mlstm_sample_kernel, seq, has_prev),
        grid=(n_b // grp,),
        in_specs=in_specs,
        out_specs=[
            pl.BlockSpec((rows, v_w), lambda i: (i, 0)),
            pl.BlockSpec((None, grp, N_HEADS, dv, dk), lambda i: (layer, i, 0, 0, 0)),
            pl.BlockSpec((grp, N_HEADS, dk), lambda i: (i, 0, 0)),
            pl.BlockSpec((grp, N_HEADS, 1), lambda i: (i, 0, 0)),
        ],
        out_shape=[jax.ShapeDtypeStruct((n_b * seq, v_w), BF16),
                   jax.ShapeDtypeStruct(state_c.shape, F32),
                   jax.ShapeDtypeStruct((n_b, N_HEADS, dk), F32),
                   jax.ShapeDtypeStruct((n_b, N_HEADS, 1), F32)],
        input_output_aliases=aliases,
        compiler_params=_cparams(("arbitrary",)),
        name="mlstm_sample",
    )(*args)
    return y, c_out, n_out, m_out[:, :, 0]


def _to_time_major(a, n_b, seq):
    return jnp.transpose(a.reshape(n_b, seq, -1), (1, 0, 2)).reshape(n_b * seq, -1)


def _to_batch_major(a, n_b, seq):
    return jnp.transpose(a.reshape(seq, n_b, -1), (1, 0, 2)).reshape(n_b * seq, -1)


def kernel(x_prompt, x_sample, state_pool, state_C, state_n, state_m, c_prompt, c_sample,
           w_ada, b_ada, g_norm, w_in, b_in, w_pool, s_pool, g_head, w_out, w1, w3, w2, g_final):
    n_bp, seq_p, d = x_prompt.shape
    n_bs, seq_s, _ = x_sample.shape
    depth = w_ada.shape[0]
    pw = s_pool.shape[-1]
    dv = g_head.shape[-1]
    dk = state_C.shape[-1]
    n_hist = state_pool.shape[2]
    rows_p = n_bp * seq_p
    rows_s = n_bs * seq_s
    n_main = pw + 2 * N_HEADS * dk + 2 * N_HEADS * dv

    xp0 = x_prompt.reshape(rows_p, d)
    xs0 = _to_time_major(x_sample, n_bs, seq_s)

    c_all = jnp.concatenate([c_sample, c_prompt, jnp.zeros((8 - n_bp, d), F32)], axis=0)
    b_ada3 = b_ada.reshape(depth, 1, b_ada.shape[-1])
    tiles_per_slab = d // ADA_TILE
    tiles_per_layer = N_SUB * N_MOD * tiles_per_slab
    assert depth == 2
    mods = _ada_head(c_all, w_ada, b_ada3, N_MOD)
    ada_jobs = {(0, 0): (N_MOD * tiles_per_slab, tiles_per_layer - N_MOD * tiles_per_slab),
                (0, 1): (tiles_per_layer, tiles_per_layer)}

    def prompt_rows(mods):
        return mods[:, :, n_bs:n_bs + n_bp][:, :, :, None, :]

    b_in3 = b_in.reshape(depth, 1, b_in.shape[-1])
    w_in_t = jnp.transpose(w_in, (0, 2, 1))
    pool_p, c_p, n_p, m_p = [], [], [], []
    n_s, m_s = [], []
    c_s_all = None
    hist_s_tm = None
    state_pool_tm = jnp.transpose(state_pool, (0, 2, 1, 3))

    def ffn(x, h, mods, l, sub_layer, ffn_idx):
        job = ada_jobs.get((l, ffn_idx))
        if job is None:
            act = _swiglu_up(h, w1, w3, l, ffn_idx)
        else:
            act, mods = _swiglu_up(h, w1, w3, l, ffn_idx, (c_all, w_ada, b_ada3, mods) + job)
        x = _acc_resid(act, w2, (l, ffn_idx), x, mods, prompt_rows(mods), l, sub_layer * N_MOD + 2,
                       rows_s, FFN_RES)
        return x, mods

    x = None
    for l in range(depth):
        mods_p = prompt_rows(mods)
        if l == 0:
            h, x = _normmod(xp0, xs0, 0, g_norm[l, 0], mods, mods_p, l, 0, seq_p, True)
        else:
            h, = _normmod(x, x, rows_p, g_norm[l, 0], mods, mods_p, l, 0, seq_p, False)
        x, mods = ffn(x, h, mods, l, 0, 0)
        mods_p = prompt_rows(mods)

        h, = _normmod(x, x, rows_p, g_norm[l, 1], mods, mods_p, l, 1, seq_p, False)
        proj = _in_proj(h, w_in_t, b_in3, l, n_main)
        gates = _gate_proj(h, w_in_t, b_in3, l, n_main)

        ymix, hist_p = _pool_prompt(proj, w_pool[l], s_pool[l], n_bp, seq_p, d)
        ymix, cp, np_, mp = _mlstm_prompt(proj, gates, g_head[l], ymix, n_bp, seq_p, pw, dk, dv)
        ymix, hist_s_tm = _pool_sample(state_pool_tm, l, hist_s_tm, proj, w_pool[l], s_pool[l], ymix, rows_p)
        proj_bm = _to_batch_major(proj[rows_p:], n_bs, seq_s)
        gates_bm = _to_batch_major(gates[rows_p:], n_bs, seq_s)
        y_ms, c_s_all, ns, ms = _mlstm_sample(proj_bm, gates_bm, g_head[l], state_C, state_n, state_m,
                                              l, c_s_all, seq_s, pw, dk, dv)
        ymix = lax.dynamic_update_slice(ymix, _to_time_major(y_ms, n_bs, seq_s), (rows_p, pw))
        x = _proj_resid(ymix, w_out, l, x, mods, mods_p, 1 * N_MOD + 2, rows_s, 1.0)

        pool_p.append(hist_p); c_p.append(cp); n_p.append(np_); m_p.append(mp)
        n_s.append(ns); m_s.append(ms)

        h, = _normmod(x, x, rows_p, g_norm[l, 2], mods, mods_p, l, 2, seq_p, False)
        x, mods = ffn(x, h, mods, l, 2, 1)

    y_prompt = _final_norm(x, g_final, 0, rows_p).reshape(n_bp, seq_p, d)
    y_sample = _to_batch_major(_final_norm(x, g_final, rows_p, rows_s), n_bs, seq_s).reshape(n_bs, seq_s, d)
    return (y_prompt, y_sample,
            jnp.stack(pool_p), jnp.stack(c_p), jnp.stack(n_p), jnp.stack(m_p),
            jnp.transpose(hist_s_tm, (0, 2, 1, 3)), c_s_all, jnp.stack(n_s), jnp.stack(m_s))
```

```python
import functools

import jax
import jax.numpy as jnp
from jax import lax
from jax.experimental import pallas as pl
from jax.experimental.pallas import tpu as pltpu

F32 = jnp.float32
BF16 = jnp.bfloat16

EPS = 1e-6
FFN_RES = 0.5
POOL_WINDOWS = (2, 4, 8, 16)
N_HEADS = 4
N_SUB = 3
N_MOD = 3
PROMPT_CHUNK = 256
SAMPLE_GROUP = 4
NEG_BIG = -1e30

VMEM_LIMIT = 56 * 1024 * 1024

ROW_TILE = 256
MM_ROW_TILE = 2176
ACC_ROW_TILE = 2176
ACC_SUB_ROWS = 544
ACC_VMEM_LIMIT = 60 * 1024 * 1024
ACC_K_TILE = 1024
ACC_N_TILE = 1024


def _cparams(sem):
    return pltpu.CompilerParams(dimension_semantics=sem, vmem_limit_bytes=VMEM_LIMIT)


def _dot(a, b):
    return lax.dot_general(a, b, (((1,), (0,)), ((), ())), preferred_element_type=F32)


def _dot_nt(a, b):
    return lax.dot_general(a, b, (((1,), (1,)), ((), ())), preferred_element_type=F32)


def _dot_tn(a, b):
    return lax.dot_general(a, b, (((0,), (0,)), ((), ())), preferred_element_type=F32)


def _ada_kernel(c_ref, w_ref, b_ref, o_ref):
    c = c_ref[...]
    sc = (c * jax.nn.sigmoid(c)).astype(BF16)
    o_ref[...] = _dot(sc, w_ref[...]) + b_ref[...]


def _ada_head(c_all, w_ada, b_ada3, n_slabs, tn=1024):
    depth, d, n = w_ada.shape
    rows = c_all.shape[0]
    per = d // tn
    return pl.pallas_call(
        _ada_kernel,
        grid=(n_slabs * per,),
        in_specs=[
            pl.BlockSpec((rows, d), lambda t: (0, 0)),
            pl.BlockSpec((None, d, tn), lambda t: (0, 0, t)),
            pl.BlockSpec((None, 1, tn), lambda t: (0, 0, t)),
        ],
        out_specs=pl.BlockSpec((None, None, rows, tn), lambda t: (0, t // per, 0, t % per)),
        out_shape=jax.ShapeDtypeStruct((depth, n // d, rows, d), F32),
        compiler_params=_cparams(("arbitrary",)),
        name="ada_mods",
    )(c_all, w_ada, b_ada3)


def _rms(x, g):
    return x * lax.rsqrt(jnp.mean(x * x, axis=-1, keepdims=True) + EPS) * g


NORM_CHUNK = 8


def _normmod_kernel(n_prompt_tiles, emit_x, xp_ref, xs_ref, g_ref, shp_ref, scp_ref, shs_ref, scs_ref,
                    h_ref, *rest):
    xo_ref = rest[0] if emit_x else None
    gm_ref = rest[-1]
    i = pl.program_id(0)
    n_bs = shs_ref.shape[0]
    n_chunks = h_ref.shape[0] // NORM_CHUNK

    def run(x_ref, mod_rows):
        def chunk(c, carry):
            r = pl.multiple_of(c * NORM_CHUNK, NORM_CHUNK)
            rows = pl.ds(r, NORM_CHUNK)
            x = x_ref[rows, :]
            rstd = lax.rsqrt(jnp.mean(x * x, axis=-1, keepdims=True) + EPS)
            gm, sh = mod_rows(r)
            h_ref[rows, :] = (x * rstd * gm + sh).astype(h_ref.dtype)
            if emit_x:
                xo_ref[rows, :] = x
            return carry

        lax.fori_loop(0, n_chunks, chunk, 0, unroll=4)

    @pl.when(i < n_prompt_tiles)
    def _():
        gm_ref[0:1, :] = g_ref[...] * (1.0 + scp_ref[...])
        run(xp_ref, lambda r: (gm_ref[0:1, :], shp_ref[...]))

    @pl.when(i >= n_prompt_tiles)
    def _():
        gm_ref[...] = g_ref[...] * (1.0 + scs_ref[...])

        def mod_rows(r):
            rb = pl.ds(pl.multiple_of(lax.rem(r, n_bs), NORM_CHUNK), NORM_CHUNK)
            return gm_ref[rb, :], shs_ref[rb, :]

        run(xs_ref, mod_rows)


def _normmod(xp, xs, xs_row0, g, mods, mods_p, layer, sub, seq, emit_x):
    d = xp.shape[1]
    tr = ROW_TILE if emit_x else 2 * ROW_TILE
    n_bp = mods_p.shape[2]
    n_bs = mods.shape[2] - 8
    rows_p = n_bp * seq
    rows_s = xs.shape[0] - xs_row0
    m = rows_p + rows_s
    n_pt = rows_p // tr
    tiles_per_seq = seq // tr
    s0 = xs_row0 // tr
    i_shift, i_scale = sub * N_MOD, sub * N_MOD + 1
    p_map = lambda which: (lambda i: (layer, which, jnp.minimum(i // tiles_per_seq, n_bp - 1), 0, 0))
    s_map = lambda which: (lambda i: (layer, which, 0, 0))
    out_specs = [pl.BlockSpec((tr, d), lambda i: (i, 0))]
    out_shape = [jax.ShapeDtypeStruct((m, d), BF16)]
    if emit_x:
        out_specs.append(pl.BlockSpec((tr, d), lambda i: (i, 0)))
        out_shape.append(jax.ShapeDtypeStruct((m, d), F32))
    return pl.pallas_call(
        functools.partial(_normmod_kernel, n_pt, emit_x),
        grid=(m // tr,),
        in_specs=[
            pl.BlockSpec((tr, d), lambda i: (jnp.minimum(i, n_pt - 1), 0)),
            pl.BlockSpec((tr, d), lambda i: (s0 + jnp.maximum(i - n_pt, 0), 0)),
            pl.BlockSpec((1, d), lambda i: (0, 0)),
            pl.BlockSpec((None, None, None, 1, d), p_map(i_shift)),
            pl.BlockSpec((None, None, None, 1, d), p_map(i_scale)),
            pl.BlockSpec((None, None, n_bs, d), s_map(i_shift)),
            pl.BlockSpec((None, None, n_bs, d), s_map(i_scale)),
        ],
        out_specs=out_specs,
        out_shape=out_shape,
        scratch_shapes=[pltpu.VMEM((n_bs, d), F32)],
        compiler_params=_cparams(("arbitrary",)),
        name="normmod",
    )(xp, xs, g.reshape(1, d), mods_p, mods_p, mods, mods)


def _final_norm_kernel(x_ref, g_ref, o_ref):
    def chunk(c, carry):
        rows = pl.ds(pl.multiple_of(c * NORM_CHUNK, NORM_CHUNK), NORM_CHUNK)
        o_ref[rows, :] = _rms(x_ref[rows, :], g_ref[...])
        return carry

    lax.fori_loop(0, o_ref.shape[0] // NORM_CHUNK, chunk, 0, unroll=4)


def _final_norm(x, g, row0, rows):
    d = x.shape[1]
    tr = ROW_TILE
    b0 = row0 // tr
    return pl.pallas_call(
        _final_norm_kernel,
        grid=(rows // tr,),
        in_specs=[pl.BlockSpec((tr, d), lambda i: (i + b0, 0)),
                  pl.BlockSpec((1, d), lambda i: (0, 0))],
        out_specs=pl.BlockSpec((tr, d), lambda i: (i, 0)),
        out_shape=jax.ShapeDtypeStruct((rows, d), F32),
        compiler_params=_cparams(("arbitrary",)),
        name="final_norm",
    )(x, g.reshape(1, d))


ADA_TILE = 256


def _swiglu_up_kernel(n_full, tail, ada_tiles, x_ref, w1_ref, w3_ref, *rest):
    i = pl.program_id(0)
    j = pl.program_id(1)
    if ada_tiles:
        c_ref, wa_ref, ba_ref, _, o_ref, mods_ref, sc_ref = rest

        @pl.when(jnp.logical_and(i == 0, j == 0))
        def _():
            c = c_ref[...]
            sc_ref[...] = (c * jax.nn.sigmoid(c)).astype(BF16)
    else:
        o_ref, = rest

    def body(cols, with_ada):
        x = x_ref[...]
        a = _dot(x, w1_ref[:, 0:cols])
        b = _dot(x, w3_ref[:, 0:cols])
        if with_ada:
            mods_ref[...] = _dot(sc_ref[...], wa_ref[...]) + ba_ref[...]
        o_ref[:, 0:cols] = (a * jax.nn.sigmoid(a) * b).astype(o_ref.dtype)

    def step(with_ada):
        if tail == 0:
            body(o_ref.shape[1], with_ada)
        else:
            @pl.when(j < n_full)
            def _():
                body(o_ref.shape[1], with_ada)

            @pl.when(j == n_full)
            def _():
                body(tail, with_ada)

    if ada_tiles:
        n_steps = pl.num_programs(0) * pl.num_programs(1)
        s = i * pl.num_programs(1) + j
        fresh = jnp.logical_or(
            s == 0, (s * ada_tiles) // n_steps != ((jnp.maximum(s, 1) - 1) * ada_tiles) // n_steps)

        @pl.when(fresh)
        def _():
            step(True)

        @pl.when(jnp.logical_not(fresh))
        def _():
            step(False)
    else:
        step(False)


def _swiglu_up(h, w1, w3, layer, sub, ada=None, tn=256):
    m, d = h.shape
    f = w1.shape[-1]
    tm = MM_ROW_TILE
    n_full, tail = divmod(f, tn)
    nj = pl.cdiv(f, tn)
    grid = (m // tm, nj)
    w_spec = pl.BlockSpec((None, None, d, tn), lambda i, j: (layer, sub, 0, j))
    in_specs = [pl.BlockSpec((tm, d), lambda i, j: (i, 0), pipeline_mode=pl.Buffered(1)), w_spec, w_spec]
    args = [h, w1, w3]
    out_specs = [pl.BlockSpec((tm, tn), lambda i, j: (i, j))]
    out_shape = [jax.ShapeDtypeStruct((m, f), BF16)]
    scratch, aliases = [], {}
    if ada is not None:
        c_all, w_ada, b_ada3, mods, first_tile, n_tiles = ada
        n_steps = grid[0] * nj
        per_layer = w_ada.shape[-1] // ADA_TILE
        per_slab = d // ADA_TILE
        assert n_tiles <= n_steps

        def tile(i, j):
            return first_tile + ((i * nj + j) * n_tiles) // n_steps

        in_specs += [
            pl.BlockSpec(c_all.shape, lambda i, j: (0, 0), pipeline_mode=pl.Buffered(1)),
            pl.BlockSpec((None, d, ADA_TILE), lambda i, j: (tile(i, j) // per_layer, 0, tile(i, j) % per_layer)),
            pl.BlockSpec((None, 1, ADA_TILE), lambda i, j: (tile(i, j) // per_layer, 0, tile(i, j) % per_layer)),
            pl.BlockSpec(memory_space=pl.ANY),
        ]
        args += [c_all, w_ada, b_ada3, mods]
        out_specs.append(pl.BlockSpec(
            (None, None, c_all.shape[0], ADA_TILE),
            lambda i, j: (tile(i, j) // per_layer, (tile(i, j) % per_layer) // per_slab, 0,
                          tile(i, j) % per_slab)))
        out_shape.append(jax.ShapeDtypeStruct(mods.shape, mods.dtype))
        scratch = [pltpu.VMEM(c_all.shape, BF16)]
        aliases = {len(args) - 1: 1}
    res = pl.pallas_call(
        functools.partial(_swiglu_up_kernel, n_full, tail, 0 if ada is None else ada[5]),
        grid=grid,
        in_specs=in_specs,
        out_specs=out_specs,
        out_shape=out_shape,
        scratch_shapes=scratch,
        input_output_aliases=aliases,
        compiler_params=_cparams(("arbitrary", "arbitrary")),
        name="swiglu_up",
    )(*args)
    return res if ada is not None else res[0]


def _proj_kernel(x_ref, w_ref, b_ref, o_ref):
    o_ref[...] = (_dot_nt(x_ref[...], w_ref[...]) + b_ref[...]).astype(o_ref.dtype)


def _in_proj(h, w_in_t, b_in3, layer, n_main, tn=512):
    m, d = h.shape
    tm = MM_ROW_TILE
    return pl.pallas_call(
        _proj_kernel,
        grid=(m // tm, n_main // tn),
        in_specs=[pl.BlockSpec((tm, d), lambda i, j: (i, 0), pipeline_mode=pl.Buffered(1)),
                  pl.BlockSpec((None, tn, d), lambda i, j: (layer, j, 0)),
                  pl.BlockSpec((None, 1, tn), lambda i, j: (layer, 0, j))],
        out_specs=pl.BlockSpec((tm, tn), lambda i, j: (i, j)),
        out_shape=jax.ShapeDtypeStruct((m, n_main), BF16),
        compiler_params=_cparams(("arbitrary", "arbitrary")),
        name="in_proj",
    )(h, w_in_t, b_in3)


def _gate_proj(h, w_in_t, b_in3, layer, n_main, tn=128):
    m, d = h.shape
    tm = MM_ROW_TILE
    jb = n_main // tn
    return pl.pallas_call(
        _proj_kernel,
        grid=(m // tm,),
        in_specs=[pl.BlockSpec((tm, d), lambda i: (i, 0)),
                  pl.BlockSpec((None, tn, d), lambda i: (layer, jb, 0)),
                  pl.BlockSpec((None, 1, tn), lambda i: (layer, 0, jb))],
        out_specs=pl.BlockSpec((tm, tn), lambda i: (i, 0)),
        out_shape=jax.ShapeDtypeStruct((m, tn), F32),
        compiler_params=_cparams(("arbitrary",)),
        name="gate_proj",
    )(h, w_in_t, b_in3)


def _gate_segments(tm, n_tiles, n_bp, seq, n_bs):
    rows_p = n_bp * seq
    tiles = []
    for t in range(n_tiles):
        segs, r = [], t * tm
        while r < (t + 1) * tm:
            if r < rows_p:
                b = r // seq
                end = min((b + 1) * seq, (t + 1) * tm)
                segs.append((r - t * tm, end - t * tm, b))
            else:
                end = r + n_bs
                assert (r - rows_p) % n_bs == 0 and end <= (t + 1) * tm
                segs.append((r - t * tm, end - t * tm, None))
            r = end
        tiles.append(segs)
    return tiles


def _acc_resid_kernel(segments, k_last_valid, res_scale, a_ref, w_ref, x_ref, gp_ref, gs_ref, o_ref):
    i = pl.program_id(0)
    k = pl.program_id(2)
    nk = pl.num_programs(2)
    tm, tk = a_ref.shape

    def sweep(kv, first):
        for r0 in range(0, tm, ACC_SUB_ROWS):
            rs = slice(r0, r0 + ACC_SUB_ROWS)
            part = _dot(a_ref[rs, 0:kv], w_ref[0:kv, :])
            if first:
                o_ref[rs, :] = part
            else:
                o_ref[rs, :] += part

    @pl.when(k == 0)
    def _():
        sweep(tk, True)

    @pl.when(jnp.logical_and(k > 0, k < nk - 1))
    def _():
        sweep(tk, False)

    for t, segs in enumerate(segments):
        @pl.when(jnp.logical_and(k == nk - 1, i == t))
        def _():
            for r0 in range(0, tm, ACC_SUB_ROWS):
                r1 = r0 + ACC_SUB_ROWS
                part = _dot(a_ref[r0:r1, 0:k_last_valid], w_ref[0:k_last_valid, :])
                for s0, s1, b in segs:
                    q0, q1 = max(s0, r0), min(s1, r1)
                    if q0 >= q1:
                        continue
                    gate = gs_ref[q0 - s0:q1 - s0, :] if b is None else gp_ref[b]
                    o_ref[q0:q1, :] = (x_ref[q0:q1, :]
                                       + (res_scale * gate) * (o_ref[q0:q1, :] + part[q0 - r0:q1 - r0, :]))


def _acc_resid(a, w_full, w_index, x, mods, mods_p, layer, which, sample_rows, res_scale):
    m, kdim = a.shape
    d = x.shape[1]
    tm, tk, tn = ACC_ROW_TILE, ACC_K_TILE, ACC_N_TILE
    n_bp = mods_p.shape[2]
    n_bs = mods.shape[2] - 8
    seq = (m - sample_rows) // n_bp
    assert m % tm == 0 and tm % ACC_SUB_ROWS == 0
    nk = pl.cdiv(kdim, tk)
    assert nk >= 3
    k_last_valid = kdim - (nk - 1) * tk
    lead = (None,) * len(w_index)
    segments = _gate_segments(tm, m // tm, n_bp, seq, n_bs)
    return pl.pallas_call(
        functools.partial(_acc_resid_kernel, segments, k_last_valid, res_scale),
        grid=(m // tm, d // tn, nk),
        in_specs=[
            pl.BlockSpec((tm, tk), lambda i, j, k: (i, k)),
            pl.BlockSpec(lead + (tk, tn), lambda i, j, k: tuple(w_index) + (k, j)),
            pl.BlockSpec((tm, tn), lambda i, j, k: (i, j)),
            pl.BlockSpec((None, None, n_bp, 1, tn), lambda i, j, k: (layer, which, 0, 0, j)),
            pl.BlockSpec((None, None, n_bs, tn), lambda i, j, k: (layer, which, 0, j)),
        ],
        out_specs=pl.BlockSpec((tm, tn), lambda i, j, k: (i, j)),
        out_shape=jax.ShapeDtypeStruct((m, d), F32),
        compiler_params=pltpu.CompilerParams(
            dimension_semantics=("arbitrary", "arbitrary", "arbitrary"),
            vmem_limit_bytes=ACC_VMEM_LIMIT),
        name="acc_resid",
    )(a, w_full, x, mods_p, mods)


N_HIST = max(POOL_WINDOWS) - 1
HALO = 32
LEVEL_ROW0 = 8


def _pool_prompt_kernel(start, u_ref, wp_ref, sp_ref, y_ref, hist_ref, z_ref, sa_ref, sb_ref):
    t = pl.program_id(1)
    nt = pl.num_programs(1)
    tt = u_ref.shape[0]
    group = wp_ref.shape[1]
    n = HALO + tt
    n_hist = hist_ref.shape[0]

    @pl.when(t == 0)
    def _():
        z_ref[0:HALO, :] = jnp.zeros((HALO, z_ref.shape[1]), F32)
        sa_ref[0:LEVEL_ROW0, :] = jnp.zeros((LEVEL_ROW0, group), F32)
        sb_ref[0:LEVEL_ROW0, :] = jnp.zeros((LEVEL_ROW0, group), F32)

    z_ref[HALO:n, :] = u_ref[...].astype(F32)

    def window_sum(cols, w):
        src, src_cols, sh, level = z_ref, cols, 1, 0
        while 2 * sh < w:
            dst = (sa_ref, sb_ref)[level % 2]
            dst[LEVEL_ROW0:n, :] = src[LEVEL_ROW0:n, src_cols] + src[LEVEL_ROW0 - sh:n - sh, src_cols]
            src, src_cols, sh, level = dst, slice(None), 2 * sh, level + 1
        return src[HALO:n, src_cols] + src[HALO - sh:n - sh, src_cols]

    pos = start + t * tt + lax.broadcasted_iota(jnp.int32, (tt, 1), 0)
    for g, w in enumerate(POOL_WINDOWS):
        assert w & (w - 1) == 0 and LEVEL_ROW0 + 2 * (w // 2 - 1) <= HALO
        cols = slice(g * group, (g + 1) * group)
        cur = z_ref[HALO:n, cols]
        cnt = jnp.minimum(w, pos + 1).astype(F32)
        pooled = window_sum(cols, w) / cnt - cur
        y = _dot(pooled.astype(BF16), wp_ref[g]) * sp_ref[:, cols]
        y_ref[:, cols] = y.astype(y_ref.dtype)

    @pl.when(t == nt - 1)
    def _():
        hist_ref[...] = z_ref[n - n_hist:n, :]

    z_ref[0:HALO, :] = z_ref[tt:n, :]


def _pool_prompt(proj, w_pool_l, s_pool_l, n_b, seq, d, tt=256):
    pw = s_pool_l.shape[-1]
    ntt = seq // tt
    n_groups, group, _ = w_pool_l.shape
    return pl.pallas_call(
        functools.partial(_pool_prompt_kernel, 0),
        grid=(n_b, ntt),
        in_specs=[
            pl.BlockSpec((tt, pw), lambda b, t: (b * ntt + t, 0)),
            pl.BlockSpec((n_groups, group, group), lambda b, t: (0, 0, 0)),
            pl.BlockSpec((1, pw), lambda b, t: (0, 0)),
        ],
        out_specs=[
            pl.BlockSpec((tt, pw), lambda b, t: (b * ntt + t, 0)),
            pl.BlockSpec((None, N_HIST, pw), lambda b, t: (b, 0, 0)),
        ],
        out_shape=[jax.ShapeDtypeStruct((proj.shape[0], d), BF16),
                   jax.ShapeDtypeStruct((n_b, N_HIST, pw), F32)],
        scratch_shapes=[pltpu.VMEM((HALO + tt, pw), F32),
                        pltpu.VMEM((HALO + tt, group), F32),
                        pltpu.VMEM((HALO + tt, group), F32)],
        compiler_params=_cparams(("arbitrary", "arbitrary")),
        name="pool_prompt",
    )(proj, w_pool_l, s_pool_l.reshape(1, pw))


def _pool_sample_kernel(hist_ref, u_ref, wp_ref, sp_ref, *rest):
    y_ref, nh_ref = rest[-2:]
    n_hist, n_b, _ = hist_ref.shape
    n_t = u_ref.shape[0] // n_b
    g = pl.program_id(0)

    def z(r):
        if r < n_hist:
            return hist_ref[r]
        return u_ref[(r - n_hist) * n_b:(r - n_hist + 1) * n_b, :].astype(F32)

    for r in range(n_hist):
        nh_ref[r] = z(r + n_t)

    for t in range(n_t):
        cur = z(n_hist + t)
        run = cur
        sums = []
        for j in range(1, POOL_WINDOWS[-1]):
            run = run + z(n_hist + t - j)
            if j + 1 in POOL_WINDOWS:
                sums.append(run)
        pooled = jnp.zeros_like(cur)
        for gi, w in enumerate(POOL_WINDOWS):
            pooled = jnp.where(g == gi, sums[gi] / float(w) - cur, pooled)
        y = _dot(pooled.astype(BF16), wp_ref[...]) * sp_ref[...]
        y_ref[t * n_b:(t + 1) * n_b, :] = y.astype(y_ref.dtype)


def _pool_sample(state_tm, layer, hist_prev, proj, w_pool_l, s_pool_l, ymix, row0):
    depth, n_hist, n_b, pw = state_tm.shape
    n_groups, group, _ = w_pool_l.shape
    rows = proj.shape[0] - row0
    rb = row0 // rows
    args = [state_tm, proj, w_pool_l, s_pool_l.reshape(1, pw), ymix]
    in_specs = [
        pl.BlockSpec((None, n_hist, n_b, group), lambda g: (layer, 0, 0, g)),
        pl.BlockSpec((rows, group), lambda g: (rb, g)),
        pl.BlockSpec((None, group, group), lambda g: (g, 0, 0)),
        pl.BlockSpec((1, group), lambda g: (0, g)),
        pl.BlockSpec(memory_space=pl.ANY),
    ]
    aliases = {4: 0}
    if hist_prev is not None:
        args.append(hist_prev)
        in_specs.append(pl.BlockSpec(memory_space=pl.ANY))
        aliases[5] = 1
    return pl.pallas_call(
        _pool_sample_kernel,
        grid=(n_groups,),
        in_specs=in_specs,
        out_specs=[pl.BlockSpec((rows, group), lambda g: (rb, g)),
                   pl.BlockSpec((None, n_hist, n_b, group), lambda g: (layer, 0, 0, g))],
        out_shape=[jax.ShapeDtypeStruct(ymix.shape, ymix.dtype),
                   jax.ShapeDtypeStruct(state_tm.shape, F32)],
        input_output_aliases=aliases,
        compiler_params=_cparams(("arbitrary",)),
        name="pool_sample",
    )(*args)


def _mlstm_segment(q, k, v, ig, lf, c_state, n_state, m_state):
    r = q.shape[0]
    row = lax.broadcasted_iota(jnp.int32, (r, r), 0)
    col = lax.broadcasted_iota(jnp.int32, (r, r), 1)
    causal = col <= row
    lf_rows = jnp.sum(jnp.where(row == col, lf, 0.0), axis=0, keepdims=True)
    ig_rows = jnp.sum(jnp.where(row == col, ig, 0.0), axis=0, keepdims=True)
    b_col = jnp.sum(jnp.where(causal, lf_rows, 0.0), axis=1, keepdims=True)
    b_rows = jnp.sum(jnp.where(row <= col, lf, 0.0), axis=0, keepdims=True)
    dmat = jnp.where(causal, b_col - b_rows + ig_rows, -jnp.inf)
    inter = b_col + m_state
    m_tok = jnp.maximum(inter, jnp.max(dmat, axis=-1, keepdims=True))
    w_intra = jnp.exp(dmat - m_tok)
    w_inter = jnp.exp(inter - m_tok)
    s = _dot_nt(q, k) * w_intra
    num = _dot(s.astype(BF16), v) + w_inter * _dot_nt(q, c_state)
    qn = jnp.sum(q.astype(F32) * n_state, axis=-1, keepdims=True)
    den = jnp.sum(s, axis=-1, keepdims=True) + w_inter * qn
    h = num * (1.0 / jnp.maximum(jnp.abs(den), jnp.exp(-m_tok)))
    b_last = jnp.sum(lf, axis=0, keepdims=True)
    dec = b_last - b_col + ig
    m_new = jnp.maximum(b_last + m_state, jnp.max(dec, axis=0, keepdims=True))
    ws = jnp.exp(dec - m_new)
    wc = jnp.exp(b_last + m_state - m_new)
    wk = ws * k.astype(F32)
    c_new = wc * c_state + _dot_tn(v, wk.astype(BF16))
    n_new = wc * n_state + jnp.sum(wk, axis=0, keepdims=True)
    return h, c_new, n_new, m_new


def _head_out(h, o, g_head):
    hn = h * lax.rsqrt(jnp.mean(h * h, axis=-1, keepdims=True) + EPS) * g_head
    return hn * jax.nn.sigmoid(o.astype(F32))


def _log_sigmoid(x):
    return jnp.minimum(x, 0.0) - jnp.log1p(jnp.exp(-jnp.abs(x)))


def _mlstm_prompt_kernel(q_ref, k_ref, v_ref, o_ref, gt_ref, gh_ref, ymix_in_ref,
                         y_ref, c_out_ref, n_out_ref, m_out_ref, c_s, n_s, m_s):
    del ymix_in_ref
    c = pl.program_id(1)
    nc = pl.num_programs(1)
    dk = c_s.shape[2]
    dv = c_s.shape[1]

    @pl.when(c == 0)
    def _():
        c_s[...] = jnp.zeros(c_s.shape, F32)
        n_s[...] = jnp.zeros(n_s.shape, F32)
        m_s[...] = jnp.zeros(m_s.shape, F32)

    gates = gt_ref[...]
    log_f = _log_sigmoid(gates)
    for h in range(N_HEADS):
        q = q_ref[:, h * dk:(h + 1) * dk]
        k = (k_ref[:, h * dk:(h + 1) * dk].astype(F32) * (dk ** -0.5)).astype(BF16)
        v = v_ref[:, h * dv:(h + 1) * dv]
        ig = gates[:, h:h + 1]
        lf = log_f[:, N_HEADS + h:N_HEADS + h + 1]
        hh, c_new, n_new, m_new = _mlstm_segment(
            q, k, v, ig, lf, c_s[h], n_s[h:h + 1, :], m_s[h:h + 1, 0:1])
        c_s[h] = c_new
        n_s[h:h + 1, :] = n_new
        m_s[h:h + 1, :] = jnp.broadcast_to(m_new, (1, m_s.shape[1]))
        y_ref[:, h * dv:(h + 1) * dv] = _head_out(
            hh, o_ref[:, h * dv:(h + 1) * dv], gh_ref[h:h + 1, :]).astype(y_ref.dtype)

    @pl.when(c == nc - 1)
    def _():
        c_out_ref[...] = c_s[...]
        n_out_ref[...] = n_s[0:N_HEADS, :]
        m_out_ref[...] = m_s[...]


def _mlstm_prompt(proj, gates, g_head_l, ymix, n_b, seq, col_q, dk, dv):
    lc = PROMPT_CHUNK
    nch = seq // lc
    qk_w = N_HEADS * dk
    v_w = N_HEADS * dv
    row = lambda b, c: b * nch + c
    assert col_q % qk_w == 0 and (col_q + 2 * qk_w) % v_w == 0
    bq = col_q // qk_w
    bv = (col_q + 2 * qk_w) // v_w
    y, c_out, n_out, m_out = pl.pallas_call(
        _mlstm_prompt_kernel,
        grid=(n_b, nch),
        in_specs=[
            pl.BlockSpec((lc, qk_w), lambda b, c: (row(b, c), bq)),
            pl.BlockSpec((lc, qk_w), lambda b, c: (row(b, c), bq + 1)),
            pl.BlockSpec((lc, v_w), lambda b, c: (row(b, c), bv)),
            pl.BlockSpec((lc, v_w), lambda b, c: (row(b, c), bv + 1)),
            pl.BlockSpec((lc, gates.shape[1]), lambda b, c: (row(b, c), 0)),
            pl.BlockSpec((N_HEADS, dv), lambda b, c: (0, 0)),
            pl.BlockSpec(memory_space=pl.ANY),
        ],
        out_specs=[
            pl.BlockSpec((lc, v_w), lambda b, c: (row(b, c), 1)),
            pl.BlockSpec((None, N_HEADS, dv, dk), lambda b, c: (b, 0, 0, 0)),
            pl.BlockSpec((None, N_HEADS, dk), lambda b, c: (b, 0, 0)),
            pl.BlockSpec((None, 8, 128), lambda b, c: (b, 0, 0)),
        ],
        out_shape=[jax.ShapeDtypeStruct(ymix.shape, ymix.dtype),
                   jax.ShapeDtypeStruct((n_b, N_HEADS, dv, dk), F32),
                   jax.ShapeDtypeStruct((n_b, N_HEADS, dk), F32),
                   jax.ShapeDtypeStruct((n_b, 8, 128), F32)],
        scratch_shapes=[pltpu.VMEM((N_HEADS, dv, dk), F32),
                        pltpu.VMEM((8, dk), F32),
                        pltpu.VMEM((8, 128), F32)],
        input_output_aliases={6: 0},
        compiler_params=_cparams(("arbitrary", "arbitrary")),
        name="mlstm_prompt",
    )(proj, proj, proj, proj, gates, g_head_l, ymix)
    return y, c_out, n_out, m_out[:, :N_HEADS, 0]


def _mlstm_sample_kernel(seq, has_prev, q_ref, k_ref, v_ref, o_ref, gt_ref, gh_ref, c0_ref, n0_ref, m0_ref,
                         *rest):
    y_ref, c_out_ref, n_out_ref, m_out_ref = rest[-4:]
    rows = q_ref.shape[0]
    dk = c0_ref.shape[3]
    dv = c0_ref.shape[2]
    seg_of_row = lax.broadcasted_iota(jnp.int32, (rows, 1), 0) // seq
    gates = gt_ref[...]
    log_f = _log_sigmoid(gates)
    for h in range(N_HEADS):
        q = q_ref[:, h * dk:(h + 1) * dk]
        k = (k_ref[:, h * dk:(h + 1) * dk].astype(F32) * (dk ** -0.5)).astype(BF16)
        v = v_ref[:, h * dv:(h + 1) * dv]
        ig_all = gates[:, h:h + 1]
        lf_all = log_f[:, N_HEADS + h:N_HEADS + h + 1]
        hh = jnp.zeros((rows, dv), F32)
        for sgm in range(rows // seq):
            mine = seg_of_row == sgm
            ig = jnp.where(mine, ig_all, NEG_BIG)
            lf = jnp.where(mine, lf_all, 0.0)
            h_seg, c_new, n_new, m_new = _mlstm_segment(
                q, k, v, ig, lf, c0_ref[sgm, h], n0_ref[sgm, h:h + 1, :], m0_ref[sgm, h:h + 1, :])
            hh = jnp.where(mine, h_seg, hh)
            c_out_ref[sgm, h] = c_new
            n_out_ref[sgm, h:h + 1, :] = n_new
            m_out_ref[sgm, h:h + 1, :] = m_new
        y_ref[:, h * dv:(h + 1) * dv] = _head_out(
            hh, o_ref[:, h * dv:(h + 1) * dv], gh_ref[h:h + 1, :]).astype(y_ref.dtype)


def _mlstm_sample(proj_bm, gates_bm, g_head_l, state_c, state_n, state_m, layer, c_prev, seq, col_q, dk, dv):
    depth, n_b = state_c.shape[:2]
    grp = SAMPLE_GROUP
    rows = grp * seq
    qk_w = N_HEADS * dk
    v_w = N_HEADS * dv
    bq = col_q // qk_w
    bv = (col_q + 2 * qk_w) // v_w
    has_prev = c_prev is not None
    in_specs = [
        pl.BlockSpec((rows, qk_w), lambda i: (i, bq)),
        pl.BlockSpec((rows, qk_w), lambda i: (i, bq + 1)),
        pl.BlockSpec((rows, v_w), lambda i: (i, bv)),
        pl.BlockSpec((rows, v_w), lambda i: (i, bv + 1)),
        pl.BlockSpec((rows, gates_bm.shape[1]), lambda i: (i, 0)),
        pl.BlockSpec((N_HEADS, dv), lambda i: (0, 0)),
        pl.BlockSpec((None, grp, N_HEADS, dv, dk), lambda i: (layer, i, 0, 0, 0)),
        pl.BlockSpec((None, grp, N_HEADS, dk), lambda i: (layer, i, 0, 0)),
        pl.BlockSpec((None, grp, N_HEADS, 1), lambda i: (layer, i, 0, 0)),
    ]
    args = [proj_bm, proj_bm, proj_bm, proj_bm, gates_bm, g_head_l, state_c, state_n,
            state_m.reshape(depth, n_b, N_HEADS, 1)]
    aliases = {}
    if has_prev:
        in_specs.append(pl.BlockSpec(memory_space=pl.ANY))
        args.append(c_prev)
        aliases = {len(args) - 1: 1}
    y, c_out, n_out, m_out = pl.pallas_call(
        functools.partial(_mlstm_sample_kernel, seq, has_prev),
        grid=(n_b // grp,),
        in_specs=in_specs,
        out_specs=[
            pl.BlockSpec((rows, v_w), lambda i: (i, 0)),
            pl.BlockSpec((None, grp, N_HEADS, dv, dk), lambda i: (layer, i, 0, 0, 0)),
            pl.BlockSpec((grp, N_HEADS, dk), lambda i: (i, 0, 0)),
            pl.BlockSpec((grp, N_HEADS, 1), lambda i: (i, 0, 0)),
        ],
        out_shape=[jax.ShapeDtypeStruct((n_b * seq, v_w), BF16),
                   jax.ShapeDtypeStruct(state_c.shape, F32),
                   jax.ShapeDtypeStruct((n_b, N_HEADS, dk), F32),
                   jax.ShapeDtypeStruct((n_b, N_HEADS, 1), F32)],
        input_output_aliases=aliases,
        compiler_params=_cparams(("arbitrary",)),
        name="mlstm_sample",
    )(*args)
    return y, c_out, n_out, m_out[:, :, 0]


def _to_time_major(a, n_b, seq):
    return jnp.transpose(a.reshape(n_b, seq, -1), (1, 0, 2)).reshape(n_b * seq, -1)


def _to_batch_major(a, n_b, seq):
    return jnp.transpose(a.reshape(seq, n_b, -1), (1, 0, 2)).reshape(n_b * seq, -1)


def kernel(x_prompt, x_sample, state_pool, state_C, state_n, state_m, c_prompt, c_sample,
           w_ada, b_ada, g_norm, w_in, b_in, w_pool, s_pool, g_head, w_out, w1, w3, w2, g_final):
    n_bp, seq_p, d = x_prompt.shape
    n_bs, seq_s, _ = x_sample.shape
    depth = w_ada.shape[0]
    pw = s_pool.shape[-1]
    dv = g_head.shape[-1]
    dk = state_C.shape[-1]
    n_hist = state_pool.shape[2]
    rows_p = n_bp * seq_p
    rows_s = n_bs * seq_s
    n_main = pw + 2 * N_HEADS * dk + 2 * N_HEADS * dv

    xp0 = x_prompt.reshape(rows_p, d)
    xs0 = _to_time_major(x_sample, n_bs, seq_s)

    c_all = jnp.concatenate([c_sample, c_prompt, jnp.zeros((8 - n_bp, d), F32)], axis=0)
    b_ada3 = b_ada.reshape(depth, 1, b_ada.shape[-1])
    tiles_per_slab = d // ADA_TILE
    tiles_per_layer = N_SUB * N_MOD * tiles_per_slab
    assert depth == 2
    mods = _ada_head(c_all, w_ada, b_ada3, N_MOD)
    ada_jobs = {(0, 0): (N_MOD * tiles_per_slab, tiles_per_layer - N_MOD * tiles_per_slab),
                (0, 1): (tiles_per_layer, tiles_per_layer)}

    def prompt_rows(mods):
        return mods[:, :, n_bs:n_bs + n_bp][:, :, :, None, :]

    b_in3 = b_in.reshape(depth, 1, b_in.shape[-1])
    w_in_t = jnp.transpose(w_in, (0, 2, 1))
    pool_p, c_p, n_p, m_p = [], [], [], []
    n_s, m_s = [], []
    c_s_all = None
    hist_s_tm = None
    state_pool_tm = jnp.transpose(state_pool, (0, 2, 1, 3))

    def ffn(x, h, mods, l, sub_layer, ffn_idx):
        job = ada_jobs.get((l, ffn_idx))
        if job is None:
            act = _swiglu_up(h, w1, w3, l, ffn_idx)
        else:
            act, mods = _swiglu_up(h, w1, w3, l, ffn_idx, (c_all, w_ada, b_ada3, mods) + job)
        x = _acc_resid(act, w2, (l, ffn_idx), x, mods, prompt_rows(mods), l, sub_layer * N_MOD + 2,
                       rows_s, FFN_RES)
        return x, mods

    x = None
    for l in range(depth):
        mods_p = prompt_rows(mods)
        if l == 0:
            h, x = _normmod(xp0, xs0, 0, g_norm[l, 0], mods, mods_p, l, 0, seq_p, True)
        else:
            h, = _normmod(x, x, rows_p, g_norm[l, 0], mods, mods_p, l, 0, seq_p, False)
        x, mods = ffn(x, h, mods, l, 0, 0)
        mods_p = prompt_rows(mods)

        h, = _normmod(x, x, rows_p, g_norm[l, 1], mods, mods_p, l, 1, seq_p, False)
        proj = _in_proj(h, w_in_t, b_in3, l, n_main)
        gates = _gate_proj(h, w_in_t, b_in3, l, n_main)

        ymix, hist_p = _pool_prompt(proj, w_pool[l], s_pool[l], n_bp, seq_p, d)
        ymix, cp, np_, mp = _mlstm_prompt(proj, gates, g_head[l], ymix, n_bp, seq_p, pw, dk, dv)
        ymix, hist_s_tm = _pool_sample(state_pool_tm, l, hist_s_tm, proj, w_pool[l], s_pool[l], ymix, rows_p)
        proj_bm = _to_batch_major(proj[rows_p:], n_bs, seq_s)
        gates_bm = _to_batch_major(gates[rows_p:], n_bs, seq_s)
        y_ms, c_s_all, ns, ms = _mlstm_sample(proj_bm, gates_bm, g_head[l], state_C, state_n, state_m,
                                              l, c_s_all, seq_s, pw, dk, dv)
        ymix = lax.dynamic_update_slice(ymix, _to_time_major(y_ms, n_bs, seq_s), (rows_p, pw))
        x = _acc_resid(ymix, w_out, (l,), x, mods, mods_p, l, 1 * N_MOD + 2, rows_s, 1.0)

        pool_p.append(hist_p); c_p.append(cp); n_p.append(np_); m_p.append(mp)
        n_s.append(ns); m_s.append(ms)

        h, = _normmod(x, x, rows_p, g_norm[l, 2], mods, mods_p, l, 2, seq_p, False)
        x, mods = ffn(x, h, mods, l, 2, 1)

    y_prompt = _final_norm(x, g_final, 0, rows_p).reshape(n_bp, seq_p, d)
    y_sample = _to_batch_major(_final_norm(x, g_final, rows_p, rows_s), n_bs, seq_s).reshape(n_bs, seq_s, d)
    return (y_prompt, y_sample,
            jnp.stack(pool_p), jnp.stack(c_p), jnp.stack(n_p), jnp.stack(m_p),
            jnp.transpose(hist_s_tm, (0, 2, 1, 3)), c_s_all, jnp.stack(n_s), jnp.stack(m_s))
```

```python
import functools

import jax
import jax.numpy as jnp
from jax import lax
from jax.experimental import pallas as pl
from jax.experimental.pallas import tpu as pltpu

F32 = jnp.float32
BF16 = jnp.bfloat16

EPS = 1e-6
FFN_RES = 0.5
POOL_WINDOWS = (2, 4, 8, 16)
N_HEADS = 4
N_SUB = 3
N_MOD = 3
PROMPT_CHUNK = 512
SAMPLE_GROUP = 4
NEG_BIG = -1e30

VMEM_LIMIT = 56 * 1024 * 1024

ROW_TILE = 256
MM_ROW_TILE = 2176
ACC_ROW_TILE = 2176
ACC_SUB_ROWS = 544
ACC_VMEM_LIMIT = 60 * 1024 * 1024
ACC_K_TILE = 1024
ACC_N_TILE = 1024


def _cparams(sem):
    return pltpu.CompilerParams(dimension_semantics=sem, vmem_limit_bytes=VMEM_LIMIT)


def _dot(a, b):
    return lax.dot_general(a, b, (((1,), (0,)), ((), ())), preferred_element_type=F32)


def _dot_nt(a, b):
    return lax.dot_general(a, b, (((1,), (1,)), ((), ())), preferred_element_type=F32)


def _dot_tn(a, b):
    return lax.dot_general(a, b, (((0,), (0,)), ((), ())), preferred_element_type=F32)


def _ada_kernel(c_ref, w_ref, b_ref, o_ref):
    c = c_ref[...]
    sc = (c * jax.nn.sigmoid(c)).astype(BF16)
    o_ref[...] = _dot(sc, w_ref[...]) + b_ref[...]


def _ada_head(c_all, w_ada, b_ada3, n_slabs, tn=1024):
    depth, d, n = w_ada.shape
    rows = c_all.shape[0]
    per = d // tn
    return pl.pallas_call(
        _ada_kernel,
        grid=(n_slabs * per,),
        in_specs=[
            pl.BlockSpec((rows, d), lambda t: (0, 0)),
            pl.BlockSpec((None, d, tn), lambda t: (0, 0, t)),
            pl.BlockSpec((None, 1, tn), lambda t: (0, 0, t)),
        ],
        out_specs=pl.BlockSpec((None, None, rows, tn), lambda t: (0, t // per, 0, t % per)),
        out_shape=jax.ShapeDtypeStruct((depth, n // d, rows, d), F32),
        compiler_params=_cparams(("arbitrary",)),
        name="ada_mods",
    )(c_all, w_ada, b_ada3)


def _rms(x, g):
    return x * lax.rsqrt(jnp.mean(x * x, axis=-1, keepdims=True) + EPS) * g


NORM_CHUNK = 8


def _normmod_kernel(n_prompt_tiles, emit_x, xp_ref, xs_ref, g_ref, shp_ref, scp_ref, shs_ref, scs_ref,
                    h_ref, *rest):
    xo_ref = rest[0] if emit_x else None
    gm_ref = rest[-1]
    i = pl.program_id(0)
    n_bs = shs_ref.shape[0]
    n_chunks = h_ref.shape[0] // NORM_CHUNK

    def run(x_ref, mod_rows):
        def chunk(c, carry):
            r = pl.multiple_of(c * NORM_CHUNK, NORM_CHUNK)
            rows = pl.ds(r, NORM_CHUNK)
            x = x_ref[rows, :]
            rstd = lax.rsqrt(jnp.mean(x * x, axis=-1, keepdims=True) + EPS)
            gm, sh = mod_rows(r)
            h_ref[rows, :] = (x * rstd * gm + sh).astype(h_ref.dtype)
            if emit_x:
                xo_ref[rows, :] = x
            return carry

        lax.fori_loop(0, n_chunks, chunk, 0, unroll=4)

    @pl.when(i < n_prompt_tiles)
    def _():
        gm_ref[0:1, :] = g_ref[...] * (1.0 + scp_ref[...])
        run(xp_ref, lambda r: (gm_ref[0:1, :], shp_ref[...]))

    @pl.when(i >= n_prompt_tiles)
    def _():
        gm_ref[...] = g_ref[...] * (1.0 + scs_ref[...])

        def mod_rows(r):
            rb = pl.ds(pl.multiple_of(lax.rem(r, n_bs), NORM_CHUNK), NORM_CHUNK)
            return gm_ref[rb, :], shs_ref[rb, :]

        run(xs_ref, mod_rows)


def _normmod(xp, xs, xs_row0, g, mods, mods_p, layer, sub, seq, emit_x):
    d = xp.shape[1]
    tr = ROW_TILE if emit_x else 2 * ROW_TILE
    n_bp = mods_p.shape[2]
    n_bs = mods.shape[2] - 8
    rows_p = n_bp * seq
    rows_s = xs.shape[0] - xs_row0
    m = rows_p + rows_s
    n_pt = rows_p // tr
    tiles_per_seq = seq // tr
    s0 = xs_row0 // tr
    i_shift, i_scale = sub * N_MOD, sub * N_MOD + 1
    p_map = lambda which: (lambda i: (layer, which, jnp.minimum(i // tiles_per_seq, n_bp - 1), 0, 0))
    s_map = lambda which: (lambda i: (layer, which, 0, 0))
    out_specs = [pl.BlockSpec((tr, d), lambda i: (i, 0))]
    out_shape = [jax.ShapeDtypeStruct((m, d), BF16)]
    if emit_x:
        out_specs.append(pl.BlockSpec((tr, d), lambda i: (i, 0)))
        out_shape.append(jax.ShapeDtypeStruct((m, d), F32))
    return pl.pallas_call(
        functools.partial(_normmod_kernel, n_pt, emit_x),
        grid=(m // tr,),
        in_specs=[
            pl.BlockSpec((tr, d), lambda i: (jnp.minimum(i, n_pt - 1), 0)),
            pl.BlockSpec((tr, d), lambda i: (s0 + jnp.maximum(i - n_pt, 0), 0)),
            pl.BlockSpec((1, d), lambda i: (0, 0)),
            pl.BlockSpec((None, None, None, 1, d), p_map(i_shift)),
            pl.BlockSpec((None, None, None, 1, d), p_map(i_scale)),
            pl.BlockSpec((None, None, n_bs, d), s_map(i_shift)),
            pl.BlockSpec((None, None, n_bs, d), s_map(i_scale)),
        ],
        out_specs=out_specs,
        out_shape=out_shape,
        scratch_shapes=[pltpu.VMEM((n_bs, d), F32)],
        compiler_params=_cparams(("arbitrary",)),
        name="normmod",
    )(xp, xs, g.reshape(1, d), mods_p, mods_p, mods, mods)


def _final_norm_kernel(x_ref, g_ref, o_ref):
    def chunk(c, carry):
        rows = pl.ds(pl.multiple_of(c * NORM_CHUNK, NORM_CHUNK), NORM_CHUNK)
        o_ref[rows, :] = _rms(x_ref[rows, :], g_ref[...])
        return carry

    lax.fori_loop(0, o_ref.shape[0] // NORM_CHUNK, chunk, 0, unroll=4)


def _final_norm(x, g, row0, rows):
    d = x.shape[1]
    tr = ROW_TILE
    b0 = row0 // tr
    return pl.pallas_call(
        _final_norm_kernel,
        grid=(rows // tr,),
        in_specs=[pl.BlockSpec((tr, d), lambda i: (i + b0, 0)),
                  pl.BlockSpec((1, d), lambda i: (0, 0))],
        out_specs=pl.BlockSpec((tr, d), lambda i: (i, 0)),
        out_shape=jax.ShapeDtypeStruct((rows, d), F32),
        compiler_params=_cparams(("arbitrary",)),
        name="final_norm",
    )(x, g.reshape(1, d))


ADA_TILE = 512


def _swiglu_up_kernel(n_full, tail, ada_tiles, x_ref, w1_ref, w3_ref, *rest):
    i = pl.program_id(0)
    j = pl.program_id(1)
    if ada_tiles:
        c_ref, wa_ref, ba_ref, _, o_ref, mods_ref, sc_ref = rest

        @pl.when(jnp.logical_and(i == 0, j == 0))
        def _():
            c = c_ref[...]
            sc_ref[...] = (c * jax.nn.sigmoid(c)).astype(BF16)
    else:
        o_ref, = rest

    def body(cols, with_ada):
        x = x_ref[...]
        a = _dot(x, w1_ref[:, 0:cols])
        b = _dot(x, w3_ref[:, 0:cols])
        if with_ada:
            mods_ref[...] = _dot(sc_ref[...], wa_ref[...]) + ba_ref[...]
        o_ref[:, 0:cols] = (a * jax.nn.sigmoid(a) * b).astype(o_ref.dtype)

    def step(with_ada):
        if tail == 0:
            body(o_ref.shape[1], with_ada)
        else:
            @pl.when(j < n_full)
            def _():
                body(o_ref.shape[1], with_ada)

            @pl.when(j == n_full)
            def _():
                body(tail, with_ada)

    if ada_tiles:
        n_steps = pl.num_programs(0) * pl.num_programs(1)
        s = i * pl.num_programs(1) + j
        fresh = jnp.logical_or(
            s == 0, (s * ada_tiles) // n_steps != ((jnp.maximum(s, 1) - 1) * ada_tiles) // n_steps)

        @pl.when(fresh)
        def _():
            step(True)

        @pl.when(jnp.logical_not(fresh))
        def _():
            step(False)
    else:
        step(False)


def _swiglu_up(h, w1, w3, layer, sub, ada=None, tn=256):
    m, d = h.shape
    f = w1.shape[-1]
    tm = MM_ROW_TILE
    n_full, tail = divmod(f, tn)
    nj = pl.cdiv(f, tn)
    grid = (m // tm, nj)
    w_spec = pl.BlockSpec((None, None, d, tn), lambda i, j: (layer, sub, 0, j))
    in_specs = [pl.BlockSpec((tm, d), lambda i, j: (i, 0), pipeline_mode=pl.Buffered(1)), w_spec, w_spec]
    args = [h, w1, w3]
    out_specs = [pl.BlockSpec((tm, tn), lambda i, j: (i, j))]
    out_shape = [jax.ShapeDtypeStruct((m, f), BF16)]
    scratch, aliases = [], {}
    if ada is not None:
        c_all, w_ada, b_ada3, mods, first_tile, n_tiles = ada
        n_steps = grid[0] * nj
        per_layer = w_ada.shape[-1] // ADA_TILE
        per_slab = d // ADA_TILE
        assert n_tiles <= n_steps

        def tile(i, j):
            return first_tile + ((i * nj + j) * n_tiles) // n_steps

        in_specs += [
            pl.BlockSpec(c_all.shape, lambda i, j: (0, 0), pipeline_mode=pl.Buffered(1)),
            pl.BlockSpec((None, d, ADA_TILE), lambda i, j: (tile(i, j) // per_layer, 0, tile(i, j) % per_layer)),
            pl.BlockSpec((None, 1, ADA_TILE), lambda i, j: (tile(i, j) // per_layer, 0, tile(i, j) % per_layer)),
            pl.BlockSpec(memory_space=pl.ANY),
        ]
        args += [c_all, w_ada, b_ada3, mods]
        out_specs.append(pl.BlockSpec(
            (None, None, c_all.shape[0], ADA_TILE),
            lambda i, j: (tile(i, j) // per_layer, (tile(i, j) % per_layer) // per_slab, 0,
                          tile(i, j) % per_slab)))
        out_shape.append(jax.ShapeDtypeStruct(mods.shape, mods.dtype))
        scratch = [pltpu.VMEM(c_all.shape, BF16)]
        aliases = {len(args) - 1: 1}
    res = pl.pallas_call(
        functools.partial(_swiglu_up_kernel, n_full, tail, 0 if ada is None else ada[5]),
        grid=grid,
        in_specs=in_specs,
        out_specs=out_specs,
        out_shape=out_shape,
        scratch_shapes=scratch,
        input_output_aliases=aliases,
        compiler_params=pltpu.CompilerParams(
            dimension_semantics=("arbitrary", "arbitrary"),
            vmem_limit_bytes=VMEM_LIMIT if ada is None else ACC_VMEM_LIMIT),
        name="swiglu_up",
    )(*args)
    return res if ada is not None else res[0]


def _proj_kernel(x_ref, w_ref, b_ref, o_ref):
    o_ref[...] = (_dot_nt(x_ref[...], w_ref[...]) + b_ref[...]).astype(o_ref.dtype)


def _in_proj(h, w_in_t, b_in3, layer, n_main, tn=512):
    m, d = h.shape
    tm = MM_ROW_TILE
    return pl.pallas_call(
        _proj_kernel,
        grid=(m // tm, n_main // tn),
        in_specs=[pl.BlockSpec((tm, d), lambda i, j: (i, 0), pipeline_mode=pl.Buffered(1)),
                  pl.BlockSpec((None, tn, d), lambda i, j: (layer, j, 0)),
                  pl.BlockSpec((None, 1, tn), lambda i, j: (layer, 0, j))],
        out_specs=pl.BlockSpec((tm, tn), lambda i, j: (i, j)),
        out_shape=jax.ShapeDtypeStruct((m, n_main), BF16),
        compiler_params=_cparams(("arbitrary", "arbitrary")),
        name="in_proj",
    )(h, w_in_t, b_in3)


def _gate_proj(h, w_in_t, b_in3, layer, n_main, tn=128):
    m, d = h.shape
    tm = MM_ROW_TILE
    jb = n_main // tn
    return pl.pallas_call(
        _proj_kernel,
        grid=(m // tm,),
        in_specs=[pl.BlockSpec((tm, d), lambda i: (i, 0)),
                  pl.BlockSpec((None, tn, d), lambda i: (layer, jb, 0)),
                  pl.BlockSpec((None, 1, tn), lambda i: (layer, 0, jb))],
        out_specs=pl.BlockSpec((tm, tn), lambda i: (i, 0)),
        out_shape=jax.ShapeDtypeStruct((m, tn), F32),
        compiler_params=_cparams(("arbitrary",)),
        name="gate_proj",
    )(h, w_in_t, b_in3)


def _gate_segments(tm, n_tiles, n_bp, seq, n_bs):
    rows_p = n_bp * seq
    tiles = []
    for t in range(n_tiles):
        segs, r = [], t * tm
        while r < (t + 1) * tm:
            if r < rows_p:
                b = r // seq
                end = min((b + 1) * seq, (t + 1) * tm)
                segs.append((r - t * tm, end - t * tm, b))
            else:
                end = r + n_bs
                assert (r - rows_p) % n_bs == 0 and end <= (t + 1) * tm
                segs.append((r - t * tm, end - t * tm, None))
            r = end
        tiles.append(segs)
    return tiles


def _acc_resid_kernel(segments, k_last_valid, res_scale, a_ref, w_ref, x_ref, gp_ref, gs_ref, o_ref):
    i = pl.program_id(0)
    k = pl.program_id(2)
    nk = pl.num_programs(2)
    tm, tk = a_ref.shape

    def sweep(kv, first):
        for r0 in range(0, tm, ACC_SUB_ROWS):
            rs = slice(r0, r0 + ACC_SUB_ROWS)
            part = _dot(a_ref[rs, 0:kv], w_ref[0:kv, :])
            if first:
                o_ref[rs, :] = part
            else:
                o_ref[rs, :] += part

    @pl.when(k == 0)
    def _():
        sweep(tk, True)

    @pl.when(jnp.logical_and(k > 0, k < nk - 1))
    def _():
        sweep(tk, False)

    for t, segs in enumerate(segments):
        @pl.when(jnp.logical_and(k == nk - 1, i == t))
        def _():
            for r0 in range(0, tm, ACC_SUB_ROWS):
                r1 = r0 + ACC_SUB_ROWS
                part = _dot(a_ref[r0:r1, 0:k_last_valid], w_ref[0:k_last_valid, :])
                for s0, s1, b in segs:
                    q0, q1 = max(s0, r0), min(s1, r1)
                    if q0 >= q1:
                        continue
                    gate = gs_ref[q0 - s0:q1 - s0, :] if b is None else gp_ref[b]
                    o_ref[q0:q1, :] = (x_ref[q0:q1, :]
                                       + (res_scale * gate) * (o_ref[q0:q1, :] + part[q0 - r0:q1 - r0, :]))


def _acc_resid(a, w_full, w_index, x, mods, mods_p, layer, which, sample_rows, res_scale):
    m, kdim = a.shape
    d = x.shape[1]
    tm, tk, tn = ACC_ROW_TILE, ACC_K_TILE, ACC_N_TILE
    n_bp = mods_p.shape[2]
    n_bs = mods.shape[2] - 8
    seq = (m - sample_rows) // n_bp
    assert m % tm == 0 and tm % ACC_SUB_ROWS == 0
    nk = pl.cdiv(kdim, tk)
    assert nk >= 2
    k_last_valid = kdim - (nk - 1) * tk
    lead = (None,) * len(w_index)
    segments = _gate_segments(tm, m // tm, n_bp, seq, n_bs)
    return pl.pallas_call(
        functools.partial(_acc_resid_kernel, segments, k_last_valid, res_scale),
        grid=(m // tm, d // tn, nk),
        in_specs=[
            pl.BlockSpec((tm, tk), lambda i, j, k: (i, k)),
            pl.BlockSpec(lead + (tk, tn), lambda i, j, k: tuple(w_index) + (k, j)),
            pl.BlockSpec((tm, tn), lambda i, j, k: (i, j)),
            pl.BlockSpec((None, None, n_bp, 1, tn), lambda i, j, k: (layer, which, 0, 0, j)),
            pl.BlockSpec((None, None, n_bs, tn), lambda i, j, k: (layer, which, 0, j)),
        ],
        out_specs=pl.BlockSpec((tm, tn), lambda i, j, k: (i, j)),
        out_shape=jax.ShapeDtypeStruct((m, d), F32),
        compiler_params=pltpu.CompilerParams(
            dimension_semantics=("arbitrary", "arbitrary", "arbitrary"),
            vmem_limit_bytes=ACC_VMEM_LIMIT),
        name="acc_resid",
    )(a, w_full, x, mods_p, mods)


N_HIST = max(POOL_WINDOWS) - 1
HALO = 32
LEVEL_ROW0 = 8


def _pool_prompt_kernel(start, u_ref, wp_ref, sp_ref, y_ref, hist_ref, z_ref, sa_ref, sb_ref):
    t = pl.program_id(1)
    nt = pl.num_programs(1)
    tt = u_ref.shape[0]
    group = wp_ref.shape[1]
    n = HALO + tt
    n_hist = hist_ref.shape[0]

    @pl.when(t == 0)
    def _():
        z_ref[0:HALO, :] = jnp.zeros((HALO, z_ref.shape[1]), F32)
        sa_ref[0:LEVEL_ROW0, :] = jnp.zeros((LEVEL_ROW0, group), F32)
        sb_ref[0:LEVEL_ROW0, :] = jnp.zeros((LEVEL_ROW0, group), F32)

    z_ref[HALO:n, :] = u_ref[...].astype(F32)

    def window_sum(cols, w):
        src, src_cols, sh, level = z_ref, cols, 1, 0
        while 2 * sh < w:
            dst = (sa_ref, sb_ref)[level % 2]
            dst[LEVEL_ROW0:n, :] = src[LEVEL_ROW0:n, src_cols] + src[LEVEL_ROW0 - sh:n - sh, src_cols]
            src, src_cols, sh, level = dst, slice(None), 2 * sh, level + 1
        return src[HALO:n, src_cols] + src[HALO - sh:n - sh, src_cols]

    pos = start + t * tt + lax.broadcasted_iota(jnp.int32, (tt, 1), 0)
    for g, w in enumerate(POOL_WINDOWS):
        assert w & (w - 1) == 0 and LEVEL_ROW0 + 2 * (w // 2 - 1) <= HALO
        cols = slice(g * group, (g + 1) * group)
        cur = z_ref[HALO:n, cols]
        cnt = jnp.minimum(w, pos + 1).astype(F32)
        pooled = window_sum(cols, w) / cnt - cur
        y = _dot(pooled.astype(BF16), wp_ref[g]) * sp_ref[:, cols]
        y_ref[:, cols] = y.astype(y_ref.dtype)

    @pl.when(t == nt - 1)
    def _():
        hist_ref[...] = z_ref[n - n_hist:n, :]

    z_ref[0:HALO, :] = z_ref[tt:n, :]


def _pool_prompt(proj, w_pool_l, s_pool_l, n_b, seq, d, tt=256):
    pw = s_pool_l.shape[-1]
    ntt = seq // tt
    n_groups, group, _ = w_pool_l.shape
    return pl.pallas_call(
        functools.partial(_pool_prompt_kernel, 0),
        grid=(n_b, ntt),
        in_specs=[
            pl.BlockSpec((tt, pw), lambda b, t: (b * ntt + t, 0)),
            pl.BlockSpec((n_groups, group, group), lambda b, t: (0, 0, 0)),
            pl.BlockSpec((1, pw), lambda b, t: (0, 0)),
        ],
        out_specs=[
            pl.BlockSpec((tt, pw), lambda b, t: (b * ntt + t, 0)),
            pl.BlockSpec((None, N_HIST, pw), lambda b, t: (b, 0, 0)),
        ],
        out_shape=[jax.ShapeDtypeStruct((proj.shape[0], d), BF16),
                   jax.ShapeDtypeStruct((n_b, N_HIST, pw), F32)],
        scratch_shapes=[pltpu.VMEM((HALO + tt, pw), F32),
                        pltpu.VMEM((HALO + tt, group), F32),
                        pltpu.VMEM((HALO + tt, group), F32)],
        compiler_params=_cparams(("arbitrary", "arbitrary")),
        name="pool_prompt",
    )(proj, w_pool_l, s_pool_l.reshape(1, pw))


def _pool_sample_kernel(hist_ref, u_ref, wp_ref, sp_ref, *rest):
    y_ref, nh_ref = rest[-2:]
    n_hist, n_b, _ = hist_ref.shape
    n_t = u_ref.shape[0] // n_b
    g = pl.program_id(0)

    def z(r):
        if r < n_hist:
            return hist_ref[r]
        return u_ref[(r - n_hist) * n_b:(r - n_hist + 1) * n_b, :].astype(F32)

    for r in range(n_hist):
        nh_ref[r] = z(r + n_t)

    for t in range(n_t):
        cur = z(n_hist + t)
        run = cur
        sums = []
        for j in range(1, POOL_WINDOWS[-1]):
            run = run + z(n_hist + t - j)
            if j + 1 in POOL_WINDOWS:
                sums.append(run)
        pooled = jnp.zeros_like(cur)
        for gi, w in enumerate(POOL_WINDOWS):
            pooled = jnp.where(g == gi, sums[gi] / float(w) - cur, pooled)
        y = _dot(pooled.astype(BF16), wp_ref[...]) * sp_ref[...]
        y_ref[t * n_b:(t + 1) * n_b, :] = y.astype(y_ref.dtype)


def _pool_sample(state_tm, layer, hist_prev, proj, w_pool_l, s_pool_l, ymix, row0):
    depth, n_hist, n_b, pw = state_tm.shape
    n_groups, group, _ = w_pool_l.shape
    rows = proj.shape[0] - row0
    rb = row0 // rows
    args = [state_tm, proj, w_pool_l, s_pool_l.reshape(1, pw), ymix]
    in_specs = [
        pl.BlockSpec((None, n_hist, n_b, group), lambda g: (layer, 0, 0, g)),
        pl.BlockSpec((rows, group), lambda g: (rb, g)),
        pl.BlockSpec((None, group, group), lambda g: (g, 0, 0)),
        pl.BlockSpec((1, group), lambda g: (0, g)),
        pl.BlockSpec(memory_space=pl.ANY),
    ]
    aliases = {4: 0}
    if hist_prev is not None:
        args.append(hist_prev)
        in_specs.append(pl.BlockSpec(memory_space=pl.ANY))
        aliases[5] = 1
    return pl.pallas_call(
        _pool_sample_kernel,
        grid=(n_groups,),
        in_specs=in_specs,
        out_specs=[pl.BlockSpec((rows, group), lambda g: (rb, g)),
                   pl.BlockSpec((None, n_hist, n_b, group), lambda g: (layer, 0, 0, g))],
        out_shape=[jax.ShapeDtypeStruct(ymix.shape, ymix.dtype),
                   jax.ShapeDtypeStruct(state_tm.shape, F32)],
        input_output_aliases=aliases,
        compiler_params=_cparams(("arbitrary",)),
        name="pool_sample",
    )(*args)


def _mlstm_segment(q, k, v, ig, lf, c_state, n_state, m_state):
    r = q.shape[0]
    row = lax.broadcasted_iota(jnp.int32, (r, r), 0)
    col = lax.broadcasted_iota(jnp.int32, (r, r), 1)
    causal = col <= row
    lf_rows = jnp.sum(jnp.where(row == col, lf, 0.0), axis=0, keepdims=True)
    ig_rows = jnp.sum(jnp.where(row == col, ig, 0.0), axis=0, keepdims=True)
    b_col = jnp.sum(jnp.where(causal, lf_rows, 0.0), axis=1, keepdims=True)
    b_rows = jnp.sum(jnp.where(row <= col, lf, 0.0), axis=0, keepdims=True)
    dmat = jnp.where(causal, b_col - b_rows + ig_rows, -jnp.inf)
    inter = b_col + m_state
    m_tok = jnp.maximum(inter, jnp.max(dmat, axis=-1, keepdims=True))
    w_intra = jnp.exp(dmat - m_tok)
    w_inter = jnp.exp(inter - m_tok)
    s = _dot_nt(q, k) * w_intra
    num = _dot(s.astype(BF16), v) + w_inter * _dot_nt(q, c_state)
    qn = jnp.sum(q.astype(F32) * n_state, axis=-1, keepdims=True)
    den = jnp.sum(s, axis=-1, keepdims=True) + w_inter * qn
    h = num * (1.0 / jnp.maximum(jnp.abs(den), jnp.exp(-m_tok)))
    b_last = jnp.sum(lf, axis=0, keepdims=True)
    dec = b_last - b_col + ig
    m_new = jnp.maximum(b_last + m_state, jnp.max(dec, axis=0, keepdims=True))
    ws = jnp.exp(dec - m_new)
    wc = jnp.exp(b_last + m_state - m_new)
    wk = ws * k.astype(F32)
    c_new = wc * c_state + _dot_tn(v, wk.astype(BF16))
    n_new = wc * n_state + jnp.sum(wk, axis=0, keepdims=True)
    return h, c_new, n_new, m_new


def _head_out(h, o, g_head):
    hn = h * lax.rsqrt(jnp.mean(h * h, axis=-1, keepdims=True) + EPS) * g_head
    return hn * jax.nn.sigmoid(o.astype(F32))


def _log_sigmoid(x):
    return jnp.minimum(x, 0.0) - jnp.log1p(jnp.exp(-jnp.abs(x)))


def _mlstm_prompt_kernel(q_ref, k_ref, v_ref, o_ref, gt_ref, gh_ref, ymix_in_ref,
                         y_ref, c_out_ref, n_out_ref, m_out_ref, c_s, n_s, m_s):
    del ymix_in_ref
    c = pl.program_id(1)
    nc = pl.num_programs(1)
    dk = c_s.shape[2]
    dv = c_s.shape[1]

    @pl.when(c == 0)
    def _():
        c_s[...] = jnp.zeros(c_s.shape, F32)
        n_s[...] = jnp.zeros(n_s.shape, F32)
        m_s[...] = jnp.zeros(m_s.shape, F32)

    gates = gt_ref[...]
    log_f = _log_sigmoid(gates)
    for h in range(N_HEADS):
        q = q_ref[:, h * dk:(h + 1) * dk]
        k = (k_ref[:, h * dk:(h + 1) * dk].astype(F32) * (dk ** -0.5)).astype(BF16)
        v = v_ref[:, h * dv:(h + 1) * dv]
        ig = gates[:, h:h + 1]
        lf = log_f[:, N_HEADS + h:N_HEADS + h + 1]
        hh, c_new, n_new, m_new = _mlstm_segment(
            q, k, v, ig, lf, c_s[h], n_s[h:h + 1, :], m_s[h:h + 1, 0:1])
        c_s[h] = c_new
        n_s[h:h + 1, :] = n_new
        m_s[h:h + 1, :] = jnp.broadcast_to(m_new, (1, m_s.shape[1]))
        y_ref[:, h * dv:(h + 1) * dv] = _head_out(
            hh, o_ref[:, h * dv:(h + 1) * dv], gh_ref[h:h + 1, :]).astype(y_ref.dtype)

    @pl.when(c == nc - 1)
    def _():
        c_out_ref[...] = c_s[...]
        n_out_ref[...] = n_s[0:N_HEADS, :]
        m_out_ref[...] = m_s[...]


def _mlstm_prompt(proj, gates, g_head_l, ymix, n_b, seq, col_q, dk, dv):
    lc = PROMPT_CHUNK
    nch = seq // lc
    qk_w = N_HEADS * dk
    v_w = N_HEADS * dv
    row = lambda b, c: b * nch + c
    assert col_q % qk_w == 0 and (col_q + 2 * qk_w) % v_w == 0
    bq = col_q // qk_w
    bv = (col_q + 2 * qk_w) // v_w
    y, c_out, n_out, m_out = pl.pallas_call(
        _mlstm_prompt_kernel,
        grid=(n_b, nch),
        in_specs=[
            pl.BlockSpec((lc, qk_w), lambda b, c: (row(b, c), bq)),
            pl.BlockSpec((lc, qk_w), lambda b, c: (row(b, c), bq + 1)),
            pl.BlockSpec((lc, v_w), lambda b, c: (row(b, c), bv)),
            pl.BlockSpec((lc, v_w), lambda b, c: (row(b, c), bv + 1)),
            pl.BlockSpec((lc, gates.shape[1]), lambda b, c: (row(b, c), 0)),
            pl.BlockSpec((N_HEADS, dv), lambda b, c: (0, 0)),
            pl.BlockSpec(memory_space=pl.ANY),
        ],
        out_specs=[
            pl.BlockSpec((lc, v_w), lambda b, c: (row(b, c), 1)),
            pl.BlockSpec((None, N_HEADS, dv, dk), lambda b, c: (b, 0, 0, 0)),
            pl.BlockSpec((None, N_HEADS, dk), lambda b, c: (b, 0, 0)),
            pl.BlockSpec((None, 8, 128), lambda b, c: (b, 0, 0)),
        ],
        out_shape=[jax.ShapeDtypeStruct(ymix.shape, ymix.dtype),
                   jax.ShapeDtypeStruct((n_b, N_HEADS, dv, dk), F32),
                   jax.ShapeDtypeStruct((n_b, N_HEADS, dk), F32),
                   jax.ShapeDtypeStruct((n_b, 8, 128), F32)],
        scratch_shapes=[pltpu.VMEM((N_HEADS, dv, dk), F32),
                        pltpu.VMEM((8, dk), F32),
                        pltpu.VMEM((8, 128), F32)],
        input_output_aliases={6: 0},
        compiler_params=_cparams(("arbitrary", "arbitrary")),
        name="mlstm_prompt",
    )(proj, proj, proj, proj, gates, g_head_l, ymix)
    return y, c_out, n_out, m_out[:, :N_HEADS, 0]


def _mlstm_sample_kernel(seq, q_ref, k_ref, v_ref, o_ref, gt_ref, gh_ref, c0_ref, n0_ref, m0_ref,
                         *rest):
    y_ref, c_out_ref, n_out_ref, m_out_ref = rest[-4:]
    rows = q_ref.shape[0]
    dk = c0_ref.shape[3]
    dv = c0_ref.shape[2]
    seg_of_row = lax.broadcasted_iota(jnp.int32, (rows, 1), 0) // seq
    gates = gt_ref[...]
    log_f = _log_sigmoid(gates)
    for h in range(N_HEADS):
        q = q_ref[:, h * dk:(h + 1) * dk]
        k = (k_ref[:, h * dk:(h + 1) * dk].astype(F32) * (dk ** -0.5)).astype(BF16)
        v = v_ref[:, h * dv:(h + 1) * dv]
        ig_all = gates[:, h:h + 1]
        lf_all = log_f[:, N_HEADS + h:N_HEADS + h + 1]
        hh = jnp.zeros((rows, dv), F32)
        for sgm in range(rows // seq):
            mine = seg_of_row == sgm
            ig = jnp.where(mine, ig_all, NEG_BIG)
            lf = jnp.where(mine, lf_all, 0.0)
            h_seg, c_new, n_new, m_new = _mlstm_segment(
                q, k, v, ig, lf, c0_ref[sgm, h], n0_ref[sgm, h:h + 1, :], m0_ref[sgm, h:h + 1, :])
            hh = jnp.where(mine, h_seg, hh)
            c_out_ref[sgm, h] = c_new
            n_out_ref[sgm, h:h + 1, :] = n_new
            m_out_ref[sgm, h:h + 1, :] = m_new
        y_ref[:, h * dv:(h + 1) * dv] = _head_out(
            hh, o_ref[:, h * dv:(h + 1) * dv], gh_ref[h:h + 1, :]).astype(y_ref.dtype)


def _mlstm_sample(proj_bm, gates_bm, g_head_l, state_c, state_n, state_m, layer, c_prev, seq, col_q, dk, dv):
    depth, n_b = state_c.shape[:2]
    grp = SAMPLE_GROUP
    rows = grp * seq
    qk_w = N_HEADS * dk
    v_w = N_HEADS * dv
    bq = col_q // qk_w
    bv = (col_q + 2 * qk_w) // v_w
    has_prev = c_prev is not None
    in_specs = [
        pl.BlockSpec((rows, qk_w), lambda i: (i, bq)),
        pl.BlockSpec((rows, qk_w), lambda i: (i, bq + 1)),
        pl.BlockSpec((rows, v_w), lambda i: (i, bv)),
        pl.BlockSpec((rows, v_w), lambda i: (i, bv + 1)),
        pl.BlockSpec((rows, gates_bm.shape[1]), lambda i: (i, 0)),
        pl.BlockSpec((N_HEADS, dv), lambda i: (0, 0)),
        pl.BlockSpec((None, grp, N_HEADS, dv, dk), lambda i: (layer, i, 0, 0, 0)),
        pl.BlockSpec((None, grp, N_HEADS, dk), lambda i: (layer, i, 0, 0)),
        pl.BlockSpec((None, grp, N_HEADS, 1), lambda i: (layer, i, 0, 0)),
    ]
    args = [proj_bm, proj_bm, proj_bm, proj_bm, gates_bm, g_head_l, state_c, state_n,
            state_m.reshape(depth, n_b, N_HEADS, 1)]
    aliases = {}
    if has_prev:
        in_specs.append(pl.BlockSpec(memory_space=pl.ANY))
        args.append(c_prev)
        aliases = {len(args) - 1: 1}
    y, c_out, n_out, m_out = pl.pallas_call(
        functools.partial(_mlstm_sample_kernel, seq),
        grid=(n_b // grp,),
        in_specs=in_specs,
        out_specs=[
            pl.BlockSpec((rows, v_w), lambda i: (i, 0)),
            pl.BlockSpec((None, grp, N_HEADS, dv, dk), lambda i: (layer, i, 0, 0, 0)),
            pl.BlockSpec((grp, N_HEADS, dk), lambda i: (i, 0, 0)),
            pl.BlockSpec((grp, N_HEADS, 1), lambda i: (i, 0, 0)),
        ],
        out_shape=[jax.ShapeDtypeStruct((n_b * seq, v_w), BF16),
                   jax.ShapeDtypeStruct(state_c.shape, F32),
                   jax.ShapeDtypeStruct((n_b, N_HEADS, dk), F32),
                   jax.ShapeDtypeStruct((n_b, N_HEADS, 1), F32)],
        input_output_aliases=aliases,
        compiler_params=_cparams(("arbitrary",)),
        name="mlstm_sample",
    )(*args)
    return y, c_out, n_out, m_out[:, :, 0]


def _to_time_major(a, n_b, seq):
    return jnp.transpose(a.reshape(n_b, seq, -1), (1, 0, 2)).reshape(n_b * seq, -1)


def _to_batch_major(a, n_b, seq):
    return jnp.transpose(a.reshape(seq, n_b, -1), (1, 0, 2)).reshape(n_b * seq, -1)


def kernel(x_prompt, x_sample, state_pool, state_C, state_n, state_m, c_prompt, c_sample,
           w_ada, b_ada, g_norm, w_in, b_in, w_pool, s_pool, g_head, w_out, w1, w3, w2, g_final):
    n_bp, seq_p, d = x_prompt.shape
    n_bs, seq_s, _ = x_sample.shape
    depth = w_ada.shape[0]
    pw = s_pool.shape[-1]
    dv = g_head.shape[-1]
    dk = state_C.shape[-1]
    n_hist = state_pool.shape[2]
    rows_p = n_bp * seq_p
    rows_s = n_bs * seq_s
    n_main = pw + 2 * N_HEADS * dk + 2 * N_HEADS * dv

    xp0 = x_prompt.reshape(rows_p, d)
    xs0 = _to_time_major(x_sample, n_bs, seq_s)

    c_all = jnp.concatenate([c_sample, c_prompt, jnp.zeros((8 - n_bp, d), F32)], axis=0)
    b_ada3 = b_ada.reshape(depth, 1, b_ada.shape[-1])
    tiles_per_slab = d // ADA_TILE
    tiles_per_layer = N_SUB * N_MOD * tiles_per_slab
    assert depth == 2
    mods = _ada_head(c_all, w_ada, b_ada3, N_MOD)
    ada_jobs = {(0, 0): (N_MOD * tiles_per_slab, tiles_per_layer - N_MOD * tiles_per_slab),
                (0, 1): (tiles_per_layer, tiles_per_layer)}

    def prompt_rows(mods):
        return mods[:, :, n_bs:n_bs + n_bp][:, :, :, None, :]

    b_in3 = b_in.reshape(depth, 1, b_in.shape[-1])
    w_in_t = jnp.transpose(w_in, (0, 2, 1))
    pool_p, c_p, n_p, m_p = [], [], [], []
    n_s, m_s = [], []
    c_s_all = None
    hist_s_tm = None
    state_pool_tm = jnp.transpose(state_pool, (0, 2, 1, 3))

    def ffn(x, h, mods, l, sub_layer, ffn_idx):
        job = ada_jobs.get((l, ffn_idx))
        if job is None:
            act = _swiglu_up(h, w1, w3, l, ffn_idx)
        else:
            act, mods = _swiglu_up(h, w1, w3, l, ffn_idx, (c_all, w_ada, b_ada3, mods) + job)
        x = _acc_resid(act, w2, (l, ffn_idx), x, mods, prompt_rows(mods), l, sub_layer * N_MOD + 2,
                       rows_s, FFN_RES)
        return x, mods

    x = None
    for l in range(depth):
        mods_p = prompt_rows(mods)
        if l == 0:
            h, x = _normmod(xp0, xs0, 0, g_norm[l, 0], mods, mods_p, l, 0, seq_p, True)
        else:
            h, = _normmod(x, x, rows_p, g_norm[l, 0], mods, mods_p, l, 0, seq_p, False)
        x, mods = ffn(x, h, mods, l, 0, 0)
        mods_p = prompt_rows(mods)

        h, = _normmod(x, x, rows_p, g_norm[l, 1], mods, mods_p, l, 1, seq_p, False)
        proj = _in_proj(h, w_in_t, b_in3, l, n_main)
        gates = _gate_proj(h, w_in_t, b_in3, l, n_main)

        ymix, hist_p = _pool_prompt(proj, w_pool[l], s_pool[l], n_bp, seq_p, d)
        ymix, cp, np_, mp = _mlstm_prompt(proj, gates, g_head[l], ymix, n_bp, seq_p, pw, dk, dv)
        ymix, hist_s_tm = _pool_sample(state_pool_tm, l, hist_s_tm, proj, w_pool[l], s_pool[l], ymix, rows_p)
        proj_bm = _to_batch_major(proj[rows_p:], n_bs, seq_s)
        gates_bm = _to_batch_major(gates[rows_p:], n_bs, seq_s)
        y_ms, c_s_all, ns, ms = _mlstm_sample(proj_bm, gates_bm, g_head[l], state_C, state_n, state_m,
                                              l, c_s_all, seq_s, pw, dk, dv)
        ymix = lax.dynamic_update_slice(ymix, _to_time_major(y_ms, n_bs, seq_s), (rows_p, pw))
        x = _acc_resid(ymix, w_out, (l,), x, mods, mods_p, l, 1 * N_MOD + 2, rows_s, 1.0)

        pool_p.append(hist_p); c_p.append(cp); n_p.append(np_); m_p.append(mp)
        n_s.append(ns); m_s.append(ms)

        h, = _normmod(x, x, rows_p, g_norm[l, 2], mods, mods_p, l, 2, seq_p, False)
        x, mods = ffn(x, h, mods, l, 2, 1)

    y_prompt = _final_norm(x, g_final, 0, rows_p).reshape(n_bp, seq_p, d)
    y_sample = _to_batch_major(_final_norm(x, g_final, rows_p, rows_s), n_bs, seq_s).reshape(n_bs, seq_s, d)
    return (y_prompt, y_sample,
            jnp.stack(pool_p), jnp.stack(c_p), jnp.stack(n_p), jnp.stack(m_p),
            jnp.transpose(hist_s_tm, (0, 2, 1, 3)), c_s_all, jnp.stack(n_s), jnp.stack(m_s))
```

```python
import functools

import jax
import jax.numpy as jnp
from jax import lax
from jax.experimental import pallas as pl
from jax.experimental.pallas import tpu as pltpu

F32 = jnp.float32
BF16 = jnp.bfloat16

EPS = 1e-6
FFN_RES = 0.5
POOL_WINDOWS = (2, 4, 8, 16)
N_HEADS = 4
N_SUB = 3
N_MOD = 3
PROMPT_CHUNK = 512
SAMPLE_GROUP = 4
NEG_BIG = -1e30

VMEM_LIMIT = 56 * 1024 * 1024

ROW_TILE = 256
MM_ROW_TILE = 2176
ACC_ROW_TILE = 2176
ACC_SUB_ROWS = 544
ACC_VMEM_LIMIT = 60 * 1024 * 1024
ACC_K_TILE = 1024
ACC_N_TILE = 1024


def _cparams(sem):
    return pltpu.CompilerParams(dimension_semantics=sem, vmem_limit_bytes=VMEM_LIMIT)


def _dot(a, b):
    return lax.dot_general(a, b, (((1,), (0,)), ((), ())), preferred_element_type=F32)


def _dot_nt(a, b):
    return lax.dot_general(a, b, (((1,), (1,)), ((), ())), preferred_element_type=F32)


def _dot_tn(a, b):
    return lax.dot_general(a, b, (((0,), (0,)), ((), ())), preferred_element_type=F32)


def _ada_kernel(c_ref, w_ref, b_ref, o_ref):
    c = c_ref[...]
    sc = (c * jax.nn.sigmoid(c)).astype(BF16)
    o_ref[...] = _dot(sc, w_ref[...]) + b_ref[...]


def _ada_head(c_all, w_ada, b_ada3, n_slabs, tn=1024):
    depth, d, n = w_ada.shape
    rows = c_all.shape[0]
    per = d // tn
    return pl.pallas_call(
        _ada_kernel,
        grid=(n_slabs * per,),
        in_specs=[
            pl.BlockSpec((rows, d), lambda t: (0, 0)),
            pl.BlockSpec((None, d, tn), lambda t: (0, 0, t)),
            pl.BlockSpec((None, 1, tn), lambda t: (0, 0, t)),
        ],
        out_specs=pl.BlockSpec((None, None, rows, tn), lambda t: (0, t // per, 0, t % per)),
        out_shape=jax.ShapeDtypeStruct((depth, n // d, rows, d), F32),
        compiler_params=_cparams(("arbitrary",)),
        name="ada_mods",
    )(c_all, w_ada, b_ada3)


def _rms(x, g):
    return x * lax.rsqrt(jnp.mean(x * x, axis=-1, keepdims=True) + EPS) * g


NORM_CHUNK = 8


def _normmod_kernel(n_prompt_tiles, emit_x, xp_ref, xs_ref, g_ref, shp_ref, scp_ref, shs_ref, scs_ref,
                    h_ref, *rest):
    xo_ref = rest[0] if emit_x else None
    gm_ref = rest[-1]
    i = pl.program_id(0)
    n_bs = shs_ref.shape[0]
    n_chunks = h_ref.shape[0] // NORM_CHUNK

    def run(x_ref, mod_rows):
        def chunk(c, carry):
            r = pl.multiple_of(c * NORM_CHUNK, NORM_CHUNK)
            rows = pl.ds(r, NORM_CHUNK)
            x = x_ref[rows, :]
            rstd = lax.rsqrt(jnp.mean(x * x, axis=-1, keepdims=True) + EPS)
            gm, sh = mod_rows(r)
            h_ref[rows, :] = (x * rstd * gm + sh).astype(h_ref.dtype)
            if emit_x:
                xo_ref[rows, :] = x
            return carry

        lax.fori_loop(0, n_chunks, chunk, 0, unroll=4)

    @pl.when(i < n_prompt_tiles)
    def _():
        gm_ref[0:1, :] = g_ref[...] * (1.0 + scp_ref[...])
        run(xp_ref, lambda r: (gm_ref[0:1, :], shp_ref[...]))

    @pl.when(i >= n_prompt_tiles)
    def _():
        gm_ref[...] = g_ref[...] * (1.0 + scs_ref[...])

        def mod_rows(r):
            rb = pl.ds(pl.multiple_of(lax.rem(r, n_bs), NORM_CHUNK), NORM_CHUNK)
            return gm_ref[rb, :], shs_ref[rb, :]

        run(xs_ref, mod_rows)


def _normmod(xp, xs, xs_row0, g, mods, mods_p, layer, sub, seq, emit_x):
    d = xp.shape[1]
    tr = ROW_TILE if emit_x else 2 * ROW_TILE
    n_bp = mods_p.shape[2]
    n_bs = mods.shape[2] - 8
    rows_p = n_bp * seq
    rows_s = xs.shape[0] - xs_row0
    m = rows_p + rows_s
    n_pt = rows_p // tr
    tiles_per_seq = seq // tr
    s0 = xs_row0 // tr
    i_shift, i_scale = sub * N_MOD, sub * N_MOD + 1
    p_map = lambda which: (lambda i: (layer, which, jnp.minimum(i // tiles_per_seq, n_bp - 1), 0, 0))
    s_map = lambda which: (lambda i: (layer, which, 0, 0))
    out_specs = [pl.BlockSpec((tr, d), lambda i: (i, 0))]
    out_shape = [jax.ShapeDtypeStruct((m, d), BF16)]
    if emit_x:
        out_specs.append(pl.BlockSpec((tr, d), lambda i: (i, 0)))
        out_shape.append(jax.ShapeDtypeStruct((m, d), F32))
    return pl.pallas_call(
        functools.partial(_normmod_kernel, n_pt, emit_x),
        grid=(m // tr,),
        in_specs=[
            pl.BlockSpec((tr, d), lambda i: (jnp.minimum(i, n_pt - 1), 0)),
            pl.BlockSpec((tr, d), lambda i: (s0 + jnp.maximum(i - n_pt, 0), 0)),
            pl.BlockSpec((1, d), lambda i: (0, 0)),
            pl.BlockSpec((None, None, None, 1, d), p_map(i_shift)),
            pl.BlockSpec((None, None, None, 1, d), p_map(i_scale)),
            pl.BlockSpec((None, None, n_bs, d), s_map(i_shift)),
            pl.BlockSpec((None, None, n_bs, d), s_map(i_scale)),
        ],
        out_specs=out_specs,
        out_shape=out_shape,
        scratch_shapes=[pltpu.VMEM((n_bs, d), F32)],
        compiler_params=_cparams(("arbitrary",)),
        name="normmod",
    )(xp, xs, g.reshape(1, d), mods_p, mods_p, mods, mods)


def _final_norm_kernel(x_ref, g_ref, o_ref):
    def chunk(c, carry):
        rows = pl.ds(pl.multiple_of(c * NORM_CHUNK, NORM_CHUNK), NORM_CHUNK)
        o_ref[rows, :] = _rms(x_ref[rows, :], g_ref[...])
        return carry

    lax.fori_loop(0, o_ref.shape[0] // NORM_CHUNK, chunk, 0, unroll=4)


def _final_norm(x, g, row0, rows):
    d = x.shape[1]
    tr = ROW_TILE
    b0 = row0 // tr
    return pl.pallas_call(
        _final_norm_kernel,
        grid=(rows // tr,),
        in_specs=[pl.BlockSpec((tr, d), lambda i: (i + b0, 0)),
                  pl.BlockSpec((1, d), lambda i: (0, 0))],
        out_specs=pl.BlockSpec((tr, d), lambda i: (i, 0)),
        out_shape=jax.ShapeDtypeStruct((rows, d), F32),
        compiler_params=_cparams(("arbitrary",)),
        name="final_norm",
    )(x, g.reshape(1, d))


ADA_TILE = 512


def _swiglu_up_kernel(n_full, tail, ada_tiles, x_ref, w1_ref, w3_ref, *rest):
    i = pl.program_id(0)
    j = pl.program_id(1)
    if ada_tiles:
        c_ref, wa_ref, ba_ref, _, o_ref, mods_ref, sc_ref = rest

        @pl.when(jnp.logical_and(i == 0, j == 0))
        def _():
            c = c_ref[...]
            sc_ref[...] = (c * jax.nn.sigmoid(c)).astype(BF16)
    else:
        o_ref, = rest

    def body(cols, with_ada):
        x = x_ref[...]
        a = _dot(x, w1_ref[:, 0:cols])
        b = _dot(x, w3_ref[:, 0:cols])
        if with_ada:
            mods_ref[...] = _dot(sc_ref[...], wa_ref[...]) + ba_ref[...]
        o_ref[:, 0:cols] = (a * jax.nn.sigmoid(a) * b).astype(o_ref.dtype)

    def step(with_ada):
        if tail == 0:
            body(o_ref.shape[1], with_ada)
        else:
            @pl.when(j < n_full)
            def _():
                body(o_ref.shape[1], with_ada)

            @pl.when(j == n_full)
            def _():
                body(tail, with_ada)

    if ada_tiles:
        n_steps = pl.num_programs(0) * pl.num_programs(1)
        s = i * pl.num_programs(1) + j
        fresh = jnp.logical_or(
            s == 0, (s * ada_tiles) // n_steps != ((jnp.maximum(s, 1) - 1) * ada_tiles) // n_steps)

        @pl.when(fresh)
        def _():
            step(True)

        @pl.when(jnp.logical_not(fresh))
        def _():
            step(False)
    else:
        step(False)


def _swiglu_up(h, w1, w3, layer, sub, ada=None, tn=256):
    m, d = h.shape
    f = w1.shape[-1]
    tm = MM_ROW_TILE
    n_full, tail = divmod(f, tn)
    nj = pl.cdiv(f, tn)
    grid = (m // tm, nj)
    w_spec = pl.BlockSpec((None, None, d, tn), lambda i, j: (layer, sub, 0, j))
    in_specs = [pl.BlockSpec((tm, d), lambda i, j: (i, 0), pipeline_mode=pl.Buffered(1)), w_spec, w_spec]
    args = [h, w1, w3]
    out_specs = [pl.BlockSpec((tm, tn), lambda i, j: (i, j))]
    out_shape = [jax.ShapeDtypeStruct((m, f), BF16)]
    scratch, aliases = [], {}
    if ada is not None:
        c_all, w_ada, b_ada3, mods, first_tile, n_tiles = ada
        n_steps = grid[0] * nj
        per_layer = w_ada.shape[-1] // ADA_TILE
        per_slab = d // ADA_TILE
        assert n_tiles <= n_steps

        def tile(i, j):
            return first_tile + ((i * nj + j) * n_tiles) // n_steps

        in_specs += [
            pl.BlockSpec(c_all.shape, lambda i, j: (0, 0), pipeline_mode=pl.Buffered(1)),
            pl.BlockSpec((None, d, ADA_TILE), lambda i, j: (tile(i, j) // per_layer, 0, tile(i, j) % per_layer)),
            pl.BlockSpec((None, 1, ADA_TILE), lambda i, j: (tile(i, j) // per_layer, 0, tile(i, j) % per_layer)),
            pl.BlockSpec(memory_space=pl.ANY),
        ]
        args += [c_all, w_ada, b_ada3, mods]
        out_specs.append(pl.BlockSpec(
            (None, None, c_all.shape[0], ADA_TILE),
            lambda i, j: (tile(i, j) // per_layer, (tile(i, j) % per_layer) // per_slab, 0,
                          tile(i, j) % per_slab)))
        out_shape.append(jax.ShapeDtypeStruct(mods.shape, mods.dtype))
        scratch = [pltpu.VMEM(c_all.shape, BF16)]
        aliases = {len(args) - 1: 1}
    res = pl.pallas_call(
        functools.partial(_swiglu_up_kernel, n_full, tail, 0 if ada is None else ada[5]),
        grid=grid,
        in_specs=in_specs,
        out_specs=out_specs,
        out_shape=out_shape,
        scratch_shapes=scratch,
        input_output_aliases=aliases,
        compiler_params=pltpu.CompilerParams(
            dimension_semantics=("arbitrary", "arbitrary"),
            vmem_limit_bytes=VMEM_LIMIT if ada is None else ACC_VMEM_LIMIT),
        name="swiglu_up",
    )(*args)
    return res if ada is not None else res[0]


def _proj_kernel(x_ref, w_ref, b_ref, o_ref):
    o_ref[...] = (_dot_nt(x_ref[...], w_ref[...]) + b_ref[...]).astype(o_ref.dtype)


def _in_proj(h, w_in_t, b_in3, layer, n_main, tn=512):
    m, d = h.shape
    tm = MM_ROW_TILE
    return pl.pallas_call(
        _proj_kernel,
        grid=(m // tm, n_main // tn),
        in_specs=[pl.BlockSpec((tm, d), lambda i, j: (i, 0), pipeline_mode=pl.Buffered(1)),
                  pl.BlockSpec((None, tn, d), lambda i, j: (layer, j, 0)),
                  pl.BlockSpec((None, 1, tn), lambda i, j: (layer, 0, j))],
        out_specs=pl.BlockSpec((tm, tn), lambda i, j: (i, j)),
        out_shape=jax.ShapeDtypeStruct((m, n_main), BF16),
        compiler_params=_cparams(("arbitrary", "arbitrary")),
        name="in_proj",
    )(h, w_in_t, b_in3)


def _gate_proj(h, w_in_t, b_in3, layer, n_main, tn=128):
    m, d = h.shape
    tm = MM_ROW_TILE
    jb = n_main // tn
    return pl.pallas_call(
        _proj_kernel,
        grid=(m // tm,),
        in_specs=[pl.BlockSpec((tm, d), lambda i: (i, 0)),
                  pl.BlockSpec((None, tn, d), lambda i: (layer, jb, 0)),
                  pl.BlockSpec((None, 1, tn), lambda i: (layer, 0, jb))],
        out_specs=pl.BlockSpec((tm, tn), lambda i: (i, 0)),
        out_shape=jax.ShapeDtypeStruct((m, tn), F32),
        compiler_params=_cparams(("arbitrary",)),
        name="gate_proj",
    )(h, w_in_t, b_in3)


def _gate_segments(tm, n_tiles, n_bp, seq, n_bs):
    rows_p = n_bp * seq
    tiles = []
    for t in range(n_tiles):
        segs, r = [], t * tm
        while r < (t + 1) * tm:
            if r < rows_p:
                b = r // seq
                end = min((b + 1) * seq, (t + 1) * tm)
                segs.append((r - t * tm, end - t * tm, b))
            else:
                end = r + n_bs
                assert (r - rows_p) % n_bs == 0 and end <= (t + 1) * tm
                segs.append((r - t * tm, end - t * tm, None))
            r = end
        tiles.append(segs)
    return tiles


def _acc_resid_kernel(segments, k_last_valid, res_scale, a_ref, w_ref, x_hbm, gp_ref, gs_ref, o_ref,
                      x_ref, x_sem):
    i = pl.program_id(0)
    j = pl.program_id(1)
    k = pl.program_id(2)
    nk = pl.num_programs(2)
    tm, tk = a_ref.shape
    tn = o_ref.shape[1]

    def resid_copy():
        return pltpu.make_async_copy(
            x_hbm.at[pl.ds(i * tm, tm), pl.ds(j * tn, tn)], x_ref, x_sem)

    def sweep(kv, first):
        for r0 in range(0, tm, ACC_SUB_ROWS):
            rs = slice(r0, r0 + ACC_SUB_ROWS)
            part = _dot(a_ref[rs, 0:kv], w_ref[0:kv, :])
            if first:
                o_ref[rs, :] = part
            else:
                o_ref[rs, :] += part

    @pl.when(k == 0)
    def _():
        resid_copy().start()
        sweep(tk, True)

    @pl.when(jnp.logical_and(k > 0, k < nk - 1))
    def _():
        sweep(tk, False)

    @pl.when(k == nk - 1)
    def _():
        resid_copy().wait()

    for t, segs in enumerate(segments):
        @pl.when(jnp.logical_and(k == nk - 1, i == t))
        def _():
            for r0 in range(0, tm, ACC_SUB_ROWS):
                r1 = r0 + ACC_SUB_ROWS
                part = _dot(a_ref[r0:r1, 0:k_last_valid], w_ref[0:k_last_valid, :])
                for s0, s1, b in segs:
                    q0, q1 = max(s0, r0), min(s1, r1)
                    if q0 >= q1:
                        continue
                    gate = gs_ref[q0 - s0:q1 - s0, :] if b is None else gp_ref[b]
                    o_ref[q0:q1, :] = (x_ref[q0:q1, :]
                                       + (res_scale * gate) * (o_ref[q0:q1, :] + part[q0 - r0:q1 - r0, :]))


def _acc_resid(a, w_full, w_index, x, mods, mods_p, layer, which, sample_rows, res_scale):
    m, kdim = a.shape
    d = x.shape[1]
    tm, tk, tn = ACC_ROW_TILE, ACC_K_TILE, ACC_N_TILE
    n_bp = mods_p.shape[2]
    n_bs = mods.shape[2] - 8
    seq = (m - sample_rows) // n_bp
    assert m % tm == 0 and tm % ACC_SUB_ROWS == 0
    nk = pl.cdiv(kdim, tk)
    assert nk >= 2
    k_last_valid = kdim - (nk - 1) * tk
    lead = (None,) * len(w_index)
    segments = _gate_segments(tm, m // tm, n_bp, seq, n_bs)
    return pl.pallas_call(
        functools.partial(_acc_resid_kernel, segments, k_last_valid, res_scale),
        grid=(m // tm, d // tn, nk),
        in_specs=[
            pl.BlockSpec((tm, tk), lambda i, j, k: (i, k)),
            pl.BlockSpec(lead + (tk, tn), lambda i, j, k: tuple(w_index) + (k, j)),
            pl.BlockSpec(memory_space=pl.ANY),
            pl.BlockSpec((None, None, n_bp, 1, tn), lambda i, j, k: (layer, which, 0, 0, j)),
            pl.BlockSpec((None, None, n_bs, tn), lambda i, j, k: (layer, which, 0, j)),
        ],
        out_specs=pl.BlockSpec((tm, tn), lambda i, j, k: (i, j)),
        out_shape=jax.ShapeDtypeStruct((m, d), F32),
        scratch_shapes=[pltpu.VMEM((tm, tn), F32), pltpu.SemaphoreType.DMA(())],
        compiler_params=pltpu.CompilerParams(
            dimension_semantics=("arbitrary", "arbitrary", "arbitrary"),
            vmem_limit_bytes=ACC_VMEM_LIMIT),
        name="acc_resid",
    )(a, w_full, x, mods_p, mods)


N_HIST = max(POOL_WINDOWS) - 1
HALO = 32
LEVEL_ROW0 = 8


def _pool_prompt_kernel(start, u_ref, wp_ref, sp_ref, y_ref, hist_ref, z_ref, sa_ref, sb_ref):
    t = pl.program_id(1)
    nt = pl.num_programs(1)
    tt = u_ref.shape[0]
    group = wp_ref.shape[1]
    n = HALO + tt
    n_hist = hist_ref.shape[0]

    @pl.when(t == 0)
    def _():
        z_ref[0:HALO, :] = jnp.zeros((HALO, z_ref.shape[1]), F32)
        sa_ref[0:LEVEL_ROW0, :] = jnp.zeros((LEVEL_ROW0, group), F32)
        sb_ref[0:LEVEL_ROW0, :] = jnp.zeros((LEVEL_ROW0, group), F32)

    z_ref[HALO:n, :] = u_ref[...].astype(F32)

    def window_sum(cols, w):
        src, src_cols, sh, level = z_ref, cols, 1, 0
        while 2 * sh < w:
            dst = (sa_ref, sb_ref)[level % 2]
            dst[LEVEL_ROW0:n, :] = src[LEVEL_ROW0:n, src_cols] + src[LEVEL_ROW0 - sh:n - sh, src_cols]
            src, src_cols, sh, level = dst, slice(None), 2 * sh, level + 1
        return src[HALO:n, src_cols] + src[HALO - sh:n - sh, src_cols]

    pos = start + t * tt + lax.broadcasted_iota(jnp.int32, (tt, 1), 0)
    for g, w in enumerate(POOL_WINDOWS):
        assert w & (w - 1) == 0 and LEVEL_ROW0 + 2 * (w // 2 - 1) <= HALO
        cols = slice(g * group, (g + 1) * group)
        cur = z_ref[HALO:n, cols]
        cnt = jnp.minimum(w, pos + 1).astype(F32)
        pooled = window_sum(cols, w) / cnt - cur
        y = _dot(pooled.astype(BF16), wp_ref[g]) * sp_ref[:, cols]
        y_ref[:, cols] = y.astype(y_ref.dtype)

    @pl.when(t == nt - 1)
    def _():
        hist_ref[...] = z_ref[n - n_hist:n, :]

    z_ref[0:HALO, :] = z_ref[tt:n, :]


def _pool_prompt(proj, w_pool_l, s_pool_l, n_b, seq, d, tt=256):
    pw = s_pool_l.shape[-1]
    ntt = seq // tt
    n_groups, group, _ = w_pool_l.shape
    return pl.pallas_call(
        functools.partial(_pool_prompt_kernel, 0),
        grid=(n_b, ntt),
        in_specs=[
            pl.BlockSpec((tt, pw), lambda b, t: (b * ntt + t, 0)),
            pl.BlockSpec((n_groups, group, group), lambda b, t: (0, 0, 0)),
            pl.BlockSpec((1, pw), lambda b, t: (0, 0)),
        ],
        out_specs=[
            pl.BlockSpec((tt, pw), lambda b, t: (b * ntt + t, 0)),
            pl.BlockSpec((None, N_HIST, pw), lambda b, t: (b, 0, 0)),
        ],
        out_shape=[jax.ShapeDtypeStruct((proj.shape[0], d), BF16),
                   jax.ShapeDtypeStruct((n_b, N_HIST, pw), F32)],
        scratch_shapes=[pltpu.VMEM((HALO + tt, pw), F32),
                        pltpu.VMEM((HALO + tt, group), F32),
                        pltpu.VMEM((HALO + tt, group), F32)],
        compiler_params=_cparams(("arbitrary", "arbitrary")),
        name="pool_prompt",
    )(proj, w_pool_l, s_pool_l.reshape(1, pw))


def _pool_sample_kernel(hist_ref, u_ref, wp_ref, sp_ref, *rest):
    y_ref, nh_ref = rest[-2:]
    n_hist, n_b, _ = hist_ref.shape
    n_t = u_ref.shape[0] // n_b
    g = pl.program_id(0)

    def z(r):
        if r < n_hist:
            return hist_ref[r]
        return u_ref[(r - n_hist) * n_b:(r - n_hist + 1) * n_b, :].astype(F32)

    for r in range(n_hist):
        nh_ref[r] = z(r + n_t)

    for t in range(n_t):
        cur = z(n_hist + t)
        run = cur
        sums = []
        for j in range(1, POOL_WINDOWS[-1]):
            run = run + z(n_hist + t - j)
            if j + 1 in POOL_WINDOWS:
                sums.append(run)
        pooled = jnp.zeros_like(cur)
        for gi, w in enumerate(POOL_WINDOWS):
            pooled = jnp.where(g == gi, sums[gi] / float(w) - cur, pooled)
        y = _dot(pooled.astype(BF16), wp_ref[...]) * sp_ref[...]
        y_ref[t * n_b:(t + 1) * n_b, :] = y.astype(y_ref.dtype)


def _pool_sample(state_tm, layer, hist_prev, proj, w_pool_l, s_pool_l, ymix, row0):
    depth, n_hist, n_b, pw = state_tm.shape
    n_groups, group, _ = w_pool_l.shape
    rows = proj.shape[0] - row0
    rb = row0 // rows
    args = [state_tm, proj, w_pool_l, s_pool_l.reshape(1, pw), ymix]
    in_specs = [
        pl.BlockSpec((None, n_hist, n_b, group), lambda g: (layer, 0, 0, g)),
        pl.BlockSpec((rows, group), lambda g: (rb, g)),
        pl.BlockSpec((None, group, group), lambda g: (g, 0, 0)),
        pl.BlockSpec((1, group), lambda g: (0, g)),
        pl.BlockSpec(memory_space=pl.ANY),
    ]
    aliases = {4: 0}
    if hist_prev is not None:
        args.append(hist_prev)
        in_specs.append(pl.BlockSpec(memory_space=pl.ANY))
        aliases[5] = 1
    return pl.pallas_call(
        _pool_sample_kernel,
        grid=(n_groups,),
        in_specs=in_specs,
        out_specs=[pl.BlockSpec((rows, group), lambda g: (rb, g)),
                   pl.BlockSpec((None, n_hist, n_b, group), lambda g: (layer, 0, 0, g))],
        out_shape=[jax.ShapeDtypeStruct(ymix.shape, ymix.dtype),
                   jax.ShapeDtypeStruct(state_tm.shape, F32)],
        input_output_aliases=aliases,
        compiler_params=_cparams(("arbitrary",)),
        name="pool_sample",
    )(*args)


def _mlstm_segment(q, k, v, ig, lf, c_state, n_state, m_state):
    r = q.shape[0]
    row = lax.broadcasted_iota(jnp.int32, (r, r), 0)
    col = lax.broadcasted_iota(jnp.int32, (r, r), 1)
    causal = col <= row
    lf_rows = jnp.sum(jnp.where(row == col, lf, 0.0), axis=0, keepdims=True)
    ig_rows = jnp.sum(jnp.where(row == col, ig, 0.0), axis=0, keepdims=True)
    b_col = jnp.sum(jnp.where(causal, lf_rows, 0.0), axis=1, keepdims=True)
    b_rows = jnp.sum(jnp.where(row <= col, lf, 0.0), axis=0, keepdims=True)
    dmat = jnp.where(causal, b_col - b_rows + ig_rows, -jnp.inf)
    inter = b_col + m_state
    m_tok = jnp.maximum(inter, jnp.max(dmat, axis=-1, keepdims=True))
    w_intra = jnp.exp(dmat - m_tok)
    w_inter = jnp.exp(inter - m_tok)
    s = _dot_nt(q, k) * w_intra
    num = _dot(s.astype(BF16), v) + w_inter * _dot_nt(q, c_state)
    qn = jnp.sum(q.astype(F32) * n_state, axis=-1, keepdims=True)
    den = jnp.sum(s, axis=-1, keepdims=True) + w_inter * qn
    h = num * (1.0 / jnp.maximum(jnp.abs(den), jnp.exp(-m_tok)))
    b_last = jnp.sum(lf, axis=0, keepdims=True)
    dec = b_last - b_col + ig
    m_new = jnp.maximum(b_last + m_state, jnp.max(dec, axis=0, keepdims=True))
    ws = jnp.exp(dec - m_new)
    wc = jnp.exp(b_last + m_state - m_new)
    wk = ws * k.astype(F32)
    c_new = wc * c_state + _dot_tn(v, wk.astype(BF16))
    n_new = wc * n_state + jnp.sum(wk, axis=0, keepdims=True)
    return h, c_new, n_new, m_new


def _head_out(h, o, g_head):
    hn = h * lax.rsqrt(jnp.mean(h * h, axis=-1, keepdims=True) + EPS) * g_head
    return hn * jax.nn.sigmoid(o.astype(F32))


def _log_sigmoid(x):
    return jnp.minimum(x, 0.0) - jnp.log1p(jnp.exp(-jnp.abs(x)))


def _mlstm_prompt_kernel(q_ref, k_ref, v_ref, o_ref, gt_ref, gh_ref, ymix_in_ref,
                         y_ref, c_out_ref, n_out_ref, m_out_ref, c_s, n_s, m_s):
    del ymix_in_ref
    c = pl.program_id(1)
    nc = pl.num_programs(1)
    dk = c_s.shape[2]
    dv = c_s.shape[1]

    @pl.when(c == 0)
    def _():
        c_s[...] = jnp.zeros(c_s.shape, F32)
        n_s[...] = jnp.zeros(n_s.shape, F32)
        m_s[...] = jnp.zeros(m_s.shape, F32)

    gates = gt_ref[...]
    log_f = _log_sigmoid(gates)
    for h in range(N_HEADS):
        q = q_ref[:, h * dk:(h + 1) * dk]
        k = (k_ref[:, h * dk:(h + 1) * dk].astype(F32) * (dk ** -0.5)).astype(BF16)
        v = v_ref[:, h * dv:(h + 1) * dv]
        ig = gates[:, h:h + 1]
        lf = log_f[:, N_HEADS + h:N_HEADS + h + 1]
        hh, c_new, n_new, m_new = _mlstm_segment(
            q, k, v, ig, lf, c_s[h], n_s[h:h + 1, :], m_s[h:h + 1, 0:1])
        c_s[h] = c_new
        n_s[h:h + 1, :] = n_new
        m_s[h:h + 1, :] = jnp.broadcast_to(m_new, (1, m_s.shape[1]))
        y_ref[:, h * dv:(h + 1) * dv] = _head_out(
            hh, o_ref[:, h * dv:(h + 1) * dv], gh_ref[h:h + 1, :]).astype(y_ref.dtype)

    @pl.when(c == nc - 1)
    def _():
        c_out_ref[...] = c_s[...]
        n_out_ref[...] = n_s[0:N_HEADS, :]
        m_out_ref[...] = m_s[...]


def _mlstm_prompt(proj, gates, g_head_l, ymix, n_b, seq, col_q, dk, dv):
    lc = PROMPT_CHUNK
    nch = seq // lc
    qk_w = N_HEADS * dk
    v_w = N_HEADS * dv
    row = lambda b, c: b * nch + c
    assert col_q % qk_w == 0 and (col_q + 2 * qk_w) % v_w == 0
    bq = col_q // qk_w
    bv = (col_q + 2 * qk_w) // v_w
    y, c_out, n_out, m_out = pl.pallas_call(
        _mlstm_prompt_kernel,
        grid=(n_b, nch),
        in_specs=[
            pl.BlockSpec((lc, qk_w), lambda b, c: (row(b, c), bq)),
            pl.BlockSpec((lc, qk_w), lambda b, c: (row(b, c), bq + 1)),
            pl.BlockSpec((lc, v_w), lambda b, c: (row(b, c), bv)),
            pl.BlockSpec((lc, v_w), lambda b, c: (row(b, c), bv + 1)),
            pl.BlockSpec((lc, gates.shape[1]), lambda b, c: (row(b, c), 0)),
            pl.BlockSpec((N_HEADS, dv), lambda b, c: (0, 0)),
            pl.BlockSpec(memory_space=pl.ANY),
        ],
        out_specs=[
            pl.BlockSpec((lc, v_w), lambda b, c: (row(b, c), 1)),
            pl.BlockSpec((None, N_HEADS, dv, dk), lambda b, c: (b, 0, 0, 0)),
            pl.BlockSpec((None, N_HEADS, dk), lambda b, c: (b, 0, 0)),
            pl.BlockSpec((None, 8, 128), lambda b, c: (b, 0, 0)),
        ],
        out_shape=[jax.ShapeDtypeStruct(ymix.shape, ymix.dtype),
                   jax.ShapeDtypeStruct((n_b, N_HEADS, dv, dk), F32),
                   jax.ShapeDtypeStruct((n_b, N_HEADS, dk), F32),
                   jax.ShapeDtypeStruct((n_b, 8, 128), F32)],
        scratch_shapes=[pltpu.VMEM((N_HEADS, dv, dk), F32),
                        pltpu.VMEM((8, dk), F32),
                        pltpu.VMEM((8, 128), F32)],
        input_output_aliases={6: 0},
        compiler_params=_cparams(("arbitrary", "arbitrary")),
        name="mlstm_prompt",
    )(proj, proj, proj, proj, gates, g_head_l, ymix)
    return y, c_out, n_out, m_out[:, :N_HEADS, 0]


def _mlstm_sample_kernel(seq, q_ref, k_ref, v_ref, o_ref, gt_ref, gh_ref, c0_ref, n0_ref, m0_ref,
                         *rest):
    y_ref, c_out_ref, n_out_ref, m_out_ref = rest[-4:]
    rows = q_ref.shape[0]
    dk = c0_ref.shape[3]
    dv = c0_ref.shape[2]
    seg_of_row = lax.broadcasted_iota(jnp.int32, (rows, 1), 0) // seq
    gates = gt_ref[...]
    log_f = _log_sigmoid(gates)
    for h in range(N_HEADS):
        q = q_ref[:, h * dk:(h + 1) * dk]
        k = (k_ref[:, h * dk:(h + 1) * dk].astype(F32) * (dk ** -0.5)).astype(BF16)
        v = v_ref[:, h * dv:(h + 1) * dv]
        ig_all = gates[:, h:h + 1]
        lf_all = log_f[:, N_HEADS + h:N_HEADS + h + 1]
        hh = jnp.zeros((rows, dv), F32)
        for sgm in range(rows // seq):
            mine = seg_of_row == sgm
            ig = jnp.where(mine, ig_all, NEG_BIG)
            lf = jnp.where(mine, lf_all, 0.0)
            h_seg, c_new, n_new, m_new = _mlstm_segment(
                q, k, v, ig, lf, c0_ref[sgm, h], n0_ref[sgm, h:h + 1, :], m0_ref[sgm, h:h + 1, :])
            hh = jnp.where(mine, h_seg, hh)
            c_out_ref[sgm, h] = c_new
            n_out_ref[sgm, h:h + 1, :] = n_new
            m_out_ref[sgm, h:h + 1, :] = m_new
        y_ref[:, h * dv:(h + 1) * dv] = _head_out(
            hh, o_ref[:, h * dv:(h + 1) * dv], gh_ref[h:h + 1, :]).astype(y_ref.dtype)


def _mlstm_sample(proj_bm, gates_bm, g_head_l, state_c, state_n, state_m, layer, c_prev, seq, col_q, dk, dv):
    depth, n_b = state_c.shape[:2]
    grp = SAMPLE_GROUP
    rows = grp * seq
    qk_w = N_HEADS * dk
    v_w = N_HEADS * dv
    bq = col_q // qk_w
    bv = (col_q + 2 * qk_w) // v_w
    has_prev = c_prev is not None
    in_specs = [
        pl.BlockSpec((rows, qk_w), lambda i: (i, bq)),
        pl.BlockSpec((rows, qk_w), lambda i: (i, bq + 1)),
        pl.BlockSpec((rows, v_w), lambda i: (i, bv)),
        pl.BlockSpec((rows, v_w), lambda i: (i, bv + 1)),
        pl.BlockSpec((rows, gates_bm.shape[1]), lambda i: (i, 0)),
        pl.BlockSpec((N_HEADS, dv), lambda i: (0, 0)),
        pl.BlockSpec((None, grp, N_HEADS, dv, dk), lambda i: (layer, i, 0, 0, 0)),
        pl.BlockSpec((None, grp, N_HEADS, dk), lambda i: (layer, i, 0, 0)),
        pl.BlockSpec((None, grp, N_HEADS, 1), lambda i: (layer, i, 0, 0)),
    ]
    args = [proj_bm, proj_bm, proj_bm, proj_bm, gates_bm, g_head_l, state_c, state_n,
            state_m.reshape(depth, n_b, N_HEADS, 1)]
    aliases = {}
    if has_prev:
        in_specs.append(pl.BlockSpec(memory_space=pl.ANY))
        args.append(c_prev)
        aliases = {len(args) - 1: 1}
    y, c_out, n_out, m_out = pl.pallas_call(
        functools.partial(_mlstm_sample_kernel, seq),
        grid=(n_b // grp,),
        in_specs=in_specs,
        out_specs=[
            pl.BlockSpec((rows, v_w), lambda i: (i, 0)),
            pl.BlockSpec((None, grp, N_HEADS, dv, dk), lambda i: (layer, i, 0, 0, 0)),
            pl.BlockSpec((grp, N_HEADS, dk), lambda i: (i, 0, 0)),
            pl.BlockSpec((grp, N_HEADS, 1), lambda i: (i, 0, 0)),
        ],
        out_shape=[jax.ShapeDtypeStruct((n_b * seq, v_w), BF16),
                   jax.ShapeDtypeStruct(state_c.shape, F32),
                   jax.ShapeDtypeStruct((n_b, N_HEADS, dk), F32),
                   jax.ShapeDtypeStruct((n_b, N_HEADS, 1), F32)],
        input_output_aliases=aliases,
        compiler_params=_cparams(("arbitrary",)),
        name="mlstm_sample",
    )(*args)
    return y, c_out, n_out, m_out[:, :, 0]


def _to_time_major(a, n_b, seq):
    return jnp.transpose(a.reshape(n_b, seq, -1), (1, 0, 2)).reshape(n_b * seq, -1)


def _to_batch_major(a, n_b, seq):
    return jnp.transpose(a.reshape(seq, n_b, -1), (1, 0, 2)).reshape(n_b * seq, -1)


def kernel(x_prompt, x_sample, state_pool, state_C, state_n, state_m, c_prompt, c_sample,
           w_ada, b_ada, g_norm, w_in, b_in, w_pool, s_pool, g_head, w_out, w1, w3, w2, g_final):
    n_bp, seq_p, d = x_prompt.shape
    n_bs, seq_s, _ = x_sample.shape
    depth = w_ada.shape[0]
    pw = s_pool.shape[-1]
    dv = g_head.shape[-1]
    dk = state_C.shape[-1]
    n_hist = state_pool.shape[2]
    rows_p = n_bp * seq_p
    rows_s = n_bs * seq_s
    n_main = pw + 2 * N_HEADS * dk + 2 * N_HEADS * dv

    xp0 = x_prompt.reshape(rows_p, d)
    xs0 = _to_time_major(x_sample, n_bs, seq_s)

    c_all = jnp.concatenate([c_sample, c_prompt, jnp.zeros((8 - n_bp, d), F32)], axis=0)
    b_ada3 = b_ada.reshape(depth, 1, b_ada.shape[-1])
    tiles_per_slab = d // ADA_TILE
    tiles_per_layer = N_SUB * N_MOD * tiles_per_slab
    assert depth == 2
    mods = _ada_head(c_all, w_ada, b_ada3, N_MOD)
    ada_jobs = {(0, 0): (N_MOD * tiles_per_slab, tiles_per_layer - N_MOD * tiles_per_slab),
                (0, 1): (tiles_per_layer, tiles_per_layer)}

    def prompt_rows(mods):
        return mods[:, :, n_bs:n_bs + n_bp][:, :, :, None, :]

    b_in3 = b_in.reshape(depth, 1, b_in.shape[-1])
    w_in_t = jnp.transpose(w_in, (0, 2, 1))
    pool_p, c_p, n_p, m_p = [], [], [], []
    n_s, m_s = [], []
    c_s_all = None
    hist_s_tm = None
    state_pool_tm = jnp.transpose(state_pool, (0, 2, 1, 3))

    def ffn(x, h, mods, l, sub_layer, ffn_idx):
        job = ada_jobs.get((l, ffn_idx))
        if job is None:
            act = _swiglu_up(h, w1, w3, l, ffn_idx)
        else:
            act, mods = _swiglu_up(h, w1, w3, l, ffn_idx, (c_all, w_ada, b_ada3, mods) + job)
        x = _acc_resid(act, w2, (l, ffn_idx), x, mods, prompt_rows(mods), l, sub_layer * N_MOD + 2,
                       rows_s, FFN_RES)
        return x, mods

    x = None
    for l in range(depth):
        mods_p = prompt_rows(mods)
        if l == 0:
            h, x = _normmod(xp0, xs0, 0, g_norm[l, 0], mods, mods_p, l, 0, seq_p, True)
        else:
            h, = _normmod(x, x, rows_p, g_norm[l, 0], mods, mods_p, l, 0, seq_p, False)
        x, mods = ffn(x, h, mods, l, 0, 0)
        mods_p = prompt_rows(mods)

        h, = _normmod(x, x, rows_p, g_norm[l, 1], mods, mods_p, l, 1, seq_p, False)
        proj = _in_proj(h, w_in_t, b_in3, l, n_main)
        gates = _gate_proj(h, w_in_t, b_in3, l, n_main)

        ymix, hist_p = _pool_prompt(proj, w_pool[l], s_pool[l], n_bp, seq_p, d)
        ymix, cp, np_, mp = _mlstm_prompt(proj, gates, g_head[l], ymix, n_bp, seq_p, pw, dk, dv)
        ymix, hist_s_tm = _pool_sample(state_pool_tm, l, hist_s_tm, proj, w_pool[l], s_pool[l], ymix, rows_p)
        proj_bm = _to_batch_major(proj[rows_p:], n_bs, seq_s)
        gates_bm = _to_batch_major(gates[rows_p:], n_bs, seq_s)
        y_ms, c_s_all, ns, ms = _mlstm_sample(proj_bm, gates_bm, g_head[l], state_C, state_n, state_m,
                                              l, c_s_all, seq_s, pw, dk, dv)
        ymix = lax.dynamic_update_slice(ymix, _to_time_major(y_ms, n_bs, seq_s), (rows_p, pw))
        x = _acc_resid(ymix, w_out, (l,), x, mods, mods_p, l, 1 * N_MOD + 2, rows_s, 1.0)

        pool_p.append(hist_p); c_p.append(cp); n_p.append(np_); m_p.append(mp)
        n_s.append(ns); m_s.append(ms)

        h, = _normmod(x, x, rows_p, g_norm[l, 2], mods, mods_p, l, 2, seq_p, False)
        x, mods = ffn(x, h, mods, l, 2, 1)

    y_prompt = _final_norm(x, g_final, 0, rows_p).reshape(n_bp, seq_p, d)
    y_sample = _to_batch_major(_final_norm(x, g_final, rows_p, rows_s), n_bs, seq_s).reshape(n_bs, seq_s, d)
    return (y_prompt, y_sample,
            jnp.stack(pool_p), jnp.stack(c_p), jnp.stack(n_p), jnp.stack(m_p),
            jnp.transpose(hist_s_tm, (0, 2, 1, 3)), c_s_all, jnp.stack(n_s), jnp.stack(m_s))
```

```python
import functools

import jax
import jax.numpy as jnp
from jax import lax
from jax.experimental import pallas as pl
from jax.experimental.pallas import tpu as pltpu

F32 = jnp.float32
BF16 = jnp.bfloat16

EPS = 1e-6
FFN_RES = 0.5
POOL_WINDOWS = (2, 4, 8, 16)
N_HEADS = 4
N_SUB = 3
N_MOD = 3
PROMPT_CHUNK = 512
SAMPLE_GROUP = 4
NEG_BIG = -1e30

VMEM_LIMIT = 56 * 1024 * 1024

ROW_TILE = 256
MM_ROW_TILE = 2176
ACC_ROW_TILE = 2176
ACC_SUB_ROWS = 544
ACC_VMEM_LIMIT = 60 * 1024 * 1024
ACC_K_TILE = 1536
ACC_N_TILE = 1024


def _cparams(sem):
    return pltpu.CompilerParams(dimension_semantics=sem, vmem_limit_bytes=VMEM_LIMIT)


def _dot(a, b):
    return lax.dot_general(a, b, (((1,), (0,)), ((), ())), preferred_element_type=F32)


def _dot_nt(a, b):
    return lax.dot_general(a, b, (((1,), (1,)), ((), ())), preferred_element_type=F32)


def _dot_tn(a, b):
    return lax.dot_general(a, b, (((0,), (0,)), ((), ())), preferred_element_type=F32)


def _ada_kernel(c_ref, w_ref, b_ref, o_ref):
    c = c_ref[...]
    sc = (c * jax.nn.sigmoid(c)).astype(BF16)
    o_ref[...] = _dot(sc, w_ref[...]) + b_ref[...]


def _ada_head(c_all, w_ada, b_ada3, n_slabs, tn=1024):
    depth, d, n = w_ada.shape
    rows = c_all.shape[0]
    per = d // tn
    return pl.pallas_call(
        _ada_kernel,
        grid=(n_slabs * per,),
        in_specs=[
            pl.BlockSpec((rows, d), lambda t: (0, 0)),
            pl.BlockSpec((None, d, tn), lambda t: (0, 0, t)),
            pl.BlockSpec((None, 1, tn), lambda t: (0, 0, t)),
        ],
        out_specs=pl.BlockSpec((None, None, rows, tn), lambda t: (0, t // per, 0, t % per)),
        out_shape=jax.ShapeDtypeStruct((depth, n // d, rows, d), F32),
        compiler_params=_cparams(("arbitrary",)),
        name="ada_mods",
    )(c_all, w_ada, b_ada3)


def _rms(x, g):
    return x * lax.rsqrt(jnp.mean(x * x, axis=-1, keepdims=True) + EPS) * g


NORM_CHUNK = 8


def _normmod_kernel(n_prompt_tiles, emit_x, xp_ref, xs_ref, g_ref, shp_ref, scp_ref, shs_ref, scs_ref,
                    h_ref, *rest):
    xo_ref = rest[0] if emit_x else None
    gm_ref = rest[-1]
    i = pl.program_id(0)
    n_bs = shs_ref.shape[0]
    n_chunks = h_ref.shape[0] // NORM_CHUNK

    def run(x_ref, mod_rows):
        def chunk(c, carry):
            r = pl.multiple_of(c * NORM_CHUNK, NORM_CHUNK)
            rows = pl.ds(r, NORM_CHUNK)
            x = x_ref[rows, :]
            rstd = lax.rsqrt(jnp.mean(x * x, axis=-1, keepdims=True) + EPS)
            gm, sh = mod_rows(r)
            h_ref[rows, :] = (x * rstd * gm + sh).astype(h_ref.dtype)
            if emit_x:
                xo_ref[rows, :] = x
            return carry

        lax.fori_loop(0, n_chunks, chunk, 0, unroll=4)

    @pl.when(i < n_prompt_tiles)
    def _():
        gm_ref[0:1, :] = g_ref[...] * (1.0 + scp_ref[...])
        run(xp_ref, lambda r: (gm_ref[0:1, :], shp_ref[...]))

    @pl.when(i >= n_prompt_tiles)
    def _():
        gm_ref[...] = g_ref[...] * (1.0 + scs_ref[...])

        def mod_rows(r):
            rb = pl.ds(pl.multiple_of(lax.rem(r, n_bs), NORM_CHUNK), NORM_CHUNK)
            return gm_ref[rb, :], shs_ref[rb, :]

        run(xs_ref, mod_rows)


def _normmod(xp, xs, xs_row0, g, mods, mods_p, layer, sub, seq, emit_x):
    d = xp.shape[1]
    tr = ROW_TILE if emit_x else 2 * ROW_TILE
    n_bp = mods_p.shape[2]
    n_bs = mods.shape[2] - 8
    rows_p = n_bp * seq
    rows_s = xs.shape[0] - xs_row0
    m = rows_p + rows_s
    n_pt = rows_p // tr
    tiles_per_seq = seq // tr
    s0 = xs_row0 // tr
    i_shift, i_scale = sub * N_MOD, sub * N_MOD + 1
    p_map = lambda which: (lambda i: (layer, which, jnp.minimum(i // tiles_per_seq, n_bp - 1), 0, 0))
    s_map = lambda which: (lambda i: (layer, which, 0, 0))
    out_specs = [pl.BlockSpec((tr, d), lambda i: (i, 0))]
    out_shape = [jax.ShapeDtypeStruct((m, d), BF16)]
    if emit_x:
        out_specs.append(pl.BlockSpec((tr, d), lambda i: (i, 0)))
        out_shape.append(jax.ShapeDtypeStruct((m, d), F32))
    return pl.pallas_call(
        functools.partial(_normmod_kernel, n_pt, emit_x),
        grid=(m // tr,),
        in_specs=[
            pl.BlockSpec((tr, d), lambda i: (jnp.minimum(i, n_pt - 1), 0)),
            pl.BlockSpec((tr, d), lambda i: (s0 + jnp.maximum(i - n_pt, 0), 0)),
            pl.BlockSpec((1, d), lambda i: (0, 0)),
            pl.BlockSpec((None, None, None, 1, d), p_map(i_shift)),
            pl.BlockSpec((None, None, None, 1, d), p_map(i_scale)),
            pl.BlockSpec((None, None, n_bs, d), s_map(i_shift)),
            pl.BlockSpec((None, None, n_bs, d), s_map(i_scale)),
        ],
        out_specs=out_specs,
        out_shape=out_shape,
        scratch_shapes=[pltpu.VMEM((n_bs, d), F32)],
        compiler_params=_cparams(("arbitrary",)),
        name="normmod",
    )(xp, xs, g.reshape(1, d), mods_p, mods_p, mods, mods)


def _final_norm_kernel(x_ref, g_ref, o_ref):
    def chunk(c, carry):
        rows = pl.ds(pl.multiple_of(c * NORM_CHUNK, NORM_CHUNK), NORM_CHUNK)
        o_ref[rows, :] = _rms(x_ref[rows, :], g_ref[...])
        return carry

    lax.fori_loop(0, o_ref.shape[0] // NORM_CHUNK, chunk, 0, unroll=4)


def _final_norm(x, g, row0, rows):
    d = x.shape[1]
    tr = ROW_TILE
    b0 = row0 // tr
    return pl.pallas_call(
        _final_norm_kernel,
        grid=(rows // tr,),
        in_specs=[pl.BlockSpec((tr, d), lambda i: (i + b0, 0)),
                  pl.BlockSpec((1, d), lambda i: (0, 0))],
        out_specs=pl.BlockSpec((tr, d), lambda i: (i, 0)),
        out_shape=jax.ShapeDtypeStruct((rows, d), F32),
        compiler_params=_cparams(("arbitrary",)),
        name="final_norm",
    )(x, g.reshape(1, d))


ADA_TILE = 512


def _swiglu_up_kernel(n_full, tail, ada_tiles, x_ref, w1_ref, w3_ref, *rest):
    i = pl.program_id(0)
    j = pl.program_id(1)
    if ada_tiles:
        c_ref, wa_ref, ba_ref, _, o_ref, mods_ref, sc_ref = rest

        @pl.when(jnp.logical_and(i == 0, j == 0))
        def _():
            c = c_ref[...]
            sc_ref[...] = (c * jax.nn.sigmoid(c)).astype(BF16)
    else:
        o_ref, = rest

    def body(cols, with_ada):
        x = x_ref[...]
        a = _dot(x, w1_ref[:, 0:cols])
        b = _dot(x, w3_ref[:, 0:cols])
        if with_ada:
            mods_ref[...] = _dot(sc_ref[...], wa_ref[...]) + ba_ref[...]
        o_ref[:, 0:cols] = (a * jax.nn.sigmoid(a) * b).astype(o_ref.dtype)

    def step(with_ada):
        if tail == 0:
            body(o_ref.shape[1], with_ada)
        else:
            @pl.when(j < n_full)
            def _():
                body(o_ref.shape[1], with_ada)

            @pl.when(j == n_full)
            def _():
                body(tail, with_ada)

    if ada_tiles:
        n_steps = pl.num_programs(0) * pl.num_programs(1)
        s = i * pl.num_programs(1) + j
        fresh = jnp.logical_or(
            s == 0, (s * ada_tiles) // n_steps != ((jnp.maximum(s, 1) - 1) * ada_tiles) // n_steps)

        @pl.when(fresh)
        def _():
            step(True)

        @pl.when(jnp.logical_not(fresh))
        def _():
            step(False)
    else:
        step(False)


def _swiglu_up(h, w1, w3, layer, sub, ada=None, tn=256):
    m, d = h.shape
    f = w1.shape[-1]
    tm = MM_ROW_TILE
    n_full, tail = divmod(f, tn)
    nj = pl.cdiv(f, tn)
    grid = (m // tm, nj)
    w_spec = pl.BlockSpec((None, None, d, tn), lambda i, j: (layer, sub, 0, j))
    in_specs = [pl.BlockSpec((tm, d), lambda i, j: (i, 0), pipeline_mode=pl.Buffered(1)), w_spec, w_spec]
    args = [h, w1, w3]
    out_specs = [pl.BlockSpec((tm, tn), lambda i, j: (i, j))]
    out_shape = [jax.ShapeDtypeStruct((m, f), BF16)]
    scratch, aliases = [], {}
    if ada is not None:
        c_all, w_ada, b_ada3, mods, first_tile, n_tiles = ada
        n_steps = grid[0] * nj
        per_layer = w_ada.shape[-1] // ADA_TILE
        per_slab = d // ADA_TILE
        assert n_tiles <= n_steps

        def tile(i, j):
            return first_tile + ((i * nj + j) * n_tiles) // n_steps

        in_specs += [
            pl.BlockSpec(c_all.shape, lambda i, j: (0, 0), pipeline_mode=pl.Buffered(1)),
            pl.BlockSpec((None, d, ADA_TILE), lambda i, j: (tile(i, j) // per_layer, 0, tile(i, j) % per_layer)),
            pl.BlockSpec((None, 1, ADA_TILE), lambda i, j: (tile(i, j) // per_layer, 0, tile(i, j) % per_layer)),
            pl.BlockSpec(memory_space=pl.ANY),
        ]
        args += [c_all, w_ada, b_ada3, mods]
        out_specs.append(pl.BlockSpec(
            (None, None, c_all.shape[0], ADA_TILE),
            lambda i, j: (tile(i, j) // per_layer, (tile(i, j) % per_layer) // per_slab, 0,
                          tile(i, j) % per_slab)))
        out_shape.append(jax.ShapeDtypeStruct(mods.shape, mods.dtype))
        scratch = [pltpu.VMEM(c_all.shape, BF16)]
        aliases = {len(args) - 1: 1}
    res = pl.pallas_call(
        functools.partial(_swiglu_up_kernel, n_full, tail, 0 if ada is None else ada[5]),
        grid=grid,
        in_specs=in_specs,
        out_specs=out_specs,
        out_shape=out_shape,
        scratch_shapes=scratch,
        input_output_aliases=aliases,
        compiler_params=pltpu.CompilerParams(
            dimension_semantics=("arbitrary", "arbitrary"),
            vmem_limit_bytes=VMEM_LIMIT if ada is None else ACC_VMEM_LIMIT),
        name="swiglu_up",
    )(*args)
    return res if ada is not None else res[0]


def _proj_kernel(x_ref, w_ref, b_ref, o_ref):
    o_ref[...] = (_dot_nt(x_ref[...], w_ref[...]) + b_ref[...]).astype(o_ref.dtype)


def _in_proj(h, w_in_t, b_in3, layer, n_main, tn=512):
    m, d = h.shape
    tm = MM_ROW_TILE
    return pl.pallas_call(
        _proj_kernel,
        grid=(m // tm, n_main // tn),
        in_specs=[pl.BlockSpec((tm, d), lambda i, j: (i, 0), pipeline_mode=pl.Buffered(1)),
                  pl.BlockSpec((None, tn, d), lambda i, j: (layer, j, 0)),
                  pl.BlockSpec((None, 1, tn), lambda i, j: (layer, 0, j))],
        out_specs=pl.BlockSpec((tm, tn), lambda i, j: (i, j)),
        out_shape=jax.ShapeDtypeStruct((m, n_main), BF16),
        compiler_params=_cparams(("arbitrary", "arbitrary")),
        name="in_proj",
    )(h, w_in_t, b_in3)


def _gate_proj(h, w_in_t, b_in3, layer, n_main, tn=128):
    m, d = h.shape
    tm = MM_ROW_TILE
    jb = n_main // tn
    return pl.pallas_call(
        _proj_kernel,
        grid=(m // tm,),
        in_specs=[pl.BlockSpec((tm, d), lambda i: (i, 0)),
                  pl.BlockSpec((None, tn, d), lambda i: (layer, jb, 0)),
                  pl.BlockSpec((None, 1, tn), lambda i: (layer, 0, jb))],
        out_specs=pl.BlockSpec((tm, tn), lambda i: (i, 0)),
        out_shape=jax.ShapeDtypeStruct((m, tn), F32),
        compiler_params=_cparams(("arbitrary",)),
        name="gate_proj",
    )(h, w_in_t, b_in3)


def _gate_segments(tm, n_tiles, n_bp, seq, n_bs):
    rows_p = n_bp * seq
    tiles = []
    for t in range(n_tiles):
        segs, r = [], t * tm
        while r < (t + 1) * tm:
            if r < rows_p:
                b = r // seq
                end = min((b + 1) * seq, (t + 1) * tm)
                segs.append((r - t * tm, end - t * tm, b))
            else:
                end = r + n_bs
                assert (r - rows_p) % n_bs == 0 and end <= (t + 1) * tm
                segs.append((r - t * tm, end - t * tm, None))
            r = end
        tiles.append(segs)
    return tiles


def _acc_resid_kernel(segments, k_last_valid, res_scale, a_ref, w_ref, x_hbm, gp_ref, gs_ref, o_ref,
                      x_ref, x_sem):
    i = pl.program_id(0)
    j = pl.program_id(1)
    k = pl.program_id(2)
    nk = pl.num_programs(2)
    tm, tk = a_ref.shape
    tn = o_ref.shape[1]

    def resid_copy():
        return pltpu.make_async_copy(
            x_hbm.at[pl.ds(i * tm, tm), pl.ds(j * tn, tn)], x_ref, x_sem)

    def sweep(kv, first):
        for r0 in range(0, tm, ACC_SUB_ROWS):
            rs = slice(r0, r0 + ACC_SUB_ROWS)
            part = _dot(a_ref[rs, 0:kv], w_ref[0:kv, :])
            if first:
                o_ref[rs, :] = part
            else:
                o_ref[rs, :] += part

    @pl.when(k == 0)
    def _():
        resid_copy().start()
        sweep(tk, True)

    @pl.when(jnp.logical_and(k > 0, k < nk - 1))
    def _():
        sweep(tk, False)

    @pl.when(k == nk - 1)
    def _():
        resid_copy().wait()

    for t, segs in enumerate(segments):
        @pl.when(jnp.logical_and(k == nk - 1, i == t))
        def _():
            for r0 in range(0, tm, ACC_SUB_ROWS):
                r1 = r0 + ACC_SUB_ROWS
                part = _dot(a_ref[r0:r1, 0:k_last_valid], w_ref[0:k_last_valid, :])
                for s0, s1, b in segs:
                    q0, q1 = max(s0, r0), min(s1, r1)
                    if q0 >= q1:
                        continue
                    gate = gs_ref[q0 - s0:q1 - s0, :] if b is None else gp_ref[b]
                    o_ref[q0:q1, :] = (x_ref[q0:q1, :]
                                       + (res_scale * gate) * (o_ref[q0:q1, :] + part[q0 - r0:q1 - r0, :]))


def _acc_resid(a, w_full, w_index, x, mods, mods_p, layer, which, sample_rows, res_scale):
    m, kdim = a.shape
    d = x.shape[1]
    tm, tk, tn = ACC_ROW_TILE, ACC_K_TILE, ACC_N_TILE
    n_bp = mods_p.shape[2]
    n_bs = mods.shape[2] - 8
    seq = (m - sample_rows) // n_bp
    assert m % tm == 0 and tm % ACC_SUB_ROWS == 0
    nk = pl.cdiv(kdim, tk)
    assert nk >= 2
    k_last_valid = kdim - (nk - 1) * tk
    lead = (None,) * len(w_index)
    segments = _gate_segments(tm, m // tm, n_bp, seq, n_bs)
    return pl.pallas_call(
        functools.partial(_acc_resid_kernel, segments, k_last_valid, res_scale),
        grid=(m // tm, d // tn, nk),
        in_specs=[
            pl.BlockSpec((tm, tk), lambda i, j, k: (i, k)),
            pl.BlockSpec(lead + (tk, tn), lambda i, j, k: tuple(w_index) + (k, j)),
            pl.BlockSpec(memory_space=pl.ANY),
            pl.BlockSpec((None, None, n_bp, 1, tn), lambda i, j, k: (layer, which, 0, 0, j)),
            pl.BlockSpec((None, None, n_bs, tn), lambda i, j, k: (layer, which, 0, j)),
        ],
        out_specs=pl.BlockSpec((tm, tn), lambda i, j, k: (i, j)),
        out_shape=jax.ShapeDtypeStruct((m, d), F32),
        scratch_shapes=[pltpu.VMEM((tm, tn), F32), pltpu.SemaphoreType.DMA(())],
        compiler_params=pltpu.CompilerParams(
            dimension_semantics=("arbitrary", "arbitrary", "arbitrary"),
            vmem_limit_bytes=ACC_VMEM_LIMIT),
        name="acc_resid",
    )(a, w_full, x, mods_p, mods)


N_HIST = max(POOL_WINDOWS) - 1
HALO = 32
LEVEL_ROW0 = 8


def _pool_prompt_kernel(start, u_ref, wp_ref, sp_ref, y_ref, hist_ref, z_ref, sa_ref, sb_ref):
    t = pl.program_id(1)
    nt = pl.num_programs(1)
    tt = u_ref.shape[0]
    group = wp_ref.shape[1]
    n = HALO + tt
    n_hist = hist_ref.shape[0]

    @pl.when(t == 0)
    def _():
        z_ref[0:HALO, :] = jnp.zeros((HALO, z_ref.shape[1]), F32)
        sa_ref[0:LEVEL_ROW0, :] = jnp.zeros((LEVEL_ROW0, group), F32)
        sb_ref[0:LEVEL_ROW0, :] = jnp.zeros((LEVEL_ROW0, group), F32)

    z_ref[HALO:n, :] = u_ref[...].astype(F32)

    def window_sum(cols, w):
        src, src_cols, sh, level = z_ref, cols, 1, 0
        while 2 * sh < w:
            dst = (sa_ref, sb_ref)[level % 2]
            dst[LEVEL_ROW0:n, :] = src[LEVEL_ROW0:n, src_cols] + src[LEVEL_ROW0 - sh:n - sh, src_cols]
            src, src_cols, sh, level = dst, slice(None), 2 * sh, level + 1
        return src[HALO:n, src_cols] + src[HALO - sh:n - sh, src_cols]

    pos = start + t * tt + lax.broadcasted_iota(jnp.int32, (tt, 1), 0)
    for g, w in enumerate(POOL_WINDOWS):
        assert w & (w - 1) == 0 and LEVEL_ROW0 + 2 * (w // 2 - 1) <= HALO
        cols = slice(g * group, (g + 1) * group)
        cur = z_ref[HALO:n, cols]
        cnt = jnp.minimum(w, pos + 1).astype(F32)
        pooled = window_sum(cols, w) / cnt - cur
        y = _dot(pooled.astype(BF16), wp_ref[g]) * sp_ref[:, cols]
        y_ref[:, cols] = y.astype(y_ref.dtype)

    @pl.when(t == nt - 1)
    def _():
        hist_ref[...] = z_ref[n - n_hist:n, :]

    z_ref[0:HALO, :] = z_ref[tt:n, :]


def _pool_prompt(proj, w_pool_l, s_pool_l, n_b, seq, d, tt=256):
    pw = s_pool_l.shape[-1]
    ntt = seq // tt
    n_groups, group, _ = w_pool_l.shape
    return pl.pallas_call(
        functools.partial(_pool_prompt_kernel, 0),
        grid=(n_b, ntt),
        in_specs=[
            pl.BlockSpec((tt, pw), lambda b, t: (b * ntt + t, 0)),
            pl.BlockSpec((n_groups, group, group), lambda b, t: (0, 0, 0)),
            pl.BlockSpec((1, pw), lambda b, t: (0, 0)),
        ],
        out_specs=[
            pl.BlockSpec((tt, pw), lambda b, t: (b * ntt + t, 0)),
            pl.BlockSpec((None, N_HIST, pw), lambda b, t: (b, 0, 0)),
        ],
        out_shape=[jax.ShapeDtypeStruct((proj.shape[0], d), BF16),
                   jax.ShapeDtypeStruct((n_b, N_HIST, pw), F32)],
        scratch_shapes=[pltpu.VMEM((HALO + tt, pw), F32),
                        pltpu.VMEM((HALO + tt, group), F32),
                        pltpu.VMEM((HALO + tt, group), F32)],
        compiler_params=_cparams(("arbitrary", "arbitrary")),
        name="pool_prompt",
    )(proj, w_pool_l, s_pool_l.reshape(1, pw))


def _pool_sample_kernel(hist_ref, u_ref, wp_ref, sp_ref, *rest):
    y_ref, nh_ref = rest[-2:]
    n_hist, n_b, _ = hist_ref.shape
    n_t = u_ref.shape[0] // n_b
    g = pl.program_id(0)

    def z(r):
        if r < n_hist:
            return hist_ref[r]
        return u_ref[(r - n_hist) * n_b:(r - n_hist + 1) * n_b, :].astype(F32)

    for r in range(n_hist):
        nh_ref[r] = z(r + n_t)

    for t in range(n_t):
        cur = z(n_hist + t)
        run = cur
        sums = []
        for j in range(1, POOL_WINDOWS[-1]):
            run = run + z(n_hist + t - j)
            if j + 1 in POOL_WINDOWS:
                sums.append(run)
        pooled = jnp.zeros_like(cur)
        for gi, w in enumerate(POOL_WINDOWS):
            pooled = jnp.where(g == gi, sums[gi] / float(w) - cur, pooled)
        y = _dot(pooled.astype(BF16), wp_ref[...]) * sp_ref[...]
        y_ref[t * n_b:(t + 1) * n_b, :] = y.astype(y_ref.dtype)


def _pool_sample(state_tm, layer, hist_prev, proj, w_pool_l, s_pool_l, ymix, row0):
    depth, n_hist, n_b, pw = state_tm.shape
    n_groups, group, _ = w_pool_l.shape
    rows = proj.shape[0] - row0
    rb = row0 // rows
    args = [state_tm, proj, w_pool_l, s_pool_l.reshape(1, pw), ymix]
    in_specs = [
        pl.BlockSpec((None, n_hist, n_b, group), lambda g: (layer, 0, 0, g)),
        pl.BlockSpec((rows, group), lambda g: (rb, g)),
        pl.BlockSpec((None, group, group), lambda g: (g, 0, 0)),
        pl.BlockSpec((1, group), lambda g: (0, g)),
        pl.BlockSpec(memory_space=pl.ANY),
    ]
    aliases = {4: 0}
    if hist_prev is not None:
        args.append(hist_prev)
        in_specs.append(pl.BlockSpec(memory_space=pl.ANY))
        aliases[5] = 1
    return pl.pallas_call(
        _pool_sample_kernel,
        grid=(n_groups,),
        in_specs=in_specs,
        out_specs=[pl.BlockSpec((rows, group), lambda g: (rb, g)),
                   pl.BlockSpec((None, n_hist, n_b, group), lambda g: (layer, 0, 0, g))],
        out_shape=[jax.ShapeDtypeStruct(ymix.shape, ymix.dtype),
                   jax.ShapeDtypeStruct(state_tm.shape, F32)],
        input_output_aliases=aliases,
        compiler_params=_cparams(("arbitrary",)),
        name="pool_sample",
    )(*args)


def _mlstm_segment(q, k, v, ig, lf, c_state, n_state, m_state):
    r = q.shape[0]
    row = lax.broadcasted_iota(jnp.int32, (r, r), 0)
    col = lax.broadcasted_iota(jnp.int32, (r, r), 1)
    causal = col <= row
    lf_rows = jnp.sum(jnp.where(row == col, lf, 0.0), axis=0, keepdims=True)
    ig_rows = jnp.sum(jnp.where(row == col, ig, 0.0), axis=0, keepdims=True)
    b_col = jnp.sum(jnp.where(causal, lf_rows, 0.0), axis=1, keepdims=True)
    b_rows = jnp.sum(jnp.where(row <= col, lf, 0.0), axis=0, keepdims=True)
    dmat = jnp.where(causal, b_col - b_rows + ig_rows, -jnp.inf)
    inter = b_col + m_state
    m_tok = jnp.maximum(inter, jnp.max(dmat, axis=-1, keepdims=True))
    w_intra = jnp.exp(dmat - m_tok)
    w_inter = jnp.exp(inter - m_tok)
    s = _dot_nt(q, k) * w_intra
    num = _dot(s.astype(BF16), v) + w_inter * _dot_nt(q, c_state)
    qn = jnp.sum(q.astype(F32) * n_state, axis=-1, keepdims=True)
    den = jnp.sum(s, axis=-1, keepdims=True) + w_inter * qn
    h = num * (1.0 / jnp.maximum(jnp.abs(den), jnp.exp(-m_tok)))
    b_last = jnp.sum(lf, axis=0, keepdims=True)
    dec = b_last - b_col + ig
    m_new = jnp.maximum(b_last + m_state, jnp.max(dec, axis=0, keepdims=True))
    ws = jnp.exp(dec - m_new)
    wc = jnp.exp(b_last + m_state - m_new)
    wk = ws * k.astype(F32)
    c_new = wc * c_state + _dot_tn(v, wk.astype(BF16))
    n_new = wc * n_state + jnp.sum(wk, axis=0, keepdims=True)
    return h, c_new, n_new, m_new


def _head_out(h, o, g_head):
    hn = h * lax.rsqrt(jnp.mean(h * h, axis=-1, keepdims=True) + EPS) * g_head
    return hn * jax.nn.sigmoid(o.astype(F32))


def _log_sigmoid(x):
    return jnp.minimum(x, 0.0) - jnp.log1p(jnp.exp(-jnp.abs(x)))


def _mlstm_prompt_kernel(q_ref, k_ref, v_ref, o_ref, gt_ref, gh_ref, ymix_in_ref,
                         y_ref, c_out_ref, n_out_ref, m_out_ref, c_s, n_s, m_s):
    del ymix_in_ref
    c = pl.program_id(1)
    nc = pl.num_programs(1)
    dk = c_s.shape[2]
    dv = c_s.shape[1]

    @pl.when(c == 0)
    def _():
        c_s[...] = jnp.zeros(c_s.shape, F32)
        n_s[...] = jnp.zeros(n_s.shape, F32)
        m_s[...] = jnp.zeros(m_s.shape, F32)

    gates = gt_ref[...]
    log_f = _log_sigmoid(gates)
    for h in range(N_HEADS):
        q = q_ref[:, h * dk:(h + 1) * dk]
        k = (k_ref[:, h * dk:(h + 1) * dk].astype(F32) * (dk ** -0.5)).astype(BF16)
        v = v_ref[:, h * dv:(h + 1) * dv]
        ig = gates[:, h:h + 1]
        lf = log_f[:, N_HEADS + h:N_HEADS + h + 1]
        hh, c_new, n_new, m_new = _mlstm_segment(
            q, k, v, ig, lf, c_s[h], n_s[h:h + 1, :], m_s[h:h + 1, 0:1])
        c_s[h] = c_new
        n_s[h:h + 1, :] = n_new
        m_s[h:h + 1, :] = jnp.broadcast_to(m_new, (1, m_s.shape[1]))
        y_ref[:, h * dv:(h + 1) * dv] = _head_out(
            hh, o_ref[:, h * dv:(h + 1) * dv], gh_ref[h:h + 1, :]).astype(y_ref.dtype)

    @pl.when(c == nc - 1)
    def _():
        c_out_ref[...] = c_s[...]
        n_out_ref[...] = n_s[0:N_HEADS, :]
        m_out_ref[...] = m_s[...]


def _mlstm_prompt(proj, gates, g_head_l, ymix, n_b, seq, col_q, dk, dv):
    lc = PROMPT_CHUNK
    nch = seq // lc
    qk_w = N_HEADS * dk
    v_w = N_HEADS * dv
    row = lambda b, c: b * nch + c
    assert col_q % qk_w == 0 and (col_q + 2 * qk_w) % v_w == 0
    bq = col_q // qk_w
    bv = (col_q + 2 * qk_w) // v_w
    y, c_out, n_out, m_out = pl.pallas_call(
        _mlstm_prompt_kernel,
        grid=(n_b, nch),
        in_specs=[
            pl.BlockSpec((lc, qk_w), lambda b, c: (row(b, c), bq)),
            pl.BlockSpec((lc, qk_w), lambda b, c: (row(b, c), bq + 1)),
            pl.BlockSpec((lc, v_w), lambda b, c: (row(b, c), bv)),
            pl.BlockSpec((lc, v_w), lambda b, c: (row(b, c), bv + 1)),
            pl.BlockSpec((lc, gates.shape[1]), lambda b, c: (row(b, c), 0)),
            pl.BlockSpec((N_HEADS, dv), lambda b, c: (0, 0)),
            pl.BlockSpec(memory_space=pl.ANY),
        ],
        out_specs=[
            pl.BlockSpec((lc, v_w), lambda b, c: (row(b, c), 1)),
            pl.BlockSpec((None, N_HEADS, dv, dk), lambda b, c: (b, 0, 0, 0)),
            pl.BlockSpec((None, N_HEADS, dk), lambda b, c: (b, 0, 0)),
            pl.BlockSpec((None, 8, 128), lambda b, c: (b, 0, 0)),
        ],
        out_shape=[jax.ShapeDtypeStruct(ymix.shape, ymix.dtype),
                   jax.ShapeDtypeStruct((n_b, N_HEADS, dv, dk), F32),
                   jax.ShapeDtypeStruct((n_b, N_HEADS, dk), F32),
                   jax.ShapeDtypeStruct((n_b, 8, 128), F32)],
        scratch_shapes=[pltpu.VMEM((N_HEADS, dv, dk), F32),
                        pltpu.VMEM((8, dk), F32),
                        pltpu.VMEM((8, 128), F32)],
        input_output_aliases={6: 0},
        compiler_params=_cparams(("arbitrary", "arbitrary")),
        name="mlstm_prompt",
    )(proj, proj, proj, proj, gates, g_head_l, ymix)
    return y, c_out, n_out, m_out[:, :N_HEADS, 0]


def _mlstm_sample_kernel(seq, q_ref, k_ref, v_ref, o_ref, gt_ref, gh_ref, c0_ref, n0_ref, m0_ref,
                         *rest):
    y_ref, c_out_ref, n_out_ref, m_out_ref = rest[-4:]
    rows = q_ref.shape[0]
    dk = c0_ref.shape[3]
    dv = c0_ref.shape[2]
    seg_of_row = lax.broadcasted_iota(jnp.int32, (rows, 1), 0) // seq
    gates = gt_ref[...]
    log_f = _log_sigmoid(gates)
    for h in range(N_HEADS):
        q = q_ref[:, h * dk:(h + 1) * dk]
        k = (k_ref[:, h * dk:(h + 1) * dk].astype(F32) * (dk ** -0.5)).astype(BF16)
        v = v_ref[:, h * dv:(h + 1) * dv]
        ig_all = gates[:, h:h + 1]
        lf_all = log_f[:, N_HEADS + h:N_HEADS + h + 1]
        hh = jnp.zeros((rows, dv), F32)
        for sgm in range(rows // seq):
            mine = seg_of_row == sgm
            ig = jnp.where(mine, ig_all, NEG_BIG)
            lf = jnp.where(mine, lf_all, 0.0)
            h_seg, c_new, n_new, m_new = _mlstm_segment(
                q, k, v, ig, lf, c0_ref[sgm, h], n0_ref[sgm, h:h + 1, :], m0_ref[sgm, h:h + 1, :])
            hh = jnp.where(mine, h_seg, hh)
            c_out_ref[sgm, h] = c_new
            n_out_ref[sgm, h:h + 1, :] = n_new
            m_out_ref[sgm, h:h + 1, :] = m_new
        y_ref[:, h * dv:(h + 1) * dv] = _head_out(
            hh, o_ref[:, h * dv:(h + 1) * dv], gh_ref[h:h + 1, :]).astype(y_ref.dtype)


def _mlstm_sample(proj_bm, gates_bm, g_head_l, state_c, state_n, state_m, layer, c_prev, seq, col_q, dk, dv):
    depth, n_b = state_c.shape[:2]
    grp = SAMPLE_GROUP
    rows = grp * seq
    qk_w = N_HEADS * dk
    v_w = N_HEADS * dv
    bq = col_q // qk_w
    bv = (col_q + 2 * qk_w) // v_w
    has_prev = c_prev is not None
    in_specs = [
        pl.BlockSpec((rows, qk_w), lambda i: (i, bq)),
        pl.BlockSpec((rows, qk_w), lambda i: (i, bq + 1)),
        pl.BlockSpec((rows, v_w), lambda i: (i, bv)),
        pl.BlockSpec((rows, v_w), lambda i: (i, bv + 1)),
        pl.BlockSpec((rows, gates_bm.shape[1]), lambda i: (i, 0)),
        pl.BlockSpec((N_HEADS, dv), lambda i: (0, 0)),
        pl.BlockSpec((None, grp, N_HEADS, dv, dk), lambda i: (layer, i, 0, 0, 0)),
        pl.BlockSpec((None, grp, N_HEADS, dk), lambda i: (layer, i, 0, 0)),
        pl.BlockSpec((None, grp, N_HEADS, 1), lambda i: (layer, i, 0, 0)),
    ]
    args = [proj_bm, proj_bm, proj_bm, proj_bm, gates_bm, g_head_l, state_c, state_n,
            state_m.reshape(depth, n_b, N_HEADS, 1)]
    aliases = {}
    if has_prev:
        in_specs.append(pl.BlockSpec(memory_space=pl.ANY))
        args.append(c_prev)
        aliases = {len(args) - 1: 1}
    y, c_out, n_out, m_out = pl.pallas_call(
        functools.partial(_mlstm_sample_kernel, seq),
        grid=(n_b // grp,),
        in_specs=in_specs,
        out_specs=[
            pl.BlockSpec((rows, v_w), lambda i: (i, 0)),
            pl.BlockSpec((None, grp, N_HEADS, dv, dk), lambda i: (layer, i, 0, 0, 0)),
            pl.BlockSpec((grp, N_HEADS, dk), lambda i: (i, 0, 0)),
            pl.BlockSpec((grp, N_HEADS, 1), lambda i: (i, 0, 0)),
        ],
        out_shape=[jax.ShapeDtypeStruct((n_b * seq, v_w), BF16),
                   jax.ShapeDtypeStruct(state_c.shape, F32),
                   jax.ShapeDtypeStruct((n_b, N_HEADS, dk), F32),
                   jax.ShapeDtypeStruct((n_b, N_HEADS, 1), F32)],
        input_output_aliases=aliases,
        compiler_params=_cparams(("arbitrary",)),
        name="mlstm_sample",
    )(*args)
    return y, c_out, n_out, m_out[:, :, 0]


def _to_time_major(a, n_b, seq):
    return jnp.transpose(a.reshape(n_b, seq, -1), (1, 0, 2)).reshape(n_b * seq, -1)


def _to_batch_major(a, n_b, seq):
    return jnp.transpose(a.reshape(seq, n_b, -1), (1, 0, 2)).reshape(n_b * seq, -1)


def kernel(x_prompt, x_sample, state_pool, state_C, state_n, state_m, c_prompt, c_sample,
           w_ada, b_ada, g_norm, w_in, b_in, w_pool, s_pool, g_head, w_out, w1, w3, w2, g_final):
    n_bp, seq_p, d = x_prompt.shape
    n_bs, seq_s, _ = x_sample.shape
    depth = w_ada.shape[0]
    pw = s_pool.shape[-1]
    dv = g_head.shape[-1]
    dk = state_C.shape[-1]
    n_hist = state_pool.shape[2]
    rows_p = n_bp * seq_p
    rows_s = n_bs * seq_s
    n_main = pw + 2 * N_HEADS * dk + 2 * N_HEADS * dv

    xp0 = x_prompt.reshape(rows_p, d)
    xs0 = _to_time_major(x_sample, n_bs, seq_s)

    c_all = jnp.concatenate([c_sample, c_prompt, jnp.zeros((8 - n_bp, d), F32)], axis=0)
    b_ada3 = b_ada.reshape(depth, 1, b_ada.shape[-1])
    tiles_per_slab = d // ADA_TILE
    tiles_per_layer = N_SUB * N_MOD * tiles_per_slab
    assert depth == 2
    mods = _ada_head(c_all, w_ada, b_ada3, N_MOD)
    ada_jobs = {(0, 0): (N_MOD * tiles_per_slab, tiles_per_layer - N_MOD * tiles_per_slab),
                (0, 1): (tiles_per_layer, tiles_per_layer)}

    def prompt_rows(mods):
        return mods[:, :, n_bs:n_bs + n_bp][:, :, :, None, :]

    b_in3 = b_in.reshape(depth, 1, b_in.shape[-1])
    w_in_t = jnp.transpose(w_in, (0, 2, 1))
    pool_p, c_p, n_p, m_p = [], [], [], []
    n_s, m_s = [], []
    c_s_all = None
    hist_s_tm = None
    state_pool_tm = jnp.transpose(state_pool, (0, 2, 1, 3))

    def ffn(x, h, mods, l, sub_layer, ffn_idx):
        job = ada_jobs.get((l, ffn_idx))
        if job is None:
            act = _swiglu_up(h, w1, w3, l, ffn_idx)
        else:
            act, mods = _swiglu_up(h, w1, w3, l, ffn_idx, (c_all, w_ada, b_ada3, mods) + job)
        x = _acc_resid(act, w2, (l, ffn_idx), x, mods, prompt_rows(mods), l, sub_layer * N_MOD + 2,
                       rows_s, FFN_RES)
        return x, mods

    x = None
    for l in range(depth):
        mods_p = prompt_rows(mods)
        if l == 0:
            h, x = _normmod(xp0, xs0, 0, g_norm[l, 0], mods, mods_p, l, 0, seq_p, True)
        else:
            h, = _normmod(x, x, rows_p, g_norm[l, 0], mods, mods_p, l, 0, seq_p, False)
        x, mods = ffn(x, h, mods, l, 0, 0)
        mods_p = prompt_rows(mods)

        h, = _normmod(x, x, rows_p, g_norm[l, 1], mods, mods_p, l, 1, seq_p, False)
        proj = _in_proj(h, w_in_t, b_in3, l, n_main)
        gates = _gate_proj(h, w_in_t, b_in3, l, n_main)

        ymix, hist_p = _pool_prompt(proj, w_pool[l], s_pool[l], n_bp, seq_p, d)
        ymix, cp, np_, mp = _mlstm_prompt(proj, gates, g_head[l], ymix, n_bp, seq_p, pw, dk, dv)
        ymix, hist_s_tm = _pool_sample(state_pool_tm, l, hist_s_tm, proj, w_pool[l], s_pool[l], ymix, rows_p)
        proj_bm = _to_batch_major(proj[rows_p:], n_bs, seq_s)
        gates_bm = _to_batch_major(gates[rows_p:], n_bs, seq_s)
        y_ms, c_s_all, ns, ms = _mlstm_sample(proj_bm, gates_bm, g_head[l], state_C, state_n, state_m,
                                              l, c_s_all, seq_s, pw, dk, dv)
        ymix = lax.dynamic_update_slice(ymix, _to_time_major(y_ms, n_bs, seq_s), (rows_p, pw))
        x = _acc_resid(ymix, w_out, (l,), x, mods, mods_p, l, 1 * N_MOD + 2, rows_s, 1.0)

        pool_p.append(hist_p); c_p.append(cp); n_p.append(np_); m_p.append(mp)
        n_s.append(ns); m_s.append(ms)

        h, = _normmod(x, x, rows_p, g_norm[l, 2], mods, mods_p, l, 2, seq_p, False)
        x, mods = ffn(x, h, mods, l, 2, 1)

    y_prompt = _final_norm(x, g_final, 0, rows_p).reshape(n_bp, seq_p, d)
    y_sample = _to_batch_major(_final_norm(x, g_final, rows_p, rows_s), n_bs, seq_s).reshape(n_bs, seq_s, d)
    return (y_prompt, y_sample,
            jnp.stack(pool_p), jnp.stack(c_p), jnp.stack(n_p), jnp.stack(m_p),
            jnp.transpose(hist_s_tm, (0, 2, 1, 3)), c_s_all, jnp.stack(n_s), jnp.stack(m_s))
```
